```python
import math
import jax, jax.numpy as jnp
from jax import lax
import numpy as np

D_MODEL = 1024
BATCH = 8
SEQ = 4096
DEPTH = 2

DA_HEADS = 4
DA_HEAD_DIM = 64
DA_V_DIM = 2 * DA_HEAD_DIM
DA_WIDTH = DA_HEADS * DA_V_DIM
NSA_HEADS = 8
NSA_KV_GROUPS = 2
NSA_HPG = NSA_HEADS // NSA_KV_GROUPS
NSA_HEAD_DIM = 64
NSA_WIDTH = NSA_HEADS * NSA_HEAD_DIM
CMP_BLOCK = 32
CMP_STRIDE = 16
CMP_HIDDEN = 128
SEL_BLOCK = 64
SEL_TOPK = 8
SEL_FORCE = 1e4
WINDOW = 512
N_BUCKETS = 32
MAX_DISTANCE = 1024
TOTAL_HEADS = DA_HEADS + NSA_HEADS
D_FF = 2816
Q_BLOCK = 128
EPS = 1e-6
NEG_INF = -1e30
NSA_KV = NSA_KV_GROUPS * NSA_HEAD_DIM
PROJ_SIZES = (
    DA_HEADS * 2 * DA_HEAD_DIM,
    DA_HEADS * 2 * DA_HEAD_DIM,
    DA_WIDTH,
    NSA_WIDTH,
    NSA_KV, NSA_KV,
    NSA_KV, NSA_KV,
    NSA_KV, NSA_KV,
    3 * NSA_HEADS,
    2 * D_MODEL,
)
N_IN = sum(PROJ_SIZES)

kernel_name = "hybrid_diffattn_nsa_macaron"


def rmsnorm(x, g):
    xf = x.astype(jnp.float32)
    y = xf * lax.rsqrt(jnp.mean(xf * xf, axis=-1, keepdims=True) + EPS)
    return (y * g.astype(jnp.float32)).astype(x.dtype)


def swiglu(h, w1, w3, w2):
    return (jax.nn.silu(h @ w1) * (h @ w3)) @ w2


def masked_softmax(logits, mask):
    s = jnp.where(mask, logits.astype(jnp.float32), NEG_INF)
    p = jax.nn.softmax(s, axis=-1)
    return jnp.where(mask, p, 0.0)


def rel_bucket(dist):
    n = jnp.maximum(dist, 0)
    max_exact = N_BUCKETS // 2
    nf = jnp.maximum(n, 1).astype(jnp.float32)
    large = max_exact + (jnp.log(nf / max_exact) / math.log(MAX_DISTANCE / max_exact)
                         * (N_BUCKETS - max_exact)).astype(jnp.int32)
    large = jnp.minimum(large, N_BUCKETS - 1)
    return jnp.where(n < max_exact, n, large)


def diff_attention(q, k, v, lam, table):
    B, S = q.shape[:2]
    nb = S // Q_BLOCK
    scale = DA_HEAD_DIM ** -0.5
    kpos = jnp.arange(S)
    qb = q.reshape(B, nb, Q_BLOCK, DA_HEADS, 2, DA_HEAD_DIM).swapaxes(0, 1)

    def block(args):
        qi, i = args
        qpos = i * Q_BLOCK + jnp.arange(Q_BLOCK)
        dist = qpos[:, None] - kpos[None, :]
        bias = jnp.take(table, rel_bucket(dist), axis=0).transpose(2, 0, 1)
        s = jnp.einsum('bqhcd,bkhcd->bhcqk', qi, k).astype(jnp.float32) * scale
        s = s + bias[None, :, None].astype(jnp.float32)
        p = masked_softmax(s, dist >= 0)
        a = p[:, :, 0] - lam * p[:, :, 1]
        return jnp.einsum('bhqk,bkhe->bqhe', a.astype(v.dtype), v)

    out = lax.map(block, (qb, jnp.arange(nb)))
    return out.swapaxes(0, 1).reshape(B, S, DA_HEADS, DA_V_DIM)


def compress(x, pe, w1, w2):
    B, S, G, d = x.shape
    ratio = CMP_BLOCK // CMP_STRIDE
    n_cmp = S // CMP_STRIDE - ratio + 1
    c = x.reshape(B, S // CMP_STRIDE, CMP_STRIDE, G, d)
    blocks = jnp.concatenate([c[:, r:r + n_cmp] for r in range(ratio)], axis=2)
    blocks = blocks + pe[None, None, :, None, :]
    flat = blocks.transpose(0, 1, 3, 2, 4).reshape(B, n_cmp, G, CMP_BLOCK * d)
    return jax.nn.gelu(flat @ w1) @ w2


def nsa_attention(q, kc, vc, ks, vs, kw, vw, gates, table):
    B, S = q.shape[:2]
    G, P, d = NSA_KV_GROUPS, NSA_HPG, NSA_HEAD_DIM
    nb = S // Q_BLOCK
    n_cmp = kc.shape[1]
    n_sel = S // SEL_BLOCK
    topk = min(SEL_TOPK, n_sel)
    scale = d ** -0.5
    table_h = table.reshape(N_BUCKETS, G, P)
    cmp_start = jnp.arange(n_cmp) * CMP_STRIDE
    cmp_end = cmp_start + CMP_BLOCK - 1
    sel_j = jnp.arange(n_sel)
    overlap = ((cmp_start[:, None] < (sel_j[None, :] + 1) * SEL_BLOCK)
               & (cmp_end[:, None] >= sel_j[None, :] * SEL_BLOCK)).astype(jnp.float32)
    ks_blocks = ks.reshape(B, n_sel, SEL_BLOCK, G, d).transpose(0, 3, 1, 2, 4)
    vs_blocks = vs.reshape(B, n_sel, SEL_BLOCK, G, d).transpose(0, 3, 1, 2, 4)
    kw_pad = jnp.pad(kw, ((0, 0), (WINDOW, 0), (0, 0), (0, 0)))
    vw_pad = jnp.pad(vw, ((0, 0), (WINDOW, 0), (0, 0), (0, 0)))
    gather = jax.vmap(jax.vmap(lambda t, i: t[i]))
    group_bias = jax.vmap(lambda t, bk: t[bk], in_axes=(1, 1), out_axes=1)
    qb = q.reshape(B, nb, Q_BLOCK, G, P, d).swapaxes(0, 1)
    gb = gates.reshape(B, nb, Q_BLOCK, G, P, 3).swapaxes(0, 1)
    n_keys = topk * SEL_BLOCK

    def block(args):
        qi, gi, i = args
        qpos = i * Q_BLOCK + jnp.arange(Q_BLOCK)
        dist_c = qpos[:, None] - cmp_end[None, :]
        bias_c = table_h[rel_bucket(dist_c)].transpose(2, 3, 0, 1)
        s_c = jnp.einsum('bqgpd,bcgd->bgpqc', qi, kc).astype(jnp.float32) * scale + bias_c
        p_c = masked_softmax(s_c, dist_c >= 0)
        o_c = jnp.einsum('bgpqc,bcgd->bqgpd', p_c.astype(vc.dtype), vc)
        imp = jnp.einsum('bgpqc,cj->bgqj', p_c, overlap)
        cur = qpos // SEL_BLOCK
        forced = ((sel_j[None, :] == 0) | (sel_j[None, :] == cur[:, None])
                  | (sel_j[None, :] == cur[:, None] - 1))
        future = sel_j[None, :] > cur[:, None]
        score = jnp.where(future, -SEL_FORCE, imp + jnp.where(forced, SEL_FORCE, 0.0))
        _, idx = lax.top_k(score, topk)
        k_g = gather(ks_blocks, idx).reshape(B, G, Q_BLOCK, n_keys, d)
        v_g = gather(vs_blocks, idx).reshape(B, G, Q_BLOCK, n_keys, d)
        tpos = (idx[..., None] * SEL_BLOCK + jnp.arange(SEL_BLOCK)).reshape(B, G, Q_BLOCK, n_keys)
        dist_s = qpos[None, None, :, None] - tpos
        bias_s = group_bias(table_h, rel_bucket(dist_s)).transpose(0, 1, 4, 2, 3)
        s_s = jnp.einsum('bqgpd,bgqkd->bgpqk', qi, k_g).astype(jnp.float32) * scale + bias_s
        p_s = masked_softmax(s_s, (dist_s >= 0)[:, :, None])
        o_s = jnp.einsum('bgpqk,bgqkd->bqgpd', p_s.astype(v_g.dtype), v_g)
        kw_i = lax.dynamic_slice_in_dim(kw_pad, i * Q_BLOCK, Q_BLOCK + WINDOW, axis=1)
        vw_i = lax.dynamic_slice_in_dim(vw_pad, i * Q_BLOCK, Q_BLOCK + WINDOW, axis=1)
        kpos_w = i * Q_BLOCK - WINDOW + jnp.arange(Q_BLOCK + WINDOW)
        dist_w = qpos[:, None] - kpos_w[None, :]
        mask_w = (dist_w >= 0) & (dist_w < WINDOW) & (kpos_w[None, :] >= 0)
        bias_w = table_h[rel_bucket(dist_w)].transpose(2, 3, 0, 1)
        s_w = jnp.einsum('bqgpd,bkgd->bgpqk', qi, kw_i).astype(jnp.float32) * scale + bias_w
        p_w = masked_softmax(s_w, mask_w)
        o_w = jnp.einsum('bgpqk,bkgd->bqgpd', p_w.astype(vw_i.dtype), vw_i)
        return gi[..., 0:1] * o_c + gi[..., 1:2] * o_s + gi[..., 2:3] * o_w

    out = lax.map(block, (qb, gb, jnp.arange(nb)))
    return out.swapaxes(0, 1).reshape(B, S, NSA_WIDTH)


def setup_inputs(seed: int = 0) -> dict:
    key = jax.random.key(seed)
    ks = jax.random.split(key, 30)
    f32 = jnp.float32

    def nrm(k, shape, scale):
        return jax.random.normal(k, shape, f32) * scale

    def gain(k, shape):
        return 1.0 + 0.05 * jax.random.normal(k, shape, f32)

    d = D_MODEL
    return {
        "x": nrm(ks[0], (BATCH, SEQ, d), 1.0),
        "w_in": nrm(ks[1], (DEPTH, d, N_IN), d ** -0.5),
        "w_branch_a": nrm(ks[2], (DEPTH, DA_WIDTH, d), DA_WIDTH ** -0.5),
        "w_branch_b": nrm(ks[3], (DEPTH, NSA_WIDTH, d), NSA_WIDTH ** -0.5),
        "w_out": nrm(ks[4], (DEPTH, d, d), d ** -0.5),
        "norm_ffn1": gain(ks[5], (DEPTH, d)),
        "norm_mix": gain(ks[6], (DEPTH, d)),
        "norm_ffn2": gain(ks[7], (DEPTH, d)),
        "ffn1_w1": nrm(ks[8], (DEPTH, d, D_FF), d ** -0.5),
        "ffn1_w3": nrm(ks[9], (DEPTH, d, D_FF), d ** -0.5),
        "ffn1_w2": nrm(ks[10], (DEPTH, D_FF, d), D_FF ** -0.5),
        "ffn2_w1": nrm(ks[11], (DEPTH, d, D_FF), d ** -0.5),
        "ffn2_w3": nrm(ks[12], (DEPTH, d, D_FF), d ** -0.5),
        "ffn2_w2": nrm(ks[13], (DEPTH, D_FF, d), D_FF ** -0.5),
        "da_q_gain": gain(ks[14], (DEPTH, DA_HEAD_DIM)),
        "da_k_gain": gain(ks[15], (DEPTH, DA_HEAD_DIM)),
        "da_lambda_q1": nrm(ks[16], (DEPTH, DA_HEAD_DIM), 0.1),
        "da_lambda_k1": nrm(ks[17], (DEPTH, DA_HEAD_DIM), 0.1),
        "da_lambda_q2": nrm(ks[18], (DEPTH, DA_HEAD_DIM), 0.1),
        "da_lambda_k2": nrm(ks[19], (DEPTH, DA_HEAD_DIM), 0.1),
        "da_subln_gain": gain(ks[20], (DEPTH, DA_V_DIM)),
        "nsa_q_gain": gain(ks[21], (DEPTH, NSA_HEAD_DIM)),
        "nsa_k_gain": gain(ks[22], (DEPTH, 3, NSA_HEAD_DIM)),
        "cmp_pe_k": nrm(ks[23], (DEPTH, CMP_BLOCK, NSA_HEAD_DIM), 0.1),
        "cmp_w1_k": nrm(ks[24], (DEPTH, CMP_BLOCK * NSA_HEAD_DIM, CMP_HIDDEN), (CMP_BLOCK * NSA_HEAD_DIM) ** -0.5),
        "cmp_w2_k": nrm(ks[25], (DEPTH, CMP_HIDDEN, NSA_HEAD_DIM), CMP_HIDDEN ** -0.5),
        "cmp_pe_v": nrm(ks[26], (DEPTH, CMP_BLOCK, NSA_HEAD_DIM), 0.1),
        "cmp_w1_v": nrm(ks[27], (DEPTH, CMP_BLOCK * NSA_HEAD_DIM, CMP_HIDDEN), (CMP_BLOCK * NSA_HEAD_DIM) ** -0.5),
        "cmp_w2_v": nrm(ks[28], (DEPTH, CMP_HIDDEN, NSA_HEAD_DIM), CMP_HIDDEN ** -0.5),
        "rel_bias_table": nrm(ks[29], (N_BUCKETS, TOTAL_HEADS), 0.3),
    }


def reference(x, w_in, w_branch_a, w_branch_b, w_out, norm_ffn1, norm_mix, norm_ffn2,
              ffn1_w1, ffn1_w3, ffn1_w2, ffn2_w1, ffn2_w3, ffn2_w2,
              da_q_gain, da_k_gain, da_lambda_q1, da_lambda_k1, da_lambda_q2, da_lambda_k2,
              da_subln_gain, nsa_q_gain, nsa_k_gain,
              cmp_pe_k, cmp_w1_k, cmp_w2_k, cmp_pe_v, cmp_w1_v, cmp_w2_v,
              rel_bias_table):
    B, S, D = x.shape
    G, P, dh = NSA_KV_GROUPS, NSA_HPG, NSA_HEAD_DIM
    offsets = [int(o) for o in np.cumsum(PROJ_SIZES)[:-1]]
    table_a = rel_bias_table[:, :DA_HEADS]
    table_b = rel_bias_table[:, DA_HEADS:]
    for l in range(DEPTH):
        h = rmsnorm(x, norm_ffn1[l])
        x = x + 0.5 * swiglu(h, ffn1_w1[l], ffn1_w3[l], ffn1_w2[l])
        h = rmsnorm(x, norm_mix[l])
        z = h @ w_in[l]
        (dq, dk, dv, nq, kc, vc, ksel, vsel, kwin, vwin, ng, mg) = jnp.split(z, offsets, axis=-1)
        dq = rmsnorm(dq.reshape(B, S, DA_HEADS, 2, DA_HEAD_DIM), da_q_gain[l])
        dk = rmsnorm(dk.reshape(B, S, DA_HEADS, 2, DA_HEAD_DIM), da_k_gain[l])
        dv = dv.reshape(B, S, DA_HEADS, DA_V_DIM)
        lam_init = 0.8 - 0.6 * math.exp(-0.3 * l)
        lam = (jnp.exp(jnp.sum(da_lambda_q1[l].astype(jnp.float32) * da_lambda_k1[l].astype(jnp.float32)))
               - jnp.exp(jnp.sum(da_lambda_q2[l].astype(jnp.float32) * da_lambda_k2[l].astype(jnp.float32)))
               + lam_init)
        ya = diff_attention(dq, dk, dv, lam, table_a)
        ya = (rmsnorm(ya, da_subln_gain[l]) * (1.0 - lam_init)).reshape(B, S, DA_WIDTH)
        nq = rmsnorm(nq.reshape(B, S, G, P, dh), nsa_q_gain[l])
        kc = rmsnorm(compress(kc.reshape(B, S, G, dh), cmp_pe_k[l], cmp_w1_k[l], cmp_w2_k[l]), nsa_k_gain[l, 0])
        vc = compress(vc.reshape(B, S, G, dh), cmp_pe_v[l], cmp_w1_v[l], cmp_w2_v[l])
        ksel = rmsnorm(ksel.reshape(B, S, G, dh), nsa_k_gain[l, 1])
        vsel = vsel.reshape(B, S, G, dh)
        kwin = rmsnorm(kwin.reshape(B, S, G, dh), nsa_k_gain[l, 2])
        vwin = vwin.reshape(B, S, G, dh)
        ng = jax.nn.sigmoid(ng.reshape(B, S, G, P, 3))
        yb = nsa_attention(nq, kc, vc, ksel, vsel, kwin, vwin, ng, table_b)
        mg = jax.nn.sigmoid(mg.reshape(B, S, 2, D))
        merged = mg[:, :, 0] * (ya @ w_branch_a[l]) + mg[:, :, 1] * (yb @ w_branch_b[l])
        x = x + merged @ w_out[l]
        h = rmsnorm(x, norm_ffn2[l])
        x = x + 0.5 * swiglu(h, ffn2_w1[l], ffn2_w3[l], ffn2_w2[l])
    return x
```

```python
import functools
import math

import numpy as np
import jax
import jax.numpy as jnp
from jax import lax
from jax.experimental import pallas as pl
from jax.experimental.pallas import tpu as pltpu

F32 = jnp.float32
BF16 = jnp.bfloat16

D_MODEL = 1024
DA_HEADS = 4
DA_HEAD_DIM = 64
DA_V_DIM = 2 * DA_HEAD_DIM
DA_WIDTH = DA_HEADS * DA_V_DIM
NSA_HEADS = 8
NSA_G = 2
NSA_P = NSA_HEADS // NSA_G
NSA_D = 64
NSA_WIDTH = NSA_HEADS * NSA_D
NSA_KV = NSA_G * NSA_D
CMP_BLOCK = 32
CMP_STRIDE = 16
CMP_HIDDEN = 128
SEL_BLOCK = 64
SEL_TOPK = 8
SEL_FORCE = 1e4
WINDOW = 512
N_BUCKETS = 32
MAX_DISTANCE = 1024
D_FF = 2816
EPS = 1e-6
NEG = -1e30

LANES = 128
VMEM_LIMIT = 52 * 1024 * 1024

ROW_TILE = 256
DA_T = 256
NSA_T = 128
NSA_SEL_LANE0 = 64

PROJ_WIDTH = 4 * 512 + 2 * NSA_KV + 4 * NSA_KV + NSA_G * LANES


def _cparams(sem):
    return pltpu.CompilerParams(dimension_semantics=sem, vmem_limit_bytes=VMEM_LIMIT)


def _const_spec(shape):
    nd = len(shape)
    return pl.BlockSpec(shape, lambda *_: (0,) * nd)


def _rms(xf, g):
    ms = jnp.mean(xf * xf, axis=-1, keepdims=True)
    return xf * lax.rsqrt(ms + EPS) * g


def _dot(a, b):
    return jnp.dot(a, b, preferred_element_type=F32)


def _dot_nt(a, b):
    return lax.dot_general(a, b, (((1,), (1,)), ((), ())), preferred_element_type=F32)


def _split_dot(x, w, parts):
    out = None
    r = x
    for i in range(parts):
        piece = r.astype(BF16)
        d = _dot(piece, w)
        out = d if out is None else out + d
        if i + 1 < parts:
            r = r - piece.astype(F32)
    return out


def _ffn_kernel(x_ref, g_ref, w1_ref, w3_ref, w2_ref, o_ref):
    x = x_ref[...]
    h = _rms(x, g_ref[...]).astype(BF16)
    a = _dot(h, w1_ref[...])
    b = _dot(h, w3_ref[...])
    t = (jax.nn.silu(a) * b).astype(BF16)
    o_ref[...] = x + 0.5 * _dot(t, w2_ref[...])


def _ffn(x2, g, w1, w3, w2):
    n, d = x2.shape
    tm = ROW_TILE
    return pl.pallas_call(
        _ffn_kernel,
        grid=(n // tm,),
        in_specs=[
            pl.BlockSpec((tm, d), lambda i: (i, 0)),
            _const_spec((1, d)),
            _const_spec(w1.shape),
            _const_spec(w3.shape),
            _const_spec(w2.shape),
        ],
        out_specs=pl.BlockSpec((tm, d), lambda i: (i, 0)),
        out_shape=jax.ShapeDtypeStruct((n, d), F32),
        compiler_params=_cparams(("parallel",)),
        name="ffn_half_step",
    )(x2, g, w1, w3, w2)


def _group_rms(z, r_ref, gain):
    w = z.shape[-1]
    ss = _split_dot(z * z, r_ref[0:w, 0:w], 2)
    return z * lax.rsqrt(ss * (1.0 / 64.0) + EPS) * gain


def _proj_kernel(x_ref, g_ref, w_ref, r_ref, gq_ref, gk_ref, gnq_ref, gks_ref, gkw_ref,
                 dq_ref, dk_ref, dv_ref, nq_ref, kcvc_ref, nkv_ref, ng_ref):
    h = _rms(x_ref[...], g_ref[...]).astype(BF16)
    z = _dot(h, w_ref[...])
    scale = DA_HEAD_DIM ** -0.5
    dq_ref[...] = (_group_rms(z[:, 0:512], r_ref, gq_ref[...]) * scale).astype(BF16)
    dk_ref[...] = _group_rms(z[:, 512:1024], r_ref, gk_ref[...]).astype(BF16)
    dv_ref[...] = z[:, 1024:1536].astype(BF16)
    nq_ref[...] = (_group_rms(z[:, 1536:2048], r_ref, gnq_ref[...]) * scale).astype(BF16)
    kcvc_ref[...] = z[:, 2048:2304]
    nkv_ref[:, 0:128] = _group_rms(z[:, 2304:2432], r_ref, gks_ref[...]).astype(BF16)
    nkv_ref[:, 128:256] = z[:, 2432:2560].astype(BF16)
    nkv_ref[:, 256:384] = _group_rms(z[:, 2560:2688], r_ref, gkw_ref[...]).astype(BF16)
    nkv_ref[:, 384:512] = z[:, 2688:2816].astype(BF16)
    ng_ref[...] = jax.nn.sigmoid(z[:, 2816:3072])


def _proj(x2, g, w, r, gq, gk, gnq, gks, gkw):
    n, d = x2.shape
    tm = ROW_TILE
    row = lambda wd: pl.BlockSpec((tm, wd), lambda i: (i, 0))
    return pl.pallas_call(
        _proj_kernel,
        grid=(n // tm,),
        in_specs=[row(d), _const_spec((1, d)), _const_spec(w.shape), _const_spec(r.shape),
                  _const_spec((1, 512)), _const_spec((1, 512)), _const_spec((1, 512)),
                  _const_spec((1, 128)), _const_spec((1, 128))],
        out_specs=[row(512), row(512), row(512), row(512), row(256), row(512), row(256)],
        out_shape=[
            jax.ShapeDtypeStruct((n, 512), BF16),
            jax.ShapeDtypeStruct((n, 512), BF16),
            jax.ShapeDtypeStruct((n, 512), BF16),
            jax.ShapeDtypeStruct((n, 512), BF16),
            jax.ShapeDtypeStruct((n, 256), F32),
            jax.ShapeDtypeStruct((n, 512), BF16),
            jax.ShapeDtypeStruct((n, 256), F32),
        ],
        compiler_params=_cparams(("parallel",)),
        name="norm_in_proj",
    )(x2, g, w, r, gq, gk, gnq, gks, gkw)


def _cmp_kernel(ck_ref, cv_ref, pek_ref, pev_ref, w1k_ref, w1v_ref, w2k_ref, w2v_ref, gk_ref, o_ref):
    def mlp(c, pe_ref, w1_ref, w2_ref):
        lo = _dot((c + pe_ref[0:1, :]).astype(BF16), w1_ref[0])
        hi = _dot((c + pe_ref[1:2, :]).astype(BF16), w1_ref[1])
        nrow = hi.shape[0]
        hid = jax.nn.gelu(lo + pltpu.roll(hi, nrow - 1, axis=0))
        return _dot(hid.astype(BF16), w2_ref[...])
    kc = mlp(ck_ref[0, 0], pek_ref, w1k_ref, w2k_ref)
    vc = mlp(cv_ref[0, 0], pev_ref, w1v_ref, w2v_ref)
    ms = jnp.sum(kc * kc, axis=-1, keepdims=True) * (1.0 / NSA_D)
    kc = kc * lax.rsqrt(ms + EPS) * gk_ref[...]
    o_ref[0, 0] = (kc + vc).astype(BF16)


def _compress(ck, cv, pek, pev, w1k, w1v, w2k, w2v, gk):
    b, g, nc, cw = ck.shape
    chunk = pl.BlockSpec((1, 1, nc, cw), lambda i, j: (i, j, 0, 0))
    return pl.pallas_call(
        _cmp_kernel,
        grid=(b, g),
        in_specs=[chunk, chunk, _const_spec(pek.shape), _const_spec(pev.shape),
                  _const_spec(w1k.shape), _const_spec(w1v.shape),
                  _const_spec(w2k.shape), _const_spec(w2v.shape), _const_spec(gk.shape)],
        out_specs=pl.BlockSpec((1, 1, nc, LANES), lambda i, j: (i, j, 0, 0)),
        out_shape=jax.ShapeDtypeStruct((b, g, nc, LANES), BF16),
        compiler_params=_cparams(("parallel", "parallel")),
        name="block_compress",
    )(ck, cv, pek, pev, w1k, w1v, w2k, w2v, gk)


def _online_step(s, v, m_ref, l_ref, acc_ref):
    m_old = m_ref[...]
    m_new = jnp.maximum(m_old, jnp.max(s, axis=-1, keepdims=True))
    p = jnp.exp(s - m_new)
    alpha = jnp.exp(m_old - m_new)
    l_ref[...] = alpha * l_ref[...] + jnp.sum(p, axis=-1, keepdims=True)
    acc_ref[...] = alpha * acc_ref[...] + _dot(p.astype(BF16), v)
    m_ref[...] = m_new


def _online_init(m_ref, l_ref, acc_ref):
    m_ref[...] = jnp.full(m_ref.shape, NEG, F32)
    l_ref[...] = jnp.zeros(l_ref.shape, F32)
    acc_ref[...] = jnp.zeros(acc_ref.shape, F32)


def _da_kernel(lam_init, nd, q_ref, k_ref, v_ref, tab_ref, lamv_ref, gs_ref, o_ref,
               qq_ref, m_ref, l_ref, acc_ref):
    t = DA_T
    qi = pl.program_id(2)
    q = q_ref[0].astype(F32)
    lane = lax.broadcasted_iota(jnp.int32, (t, LANES), 1)
    qq_ref[0:t, :] = jnp.where(lane < DA_HEAD_DIM, q, 0.0).astype(BF16)
    qq_ref[t:2 * t, :] = jnp.where(lane >= DA_HEAD_DIM, q, 0.0).astype(BF16)
    _online_init(m_ref, l_ref, acc_ref)

    def body(kt, carry):
        start = pl.multiple_of(kt * t, t)
        k = k_ref[0, pl.ds(start, t), :]
        v = v_ref[0, pl.ds(start, t), :]
        bias = tab_ref[0, jnp.minimum(qi - kt, nd)]
        s = _dot_nt(qq_ref[...], k).reshape(2, t, t) + bias[None]
        _online_step(s.reshape(2 * t, t), v, m_ref, l_ref, acc_ref)
        return carry

    lax.fori_loop(0, qi + 1, body, 0)

    o = acc_ref[...] / l_ref[...]
    lv = lamv_ref[...]
    lam = (jnp.exp(jnp.sum(lv[0:1] * lv[1:2], axis=-1, keepdims=True))
           - jnp.exp(jnp.sum(lv[2:3] * lv[3:4], axis=-1, keepdims=True)) + lam_init)
    y = o[0:t] - lam * o[t:2 * t]
    o_ref[0] = (_rms(y, gs_ref[...]) * (1.0 - lam_init)).astype(BF16)


def _diff_attention(dq, dk, dv, tab, lamv, gs, lam_init):
    b, s, _ = dq.shape
    t = DA_T
    nd = tab.shape[1] - 1
    return pl.pallas_call(
        functools.partial(_da_kernel, lam_init, nd),
        grid=(b, DA_HEADS, s // t),
        in_specs=[
            pl.BlockSpec((1, t, LANES), lambda i, h, j: (i, j, h)),
            pl.BlockSpec((1, s, LANES), lambda i, h, j: (i, 0, h)),
            pl.BlockSpec((1, s, LANES), lambda i, h, j: (i, 0, h)),
            pl.BlockSpec((1,) + tab.shape[1:], lambda i, h, j: (h, 0, 0, 0)),
            _const_spec(lamv.shape),
            _const_spec(gs.shape),
        ],
        out_specs=pl.BlockSpec((1, t, LANES), lambda i, h, j: (i, j, h)),
        out_shape=jax.ShapeDtypeStruct((b, s, DA_WIDTH), BF16),
        scratch_shapes=[
            pltpu.VMEM((2 * t, LANES), BF16),
            pltpu.VMEM((2 * t, 1), F32),
            pltpu.VMEM((2 * t, 1), F32),
            pltpu.VMEM((2 * t, LANES), F32),
        ],
        compiler_params=_cparams(("parallel", "parallel", "arbitrary")),
        name="diff_attention",
    )(dq, dk, dv, tab, lamv, gs)


def _nsa_kernel(nd, q_ref, ks_ref, vs_ref, kvw_ref, kvc_ref, bc_ref, tab_ref, gate_ref,
                selq_ref, place_ref, ovl_ref, o_ref,
                qz_ref, qs_ref, m_ref, l_ref, acc_ref, comb_ref):
    t = NSA_T
    rows = NSA_P * t
    qi = pl.program_id(2)
    gates = gate_ref[0]
    win_tiles = WINDOW // t
    win_edge = nd + 1

    def gate_rows(r):
        return jnp.concatenate(
            [jnp.broadcast_to(gates[:, 3 * p + r:3 * p + r + 1], (t, LANES)) for p in range(NSA_P)], axis=0)

    qt = q_ref[0]
    for p in range(NSA_P):
        qz_ref[p * t:(p + 1) * t, :] = _dot(qt, selq_ref[p]).astype(BF16)

    kvc = kvc_ref[0, 0]
    nc = kvc.shape[0]
    s = _dot_nt(qz_ref[...], kvc) + bc_ref[...].reshape(rows, nc)
    m = jnp.max(s, axis=-1, keepdims=True)
    e = jnp.exp(s - m)
    l = jnp.sum(e, axis=-1, keepdims=True)
    pc = jnp.where(m > 0.5 * NEG, e / l, 0.0)
    comb_ref[...] = gate_rows(0) * _dot(pc.astype(BF16), kvc)

    psum = pc[0:t] + pc[t:2 * t] + pc[2 * t:3 * t] + pc[3 * t:4 * t]
    imp = _split_dot(psum, ovl_ref[...], 3)
    lane = lax.broadcasted_iota(jnp.int32, (t, LANES), 1)
    j = lane - NSA_SEL_LANE0
    qpos = qi * t + lax.broadcasted_iota(jnp.int32, (t, LANES), 0)
    cur = qpos // SEL_BLOCK
    forced = (j == 0) | (j == cur) | (j == cur - 1)
    score = jnp.where(j > cur, -SEL_FORCE, imp + jnp.where(forced, SEL_FORCE, 0.0))
    lowest = jnp.float32(-3e38)
    score = jnp.where(j >= 0, score, lowest)
    lane_f = lane.astype(F32)
    nsel = jnp.where(j >= 0, 1.0, 0.0)
    for _ in range(SEL_TOPK):
        mx = jnp.max(score, axis=-1, keepdims=True)
        first = jnp.min(jnp.where(score == mx, lane_f, 2.0 * LANES), axis=-1, keepdims=True)
        hit = lane_f == first
        nsel = jnp.where(hit, 0.0, nsel)
        score = jnp.where(hit, lowest, score)
    for p in range(NSA_P):
        qs_ref[p * t:(p + 1) * t, :] = (qz_ref[p * t:(p + 1) * t, :].astype(F32) + nsel).astype(BF16)

    _online_init(m_ref, l_ref, acc_ref)

    def sel_body(kt, carry):
        start = pl.multiple_of(kt * t, t)
        k = ks_ref[0, 0, pl.ds(start, t), :]
        v = vs_ref[0, 0, pl.ds(start, t), :]
        bias = tab_ref[0, jnp.minimum(qi - kt, nd)].reshape(rows, t)
        _online_step(_dot_nt(qs_ref[...], k) + bias, v, m_ref, l_ref, acc_ref)
        return carry

    lax.fori_loop(0, qi + 1, sel_body, 0)
    comb_ref[...] += gate_rows(1) * (acc_ref[...] / l_ref[...])

    _online_init(m_ref, l_ref, acc_ref)

    def win_body(kt, carry):
        start = pl.multiple_of(kt * t, t)
        kv = kvw_ref[0, 0, pl.ds(start, t), :]
        d = qi - kt
        bias = tab_ref[0, jnp.where(d == win_tiles, win_edge, d)].reshape(rows, t)
        _online_step(_dot_nt(qz_ref[...], kv) + bias, kv, m_ref, l_ref, acc_ref)
        return carry

    lax.fori_loop(jnp.maximum(qi - win_tiles, 0), qi + 1, win_body, 0)
    comb = (comb_ref[...] + gate_rows(2) * (acc_ref[...] / l_ref[...])).astype(BF16)

    out = _dot(comb[0:t], place_ref[0])
    for p in range(1, NSA_P):
        out = out + _dot(comb[p * t:(p + 1) * t], place_ref[p])
    o_ref[0] = out.astype(BF16)


def _nsa_attention(nq, ks, vs, kvw, kvc, bias_c, tab, gates, selq, place, ovl):
    b, s, _ = nq.shape
    t = NSA_T
    nc = kvc.shape[2]
    nd = tab.shape[1] - 2
    gw = NSA_P * NSA_D
    seq = pl.BlockSpec((1, 1, s, LANES), lambda i, g, j: (i, g, 0, 0))
    return pl.pallas_call(
        functools.partial(_nsa_kernel, nd),
        grid=(b, NSA_G, s // t),
        in_specs=[
            pl.BlockSpec((1, t, gw), lambda i, g, j: (i, j, g)),
            seq, seq, seq,
            pl.BlockSpec((1, 1, nc, LANES), lambda i, g, j: (i, g, 0, 0)),
            pl.BlockSpec((NSA_P, t, nc), lambda i, g, j: (g, j, 0)),
            pl.BlockSpec((1,) + tab.shape[1:], lambda i, g, j: (g, 0, 0, 0, 0)),
            pl.BlockSpec((1, t, LANES), lambda i, g, j: (i, j, g)),
            _const_spec(selq.shape), _const_spec(place.shape), _const_spec(ovl.shape),
        ],
        out_specs=pl.BlockSpec((1, t, gw), lambda i, g, j: (i, j, g)),
        out_shape=jax.ShapeDtypeStruct((b, s, NSA_WIDTH), BF16),
        scratch_shapes=[
            pltpu.VMEM((NSA_P * t, LANES), BF16),
            pltpu.VMEM((NSA_P * t, LANES), BF16),
            pltpu.VMEM((NSA_P * t, 1), F32),
            pltpu.VMEM((NSA_P * t, 1), F32),
            pltpu.VMEM((NSA_P * t, LANES), F32),
            pltpu.VMEM((NSA_P * t, LANES), F32),
        ],
        compiler_params=_cparams(("parallel", "parallel", "arbitrary")),
        name="nsa_attention",
    )(nq, ks, vs, kvw, kvc, bias_c, tab, gates, selq, place, ovl)


def _merge_kernel(x_ref, ya_ref, yb_ref, g_ref, wmg_ref, pa_ref, pb_ref, wo_ref, o_ref):
    x = x_ref[...]
    h = _rms(x, g_ref[...]).astype(BF16)
    mg = jax.nn.sigmoid(_dot(h, wmg_ref[...]))
    merged = mg[:, 0:D_MODEL] * _dot(ya_ref[...], pa_ref[...]) + mg[:, D_MODEL:] * _dot(yb_ref[...], pb_ref[...])
    o_ref[...] = x + _dot(merged.astype(BF16), wo_ref[...])


def _merge(x2, ya, yb, g, wmg, pa, pb, wo):
    n, d = x2.shape
    tm = ROW_TILE
    row = lambda wd: pl.BlockSpec((tm, wd), lambda i: (i, 0))
    return pl.pallas_call(
        _merge_kernel,
        grid=(n // tm,),
        in_specs=[row(d), row(DA_WIDTH), row(NSA_WIDTH), _const_spec((1, d)),
                  _const_spec(wmg.shape), _const_spec(pa.shape), _const_spec(pb.shape), _const_spec(wo.shape)],
        out_specs=row(d),
        out_shape=jax.ShapeDtypeStruct((n, d), F32),
        compiler_params=_cparams(("parallel",)),
        name="gated_merge_out_proj",
    )(x2, ya, yb, g, wmg, pa, pb, wo)


def _rel_bucket(dist):
    n = jnp.maximum(dist, 0)
    max_exact = N_BUCKETS // 2
    nf = jnp.maximum(n, 1).astype(jnp.float32)
    large = max_exact + (jnp.log(nf / max_exact) / math.log(MAX_DISTANCE / max_exact)
                         * (N_BUCKETS - max_exact)).astype(jnp.int32)
    large = jnp.minimum(large, N_BUCKETS - 1)
    return jnp.where(n < max_exact, n, large)


def _far_tiles(t):
    return -(-(MAX_DISTANCE + t - 1) // t)


def _tile_bias(table, t, extra_window_edge):
    nd = _far_tiles(t)
    i = jnp.arange(t)[:, None]
    j = jnp.arange(t)[None, :]
    d = jnp.arange(nd + 1)[:, None, None]
    dist = d * t + i - j
    bias = jnp.take(table, _rel_bucket(dist), axis=0)
    bias = jnp.where((dist >= 0)[..., None], bias, NEG)
    if extra_window_edge:
        we = WINDOW // t
        edge = jnp.where((dist[we] < WINDOW)[..., None], bias[we], NEG)
        bias = jnp.concatenate([bias, edge[None]], axis=0)
    return jnp.transpose(bias, (3, 0, 1, 2)).astype(F32)


def _cmp_bias(table, s):
    nc = s // CMP_STRIDE
    c = jnp.arange(nc)[None, :]
    dist = jnp.arange(s)[:, None] - (c * CMP_STRIDE + CMP_BLOCK - 1)
    bias = jnp.take(table, _rel_bucket(dist), axis=0)
    valid = (dist >= 0) & (c < nc - CMP_BLOCK // CMP_STRIDE + 1)
    return jnp.transpose(jnp.where(valid[..., None], bias, NEG), (2, 0, 1)).astype(F32)


def _static_tables(s):
    nc = s // CMP_STRIDE
    n_cmp = nc - CMP_BLOCK // CMP_STRIDE + 1
    selq = np.zeros((NSA_P, NSA_P * NSA_D, LANES), np.float32)
    place = np.zeros((NSA_P, LANES, NSA_P * NSA_D), np.float32)
    for p in range(NSA_P):
        for dd in range(NSA_D):
            selq[p, p * NSA_D + dd, dd] = 1.0
            place[p, NSA_D + dd, p * NSA_D + dd] = 1.0
    ovl = np.zeros((nc, LANES), np.float32)
    for c in range(n_cmp):
        for jb in range(s // SEL_BLOCK):
            if c * CMP_STRIDE < (jb + 1) * SEL_BLOCK and c * CMP_STRIDE + CMP_BLOCK - 1 >= jb * SEL_BLOCK:
                ovl[c, NSA_SEL_LANE0 + jb] = 1.0
    blk = np.zeros((s, NSA_D), np.float32)
    blk[np.arange(s), np.arange(s) // SEL_BLOCK] = NEG
    r = np.kron(np.eye(512 // 64, dtype=np.float32), np.ones((64, 64), np.float32))
    return (jnp.asarray(selq, BF16), jnp.asarray(place, BF16), jnp.asarray(ovl, BF16),
            jnp.asarray(blk, BF16), jnp.asarray(r, BF16))


def kernel(x, w_in, w_branch_a, w_branch_b, w_out, norm_ffn1, norm_mix, norm_ffn2, ffn1_w1, ffn1_w3, ffn1_w2, ffn2_w1, ffn2_w3, ffn2_w2, da_q_gain, da_k_gain, da_lambda_q1, da_lambda_k1, da_lambda_q2, da_lambda_k2, da_subln_gain, nsa_q_gain, nsa_k_gain, cmp_pe_k, cmp_w1_k, cmp_w2_k, cmp_pe_v, cmp_w1_v, cmp_w2_v, rel_bias_table):
    b, s, d = x.shape
    depth = w_in.shape[0]
    assert d == D_MODEL and s % DA_T == 0 and (b * s) % ROW_TILE == 0
    assert SEL_TOPK <= s // SEL_BLOCK <= LANES - NSA_SEL_LANE0 and s >= WINDOW
    n = b * s
    nc = s // CMP_STRIDE
    half = CMP_STRIDE * NSA_D

    selq, place, ovl, blk, r512 = _static_tables(s)
    tab_a = _tile_bias(rel_bias_table[:, :DA_HEADS], DA_T, False)
    tab_b = _tile_bias(rel_bias_table[:, DA_HEADS:], NSA_T, True)
    tab_b = tab_b.reshape(NSA_G, NSA_P, -1, NSA_T, NSA_T).transpose(0, 2, 1, 3, 4)
    bias_c = _cmp_bias(rel_bias_table[:, DA_HEADS:], s)
    blk_b = jnp.broadcast_to(blk[None, None], (b, NSA_G, s, NSA_D))

    x2 = x.reshape(n, d)
    row = lambda v: v.reshape(1, -1).astype(F32)
    tile = lambda v, k: jnp.tile(v.astype(F32), k).reshape(1, -1)
    for l in range(depth):
        x2 = _ffn(x2, row(norm_ffn1[l]), ffn1_w1[l].astype(BF16), ffn1_w3[l].astype(BF16), ffn1_w2[l].astype(BF16))

        wl = w_in[l]
        w_ng = wl[:, 2816:2840].reshape(d, NSA_G, NSA_P * 3)
        w_ng = jnp.pad(w_ng, ((0, 0), (0, 0), (0, LANES - NSA_P * 3))).reshape(d, NSA_G * LANES)
        w_proj = jnp.concatenate([wl[:, :2816], w_ng], axis=1).astype(BF16)
        dq, dk, dv, nq, kcvc, nkv, ng = _proj(
            x2, row(norm_mix[l]), w_proj, r512,
            tile(da_q_gain[l], 8), tile(da_k_gain[l], 8), tile(nsa_q_gain[l], 8),
            tile(nsa_k_gain[l, 1], 2), tile(nsa_k_gain[l, 2], 2))

        lam_init = 0.8 - 0.6 * math.exp(-0.3 * l)
        lamv = jnp.stack([da_lambda_q1[l], da_lambda_k1[l], da_lambda_q2[l], da_lambda_k2[l]]).astype(F32)
        ya = _diff_attention(dq.reshape(b, s, 512), dk.reshape(b, s, 512), dv.reshape(b, s, 512),
                             tab_a, lamv, row(da_subln_gain[l]), lam_init)

        chunks = kcvc.reshape(b, nc, CMP_STRIDE, 2, NSA_G, NSA_D).transpose(3, 0, 4, 1, 2, 5).reshape(2, b, NSA_G, nc, half)
        w2k = jnp.pad(cmp_w2_k[l], ((0, 0), (0, NSA_D))).astype(BF16)
        w2v = jnp.pad(cmp_w2_v[l], ((0, 0), (NSA_D, 0))).astype(BF16)
        kvc = _compress(chunks[0], chunks[1],
                        cmp_pe_k[l].reshape(2, half).astype(F32), cmp_pe_v[l].reshape(2, half).astype(F32),
                        cmp_w1_k[l].reshape(2, half, CMP_HIDDEN).astype(BF16),
                        cmp_w1_v[l].reshape(2, half, CMP_HIDDEN).astype(BF16),
                        w2k, w2v, jnp.pad(row(nsa_k_gain[l, 0]), ((0, 0), (0, NSA_D))))
        kv4 = nkv.reshape(b, s, 4, NSA_G, NSA_D).transpose(2, 0, 3, 1, 4)
        ks_aug = jnp.concatenate([kv4[0], blk_b], axis=-1)
        vs_aug = jnp.concatenate([jnp.zeros_like(kv4[1]), kv4[1]], axis=-1)
        kvw = jnp.concatenate([kv4[2], kv4[3]], axis=-1)
        yb = _nsa_attention(nq.reshape(b, s, 512), ks_aug, vs_aug, kvw, kvc, bias_c, tab_b,
                            ng.reshape(b, s, NSA_G * LANES), selq, place, ovl)

        x2 = _merge(x2, ya.reshape(n, DA_WIDTH), yb.reshape(n, NSA_WIDTH), row(norm_mix[l]),
                    wl[:, 2840:].astype(BF16), w_branch_a[l].astype(BF16), w_branch_b[l].astype(BF16),
                    w_out[l].astype(BF16))
        x2 = _ffn(x2, row(norm_ffn2[l]), ffn2_w1[l].astype(BF16), ffn2_w3[l].astype(BF16), ffn2_w2[l].astype(BF16))
    return x2.reshape(b, s, d)
```

```python
import functools
import math

import numpy as np
import jax
import jax.numpy as jnp
from jax import lax
from jax.experimental import pallas as pl
from jax.experimental.pallas import tpu as pltpu

F32 = jnp.float32
BF16 = jnp.bfloat16

D_MODEL = 1024
DA_HEADS = 4
DA_HEAD_DIM = 64
DA_V_DIM = 2 * DA_HEAD_DIM
DA_WIDTH = DA_HEADS * DA_V_DIM
NSA_HEADS = 8
NSA_G = 2
NSA_P = NSA_HEADS // NSA_G
NSA_D = 64
NSA_WIDTH = NSA_HEADS * NSA_D
NSA_KV = NSA_G * NSA_D
CMP_BLOCK = 32
CMP_STRIDE = 16
CMP_HIDDEN = 128
SEL_BLOCK = 64
SEL_TOPK = 8
SEL_FORCE = 1e4
WINDOW = 512
N_BUCKETS = 32
MAX_DISTANCE = 1024
D_FF = 2816
EPS = 1e-6
NEG = -1e30

LANES = 128
VMEM_LIMIT = 52 * 1024 * 1024

TILE = 256
CW = LANES
GATE_ROWS = 16

ROW_COLS = 3 * 512 + 2 * NSA_KV + 4 * LANES
COL_ROWS = DA_WIDTH + 2 * NSA_KV + NSA_G * GATE_ROWS


def _cparams(sem):
    return pltpu.CompilerParams(dimension_semantics=sem, vmem_limit_bytes=VMEM_LIMIT)


def _const_spec(shape):
    nd = len(shape)
    return pl.BlockSpec(shape, lambda *_: (0,) * nd)


def _rms(xf, g):
    ms = jnp.mean(xf * xf, axis=-1, keepdims=True)
    return xf * lax.rsqrt(ms + EPS) * g


def _dot(a, b):
    return jnp.dot(a, b, preferred_element_type=F32)


def _dot_nt(a, b):
    return lax.dot_general(a, b, (((1,), (1,)), ((), ())), preferred_element_type=F32)


def _bf16_pieces(x, parts):
    out = []
    r = x
    for i in range(parts):
        piece = r.astype(BF16)
        out.append(piece)
        if i + 1 < parts:
            r = r - piece.astype(F32)
    return out


def _ffn_kernel(x_ref, g_ref, w1_ref, w3_ref, w2_ref, o_ref):
    x = x_ref[...]
    h = _rms(x, g_ref[...]).astype(BF16)
    a = _dot(h, w1_ref[...])
    b = _dot(h, w3_ref[...])
    t = (jax.nn.silu(a) * b).astype(BF16)
    o_ref[...] = x + 0.5 * _dot(t, w2_ref[...])


def _ffn(x2, g, w1, w3, w2):
    n, d = x2.shape
    tm = TILE
    return pl.pallas_call(
        _ffn_kernel,
        grid=(n // tm,),
        in_specs=[
            pl.BlockSpec((tm, d), lambda i: (i, 0)),
            _const_spec((1, d)),
            _const_spec(w1.shape),
            _const_spec(w3.shape),
            _const_spec(w2.shape),
        ],
        out_specs=pl.BlockSpec((tm, d), lambda i: (i, 0)),
        out_shape=jax.ShapeDtypeStruct((n, d), F32),
        compiler_params=_cparams(("parallel",)),
        name="ffn_half_step",
    )(x2, g, w1, w3, w2)


def _group_rms(z, r_ref, gain):
    w = z.shape[-1]
    r = r_ref[0:w, 0:w]
    hi, lo = _bf16_pieces(z * z, 2)
    ss = _dot(hi, r) + _dot(lo, r)
    return z * lax.rsqrt(ss * (1.0 / 64.0) + EPS) * gain


def _proj_kernel(x_ref, g_ref, w_ref, wt_ref, r_ref, blk_ref, gq_ref, gk_ref, gnq_ref, gks_ref, gkw_ref,
                 dq_ref, dk_ref, nq_ref, kcvc_ref, kaug_ref, vt_ref, gate_ref):
    h = _rms(x_ref[...], g_ref[...]).astype(BF16)
    z = _dot(h, w_ref[...])
    zt = _dot_nt(wt_ref[...], h)
    scale = DA_HEAD_DIM ** -0.5
    dq_ref[...] = (_group_rms(z[:, 0:512], r_ref, gq_ref[...]) * scale).astype(BF16)
    dk_ref[...] = _group_rms(z[:, 512:1024], r_ref, gk_ref[...]).astype(BF16)
    nq_ref[...] = (_group_rms(z[:, 1024:1536], r_ref, gnq_ref[...]) * scale).astype(BF16)
    kcvc_ref[...] = z[:, 1536:1792]
    ks = _group_rms(z[:, 1792:2048], r_ref, gks_ref[...])
    kw = _group_rms(z[:, 2048:2304], r_ref, gkw_ref[...])
    blk = blk_ref[...].astype(F32)
    for g in range(NSA_G):
        kaug_ref[0, 0, g] = (ks[:, g * LANES:(g + 1) * LANES] + blk).astype(BF16)
        kaug_ref[0, 1, g] = kw[:, g * LANES:(g + 1) * LANES].astype(BF16)
    nv = DA_WIDTH + 2 * NSA_KV
    vt_ref[0] = zt[0:nv].astype(BF16)
    gate_ref[0] = jax.nn.sigmoid(zt[nv:COL_ROWS])


def _proj(x2, g, w, wt, r, blk, gq, gk, gnq, gks, gkw, b, s):
    n, d = x2.shape
    tm = TILE
    nt = s // tm
    nv = DA_WIDTH + 2 * NSA_KV
    row = lambda wd: pl.BlockSpec((tm, wd), lambda i: (i, 0))
    return pl.pallas_call(
        _proj_kernel,
        grid=(n // tm,),
        in_specs=[row(d), _const_spec((1, d)), _const_spec(w.shape), _const_spec(wt.shape),
                  _const_spec(r.shape), pl.BlockSpec((tm, LANES), lambda i: (i % nt, 0)),
                  _const_spec((1, 512)), _const_spec((1, 512)), _const_spec((1, 512)),
                  _const_spec((1, 256)), _const_spec((1, 256))],
        out_specs=[row(512), row(512), row(512), row(256),
                   pl.BlockSpec((1, 2, NSA_G, tm, LANES), lambda i: (i // nt, 0, 0, i % nt, 0)),
                   pl.BlockSpec((1, nv, tm), lambda i: (i, 0, 0)),
                   pl.BlockSpec((1, NSA_G * GATE_ROWS, tm), lambda i: (i, 0, 0))],
        out_shape=[
            jax.ShapeDtypeStruct((n, 512), BF16),
            jax.ShapeDtypeStruct((n, 512), BF16),
            jax.ShapeDtypeStruct((n, 512), BF16),
            jax.ShapeDtypeStruct((n, 256), F32),
            jax.ShapeDtypeStruct((b, 2, NSA_G, s, LANES), BF16),
            jax.ShapeDtypeStruct((n // tm, nv, tm), BF16),
            jax.ShapeDtypeStruct((n // tm, NSA_G * GATE_ROWS, tm), F32),
        ],
        compiler_params=_cparams(("parallel",)),
        name="norm_in_proj",
    )(x2, g, w, wt, r, blk, gq, gk, gnq, gks, gkw)


def _cmp_kernel(ck_ref, cv_ref, pek_ref, pev_ref, w1k_ref, w1v_ref, w2k_ref, w2vt_ref, gk_ref, kc_ref, vct_ref):
    def hidden(c, pe_ref, w1_ref):
        lo = _dot((c + pe_ref[0:1, :]).astype(BF16), w1_ref[0])
        hi = _dot((c + pe_ref[1:2, :]).astype(BF16), w1_ref[1])
        nrow = hi.shape[0]
        return jax.nn.gelu(lo + pltpu.roll(hi, nrow - 1, axis=0)).astype(BF16)
    kc = _dot(hidden(ck_ref[0, 0], pek_ref, w1k_ref), w2k_ref[...])
    ms = jnp.sum(kc * kc, axis=-1, keepdims=True) * (1.0 / NSA_D)
    kc_ref[0, 0] = (kc * lax.rsqrt(ms + EPS) * gk_ref[...]).astype(BF16)
    vct_ref[0, 0] = _dot_nt(w2vt_ref[...], hidden(cv_ref[0, 0], pev_ref, w1v_ref)).astype(BF16)


def _compress(ck, cv, pek, pev, w1k, w1v, w2k, w2vt, gk):
    b, g, nc, cw = ck.shape
    chunk = pl.BlockSpec((1, 1, nc, cw), lambda i, j: (i, j, 0, 0))
    return pl.pallas_call(
        _cmp_kernel,
        grid=(b, g),
        in_specs=[chunk, chunk, _const_spec(pek.shape), _const_spec(pev.shape),
                  _const_spec(w1k.shape), _const_spec(w1v.shape),
                  _const_spec(w2k.shape), _const_spec(w2vt.shape), _const_spec(gk.shape)],
        out_specs=[pl.BlockSpec((1, 1, nc, LANES), lambda i, j: (i, j, 0, 0)),
                   pl.BlockSpec((1, 1, NSA_D, nc), lambda i, j: (i, j, 0, 0))],
        out_shape=[jax.ShapeDtypeStruct((b, g, nc, LANES), BF16),
                   jax.ShapeDtypeStruct((b, g, NSA_D, nc), BF16)],
        compiler_params=_cparams(("parallel", "parallel")),
        name="block_compress",
    )(ck, cv, pek, pev, w1k, w1v, w2k, w2vt, gk)


def _flash_tiles(lo, hi, last, qk_fn, vt_fn, bias_fn, s_ref, p_ref, alpha_ref, m_ref, l_ref, acc_ref):
    width = s_ref.shape[1]
    m_ref[...] = jnp.full(m_ref.shape, NEG, F32)
    l_ref[...] = jnp.zeros(l_ref.shape, F32)
    acc_ref[...] = jnp.zeros(acc_ref.shape, F32)
    p_ref[...] = jnp.zeros(p_ref.shape, BF16)
    alpha_ref[...] = jnp.ones(alpha_ref.shape, F32)

    def pv(kt):
        acc_ref[...] = alpha_ref[...] * acc_ref[...] + _dot(vt_fn(kt), p_ref[...])

    def softmax(kt):
        for c in range(width // CW):
            cs = slice(c * CW, (c + 1) * CW)
            s = s_ref[:, cs] + bias_fn(kt, cs)
            m_old = m_ref[:, cs]
            m_new = jnp.maximum(m_old, jnp.max(s, axis=0, keepdims=True))
            p = jnp.exp(s - m_new)
            alpha = jnp.exp(m_old - m_new)
            l_ref[:, cs] = alpha * l_ref[:, cs] + jnp.sum(p, axis=0, keepdims=True)
            m_ref[:, cs] = m_new
            alpha_ref[:, cs] = alpha
            p_ref[:, cs] = p.astype(BF16)

    s_ref[...] = qk_fn(lo)

    def body(kt, carry):
        pv(jnp.maximum(kt - 1, lo))
        softmax(kt)
        s_ref[...] = qk_fn(jnp.minimum(kt + 1, last))
        return carry

    lax.fori_loop(lo, hi, body, 0)
    pv(hi - 1)


def _to_rows(xt):
    t = xt.shape[1]
    return jnp.concatenate([xt[:, c * LANES:(c + 1) * LANES].T for c in range(t // LANES)], axis=0)


def _da_kernel(lam_init, nd, q_ref, k_ref, vt_ref, tab_ref, lamv_ref, gs_ref, o_ref,
               qq_ref, s_ref, p_ref, alpha_ref, m_ref, l_ref, acc_ref):
    t = TILE
    qi = pl.program_id(2)
    q = q_ref[0].astype(F32)
    lane = lax.broadcasted_iota(jnp.int32, (t, LANES), 1)
    qq_ref[0:t, :] = jnp.where(lane < DA_HEAD_DIM, q, 0.0).astype(BF16)
    qq_ref[t:2 * t, :] = jnp.where(lane >= DA_HEAD_DIM, q, 0.0).astype(BF16)

    def qk(kt):
        return _dot_nt(k_ref[0, pl.ds(pl.multiple_of(kt * t, t), t), :], qq_ref[...])

    def bias(kt, cs):
        return tab_ref[0, jnp.minimum(qi - kt, nd), :, slice(cs.start % t, cs.start % t + CW)]

    _flash_tiles(0, qi + 1, qi, qk, lambda kt: vt_ref[0, kt], bias,
                 s_ref, p_ref, alpha_ref, m_ref, l_ref, acc_ref)

    ot = acc_ref[...] / l_ref[...]
    lv = lamv_ref[...]
    lam = (jnp.exp(jnp.sum(lv[0:1] * lv[1:2], axis=-1, keepdims=True))
           - jnp.exp(jnp.sum(lv[2:3] * lv[3:4], axis=-1, keepdims=True)) + lam_init)
    y = _to_rows(ot[:, 0:t] - lam * ot[:, t:2 * t])
    o_ref[0] = (_rms(y, gs_ref[...]) * (1.0 - lam_init)).astype(BF16)


def _diff_attention(dq, dk, vt, tab, lamv, gs, lam_init):
    b, s, _ = dq.shape
    t = TILE
    nkt = s // t
    nd = tab.shape[1] - 1
    return pl.pallas_call(
        functools.partial(_da_kernel, lam_init, nd),
        grid=(b, DA_HEADS, nkt),
        in_specs=[
            pl.BlockSpec((1, t, LANES), lambda i, h, j: (i, j, h)),
            pl.BlockSpec((1, s, LANES), lambda i, h, j: (i, 0, h)),
            pl.BlockSpec((1, nkt, DA_V_DIM, t), lambda i, h, j: (i, 0, h, 0)),
            pl.BlockSpec((1,) + tab.shape[1:], lambda i, h, j: (h, 0, 0, 0)),
            _const_spec(lamv.shape),
            _const_spec(gs.shape),
        ],
        out_specs=pl.BlockSpec((1, t, LANES), lambda i, h, j: (i, j, h)),
        out_shape=jax.ShapeDtypeStruct((b, s, DA_WIDTH), BF16),
        scratch_shapes=[
            pltpu.VMEM((2 * t, LANES), BF16),
            pltpu.VMEM((t, 2 * t), F32),
            pltpu.VMEM((t, 2 * t), BF16),
            pltpu.VMEM((1, 2 * t), F32),
            pltpu.VMEM((1, 2 * t), F32),
            pltpu.VMEM((1, 2 * t), F32),
            pltpu.VMEM((DA_V_DIM, 2 * t), F32),
        ],
        compiler_params=_cparams(("parallel", "parallel", "arbitrary")),
        name="diff_attention",
    )(dq, dk, vt, tab, lamv, gs)


def _nsa_kernel(nd, nqt, q_ref, kaug_ref, vst_ref, vwt_ref, kc_ref, vct_ref, longc_ref, tab_ref, gate_ref,
                selq_ref, place_ref, ovlt_ref, o_ref,
                qz_ref, qs_ref, s_ref, p_ref, alpha_ref, m_ref, l_ref, acc_ref, comb_ref, psum_ref):
    t = TILE
    halves = t // CW
    nchunk = NSA_P * halves
    qi = pl.program_id(2)
    win_tiles = WINDOW // t
    win_edge = nd + 1

    def gate_row(r):
        return jnp.concatenate([gate_ref[0, 3 * p + r:3 * p + r + 1, :] for p in range(NSA_P)], axis=1)

    qt = q_ref[0]
    for p in range(NSA_P):
        qz_ref[p * t:(p + 1) * t, :] = _dot(qt, selq_ref[p]).astype(BF16)

    kc = kc_ref[0, 0]
    vct = vct_ref[0, 0]
    nc = kc.shape[0]
    cstart = pl.multiple_of((nqt - 1 - qi) * (t // CMP_STRIDE), t // CMP_STRIDE)
    for c in range(nchunk):
        p, half = divmod(c, halves)
        cs = slice(c * CW, (c + 1) * CW)
        hs = slice(half * CW, (half + 1) * CW)
        s = _dot_nt(kc, qz_ref[cs, :]) + longc_ref[0, pl.ds(cstart, nc), cs]
        m = jnp.max(s, axis=0, keepdims=True)
        e = jnp.exp(s - m)
        pc = jnp.where(m > 0.5 * NEG, e / jnp.sum(e, axis=0, keepdims=True), 0.0)
        comb_ref[:, cs] = gate_ref[0, 3 * p:3 * p + 1, hs] * _dot(vct, pc.astype(BF16))
        if p == 0:
            psum_ref[:, hs] = pc
        else:
            psum_ref[:, hs] += pc

    ovlt = ovlt_ref[...]
    imp = None
    for piece in _bf16_pieces(psum_ref[...], 3):
        d = _dot(ovlt, piece)
        imp = d if imp is None else imp + d
    nblk = imp.shape[0]
    jrow = lax.broadcasted_iota(jnp.int32, (nblk, t), 0)
    cur = (qi * t + lax.broadcasted_iota(jnp.int32, (nblk, t), 1)) // SEL_BLOCK
    forced = (jrow == 0) | (jrow == cur) | (jrow == cur - 1)
    score = jnp.where(jrow > cur, -SEL_FORCE, imp + jnp.where(forced, SEL_FORCE, 0.0))
    lowest = jnp.float32(-3e38)
    jrow_f = jrow.astype(F32)
    nsel = jnp.ones((nblk, t), F32)
    for _ in range(SEL_TOPK):
        mx = jnp.max(score, axis=0, keepdims=True)
        first = jnp.min(jnp.where(score == mx, jrow_f, 2.0 * LANES), axis=0, keepdims=True)
        hit = jrow_f == first
        nsel = jnp.where(hit, 0.0, nsel)
        score = jnp.where(hit, lowest, score)
    parts = [jnp.zeros((NSA_D, t), F32), nsel]
    if LANES - NSA_D - nblk:
        parts.append(jnp.zeros((LANES - NSA_D - nblk, t), F32))
    nsel_rows = _to_rows(jnp.concatenate(parts, axis=0))
    for p in range(NSA_P):
        rs = slice(p * t, (p + 1) * t)
        qs_ref[rs, :] = (qz_ref[rs, :].astype(F32) + nsel_rows).astype(BF16)

    def key_tile(branch, kt):
        return kaug_ref[0, branch, 0, pl.ds(pl.multiple_of(kt * t, t), t), :]

    flash_refs = (s_ref, p_ref, alpha_ref, m_ref, l_ref, acc_ref)
    _flash_tiles(0, qi + 1, qi,
                 lambda kt: _dot_nt(key_tile(0, kt), qs_ref[...]),
                 lambda kt: vst_ref[0, kt],
                 lambda kt, cs: tab_ref[0, jnp.minimum(qi - kt, nd), :, cs],
                 *flash_refs)
    comb_ref[...] += gate_row(1) * (acc_ref[...] / l_ref[...])

    def win_bias(kt, cs):
        d = qi - kt
        return tab_ref[0, jnp.where(d == win_tiles, win_edge, d), :, cs]

    _flash_tiles(jnp.maximum(qi - win_tiles, 0), qi + 1, qi,
                 lambda kt: _dot_nt(key_tile(1, kt), qz_ref[...]),
                 lambda kt: vwt_ref[0, kt],
                 win_bias, *flash_refs)
    comb = comb_ref[...] + gate_row(2) * (acc_ref[...] / l_ref[...])

    pad = jnp.zeros((LANES - NSA_D, t), F32)
    out = None
    for p in range(NSA_P):
        rows = _to_rows(jnp.concatenate([comb[:, p * t:(p + 1) * t], pad], axis=0)).astype(BF16)
        d = _dot(rows, place_ref[p])
        out = d if out is None else out + d
    o_ref[0] = out.astype(BF16)


def _nsa_attention(nq, kaug, vt, kc, vct, longc, tab, gates, selq, place, ovlt):
    b, s, _ = nq.shape
    t = TILE
    nqt = s // t
    nc = kc.shape[2]
    nd = tab.shape[1] - 2
    gw = NSA_P * NSA_D
    vrow0 = DA_WIDTH // NSA_D
    return pl.pallas_call(
        functools.partial(_nsa_kernel, nd, nqt),
        grid=(b, NSA_G, nqt),
        in_specs=[
            pl.BlockSpec((1, t, gw), lambda i, g, j: (i, j, g)),
            pl.BlockSpec((1, 2, 1, s, LANES), lambda i, g, j: (i, 0, g, 0, 0)),
            pl.BlockSpec((1, nqt, NSA_D, t), lambda i, g, j: (i, 0, vrow0 + g, 0)),
            pl.BlockSpec((1, nqt, NSA_D, t), lambda i, g, j: (i, 0, vrow0 + NSA_G + g, 0)),
            pl.BlockSpec((1, 1, nc, LANES), lambda i, g, j: (i, g, 0, 0)),
            pl.BlockSpec((1, 1, NSA_D, nc), lambda i, g, j: (i, g, 0, 0)),
            pl.BlockSpec((1,) + longc.shape[1:], lambda i, g, j: (g, 0, 0)),
            pl.BlockSpec((1,) + tab.shape[1:], lambda i, g, j: (g, 0, 0, 0)),
            pl.BlockSpec((1, GATE_ROWS, t), lambda i, g, j: (i * nqt + j, g, 0)),
            _const_spec(selq.shape), _const_spec(place.shape), _const_spec(ovlt.shape),
        ],
        out_specs=pl.BlockSpec((1, t, gw), lambda i, g, j: (i, j, g)),
        out_shape=jax.ShapeDtypeStruct((b, s, NSA_WIDTH), BF16),
        scratch_shapes=[
            pltpu.VMEM((NSA_P * t, LANES), BF16),
            pltpu.VMEM((NSA_P * t, LANES), BF16),
            pltpu.VMEM((t, NSA_P * t), F32),
            pltpu.VMEM((t, NSA_P * t), BF16),
            pltpu.VMEM((1, NSA_P * t), F32),
            pltpu.VMEM((1, NSA_P * t), F32),
            pltpu.VMEM((1, NSA_P * t), F32),
            pltpu.VMEM((NSA_D, NSA_P * t), F32),
            pltpu.VMEM((NSA_D, NSA_P * t), F32),
            pltpu.VMEM((nc, t), F32),
        ],
        compiler_params=_cparams(("parallel", "parallel", "arbitrary")),
        name="nsa_attention",
    )(nq, kaug, vt, vt, kc, vct, longc, tab, gates, selq, place, ovlt)


def _merge_kernel(x_ref, ya_ref, yb_ref, g_ref, wmg_ref, pa_ref, pb_ref, wo_ref, o_ref):
    x = x_ref[...]
    h = _rms(x, g_ref[...]).astype(BF16)
    mg = jax.nn.sigmoid(_dot(h, wmg_ref[...]))
    merged = mg[:, 0:D_MODEL] * _dot(ya_ref[...], pa_ref[...]) + mg[:, D_MODEL:] * _dot(yb_ref[...], pb_ref[...])
    o_ref[...] = x + _dot(merged.astype(BF16), wo_ref[...])


def _merge(x2, ya, yb, g, wmg, pa, pb, wo):
    n, d = x2.shape
    tm = TILE
    row = lambda wd: pl.BlockSpec((tm, wd), lambda i: (i, 0))
    return pl.pallas_call(
        _merge_kernel,
        grid=(n // tm,),
        in_specs=[row(d), row(DA_WIDTH), row(NSA_WIDTH), _const_spec((1, d)),
                  _const_spec(wmg.shape), _const_spec(pa.shape), _const_spec(pb.shape), _const_spec(wo.shape)],
        out_specs=row(d),
        out_shape=jax.ShapeDtypeStruct((n, d), F32),
        compiler_params=_cparams(("parallel",)),
        name="gated_merge_out_proj",
    )(x2, ya, yb, g, wmg, pa, pb, wo)


def _rel_bucket(dist):
    n = jnp.maximum(dist, 0)
    max_exact = N_BUCKETS // 2
    nf = jnp.maximum(n, 1).astype(jnp.float32)
    large = max_exact + (jnp.log(nf / max_exact) / math.log(MAX_DISTANCE / max_exact)
                         * (N_BUCKETS - max_exact)).astype(jnp.int32)
    large = jnp.minimum(large, N_BUCKETS - 1)
    return jnp.where(n < max_exact, n, large)


def _bias_of(table, dist):
    bucket = _rel_bucket(dist)
    col = lambda v: v.reshape((-1,) + (1,) * dist.ndim)
    out = jnp.zeros((table.shape[1],) + dist.shape, F32)
    for bkt in range(N_BUCKETS):
        out = jnp.where(bucket == bkt, col(table[bkt].astype(F32)), out)
    return jnp.where(dist >= 0, out, NEG)


def _far_tiles(t):
    return -(-(MAX_DISTANCE + t - 1) // t)


def _tile_bias(table, t, window_edge):
    nd = _far_tiles(t)
    j = jnp.arange(t)[:, None]
    i = jnp.arange(t)[None, :]
    dist = jnp.arange(nd + 1)[:, None, None] * t + i - j
    bias = _bias_of(table, dist)
    if window_edge:
        we = WINDOW // t
        edge = jnp.where(dist[we] < WINDOW, bias[:, we], NEG)
        bias = jnp.concatenate([bias, edge[:, None]], axis=1)
    return bias


def _cmp_bias(table, t, nqt):
    cpt = t // CMP_STRIDE
    r = jnp.arange((2 * nqt - 1) * cpt)[:, None]
    dist = (nqt - 1 - r // cpt) * t + jnp.arange(t)[None, :] - ((r % cpt) * CMP_STRIDE + CMP_BLOCK - 1)
    return _bias_of(table, dist)


def _lanes_by_head(x):
    x = x.reshape((NSA_G, NSA_P) + x.shape[1:])
    x = jnp.moveaxis(x, 1, -2)
    return x.reshape(x.shape[:-2] + (NSA_P * x.shape[-1],))


def _static_tables(s):
    nc = s // CMP_STRIDE
    n_cmp = nc - CMP_BLOCK // CMP_STRIDE + 1
    nblk = s // SEL_BLOCK
    selq = np.zeros((NSA_P, NSA_P * NSA_D, LANES), np.float32)
    place = np.zeros((NSA_P, LANES, NSA_P * NSA_D), np.float32)
    for p in range(NSA_P):
        for dd in range(NSA_D):
            selq[p, p * NSA_D + dd, dd] = 1.0
            place[p, dd, p * NSA_D + dd] = 1.0
    ovlt = np.zeros((nblk, nc), np.float32)
    for c in range(n_cmp):
        for jb in range(nblk):
            if c * CMP_STRIDE < (jb + 1) * SEL_BLOCK and c * CMP_STRIDE + CMP_BLOCK - 1 >= jb * SEL_BLOCK:
                ovlt[jb, c] = 1.0
    blk = np.zeros((s, LANES), np.float32)
    blk[np.arange(s), NSA_D + np.arange(s) // SEL_BLOCK] = NEG
    r = np.kron(np.eye(512 // 64, dtype=np.float32), np.ones((64, 64), np.float32))
    return (jnp.asarray(selq, BF16), jnp.asarray(place, BF16), jnp.asarray(ovlt, BF16),
            jnp.asarray(blk, BF16), jnp.asarray(r, BF16))


def kernel(x, w_in, w_branch_a, w_branch_b, w_out, norm_ffn1, norm_mix, norm_ffn2, ffn1_w1, ffn1_w3, ffn1_w2, ffn2_w1, ffn2_w3, ffn2_w2, da_q_gain, da_k_gain, da_lambda_q1, da_lambda_k1, da_lambda_q2, da_lambda_k2, da_subln_gain, nsa_q_gain, nsa_k_gain, cmp_pe_k, cmp_w1_k, cmp_w2_k, cmp_pe_v, cmp_w1_v, cmp_w2_v, rel_bias_table):
    b, s, d = x.shape
    depth = w_in.shape[0]
    t = TILE
    assert d == D_MODEL and s % t == 0 and WINDOW % t == 0 and s >= WINDOW
    assert SEL_TOPK <= s // SEL_BLOCK <= LANES - NSA_D
    n = b * s
    nqt = s // t
    nc = s // CMP_STRIDE
    half = CMP_STRIDE * NSA_D

    selq, place, ovlt, blk, r512 = _static_tables(s)
    tab_a = _tile_bias(rel_bias_table[:, :DA_HEADS], t, False)
    tab_b = _lanes_by_head(_tile_bias(rel_bias_table[:, DA_HEADS:], t, True))
    longc = _lanes_by_head(_cmp_bias(rel_bias_table[:, DA_HEADS:], t, nqt))

    x2 = x.reshape(n, d)
    row = lambda v: v.reshape(1, -1).astype(F32)
    tile = lambda v, k: jnp.tile(v.astype(F32), k).reshape(1, -1)
    padk = lambda w: jnp.pad(w.reshape(d, NSA_G, NSA_D), ((0, 0), (0, 0), (0, LANES - NSA_D))).reshape(d, NSA_G * LANES)
    for l in range(depth):
        x2 = _ffn(x2, row(norm_ffn1[l]), ffn1_w1[l].astype(BF16), ffn1_w3[l].astype(BF16), ffn1_w2[l].astype(BF16))

        wl = w_in[l]
        w_row = jnp.concatenate([wl[:, 0:1024], wl[:, 1536:2048], wl[:, 2048:2304],
                                 padk(wl[:, 2304:2432]), padk(wl[:, 2560:2688])], axis=1).astype(BF16)
        w_ng = jnp.pad(wl[:, 2816:2840].reshape(d, NSA_G, NSA_P * 3), ((0, 0), (0, 0), (0, GATE_ROWS - NSA_P * 3)))
        w_col = jnp.concatenate([wl[:, 1024:1536], wl[:, 2432:2560], wl[:, 2688:2816],
                                 w_ng.reshape(d, NSA_G * GATE_ROWS)], axis=1).T.astype(BF16)
        dq, dk, nq, kcvc, kaug, vt, gates = _proj(
            x2, row(norm_mix[l]), w_row, w_col, r512, blk,
            tile(da_q_gain[l], 8), tile(da_k_gain[l], 8), tile(nsa_q_gain[l], 8),
            tile(nsa_k_gain[l, 1], 4), tile(nsa_k_gain[l, 2], 4), b, s)
        vt = vt.reshape(b, nqt, DA_WIDTH + 2 * NSA_KV, t)

        lam_init = 0.8 - 0.6 * math.exp(-0.3 * l)
        lamv = jnp.stack([da_lambda_q1[l], da_lambda_k1[l], da_lambda_q2[l], da_lambda_k2[l]]).astype(F32)
        ya = _diff_attention(dq.reshape(b, s, 512), dk.reshape(b, s, 512), vt,
                             tab_a, lamv, row(da_subln_gain[l]), lam_init)

        chunks = kcvc.reshape(b, nc, CMP_STRIDE, 2, NSA_G, NSA_D).transpose(3, 0, 4, 1, 2, 5).reshape(2, b, NSA_G, nc, half)
        kc, vct = _compress(chunks[0], chunks[1],
                            cmp_pe_k[l].reshape(2, half).astype(F32), cmp_pe_v[l].reshape(2, half).astype(F32),
                            cmp_w1_k[l].reshape(2, half, CMP_HIDDEN).astype(BF16),
                            cmp_w1_v[l].reshape(2, half, CMP_HIDDEN).astype(BF16),
                            jnp.pad(cmp_w2_k[l], ((0, 0), (0, LANES - NSA_D))).astype(BF16),
                            cmp_w2_v[l].T.astype(BF16),
                            jnp.pad(row(nsa_k_gain[l, 0]), ((0, 0), (0, LANES - NSA_D))))
        yb = _nsa_attention(nq.reshape(b, s, 512), kaug, vt, kc, vct, longc, tab_b, gates, selq, place, ovlt)

        x2 = _merge(x2, ya.reshape(n, DA_WIDTH), yb.reshape(n, NSA_WIDTH), row(norm_mix[l]),
                    wl[:, 2840:].astype(BF16), w_branch_a[l].astype(BF16), w_branch_b[l].astype(BF16),
                    w_out[l].astype(BF16))
        x2 = _ffn(x2, row(norm_ffn2[l]), ffn2_w1[l].astype(BF16), ffn2_w3[l].astype(BF16), ffn2_w2[l].astype(BF16))
    return x2.reshape(b, s, d)
```

```python
import functools
import math

import numpy as np
import jax
import jax.numpy as jnp
from jax import lax
from jax.experimental import pallas as pl
from jax.experimental.pallas import tpu as pltpu

F32 = jnp.float32
BF16 = jnp.bfloat16

D_MODEL = 1024
DA_HEADS = 4
DA_HEAD_DIM = 64
DA_V_DIM = 2 * DA_HEAD_DIM
DA_WIDTH = DA_HEADS * DA_V_DIM
NSA_HEADS = 8
NSA_G = 2
NSA_P = NSA_HEADS // NSA_G
NSA_D = 64
NSA_WIDTH = NSA_HEADS * NSA_D
NSA_KV = NSA_G * NSA_D
CMP_BLOCK = 32
CMP_STRIDE = 16
CMP_HIDDEN = 128
SEL_BLOCK = 64
SEL_TOPK = 8
SEL_FORCE = 1e4
WINDOW = 512
N_BUCKETS = 32
MAX_DISTANCE = 1024
D_FF = 2816
EPS = 1e-6
NEG = -1e30

LANES = 128
VMEM_LIMIT = 52 * 1024 * 1024

TILE = 256
CW = LANES
FLASH_UNROLL = 4
GATE_ROWS = 16
ONES_ROWS = 16
DA_VT_ROWS = DA_V_DIM + ONES_ROWS
NSA_VT_ROWS = NSA_D + ONES_ROWS
LOG2E = math.log2(math.e)

COL_ROWS = DA_WIDTH + 2 * NSA_KV + NSA_G * GATE_ROWS


def _cparams(sem):
    return pltpu.CompilerParams(dimension_semantics=sem, vmem_limit_bytes=VMEM_LIMIT)


def _const_spec(shape):
    nd = len(shape)
    return pl.BlockSpec(shape, lambda *_: (0,) * nd)


def _rms(xf, g):
    ms = jnp.mean(xf * xf, axis=-1, keepdims=True)
    return xf * lax.rsqrt(ms + EPS) * g


def _dot(a, b):
    return jnp.dot(a, b, preferred_element_type=F32)


def _dot_nt(a, b):
    return lax.dot_general(a, b, (((1,), (1,)), ((), ())), preferred_element_type=F32)


def _bf16_pieces(x, parts):
    out = []
    r = x
    for i in range(parts):
        piece = r.astype(BF16)
        out.append(piece)
        if i + 1 < parts:
            r = r - piece.astype(F32)
    return out


def _ffn_kernel(x_ref, g_ref, w1_ref, w3_ref, w2_ref, o_ref):
    x = x_ref[...]
    h = _rms(x, g_ref[...]).astype(BF16)
    a = _dot(h, w1_ref[...])
    b = _dot(h, w3_ref[...])
    t = (jax.nn.silu(a) * b).astype(BF16)
    o_ref[...] = x + 0.5 * _dot(t, w2_ref[...])


def _ffn(x2, g, w1, w3, w2):
    n, d = x2.shape
    tm = TILE
    return pl.pallas_call(
        _ffn_kernel,
        grid=(n // tm,),
        in_specs=[
            pl.BlockSpec((tm, d), lambda i: (i, 0)),
            _const_spec((1, d)),
            _const_spec(w1.shape),
            _const_spec(w3.shape),
            _const_spec(w2.shape),
        ],
        out_specs=pl.BlockSpec((tm, d), lambda i: (i, 0)),
        out_shape=jax.ShapeDtypeStruct((n, d), F32),
        compiler_params=_cparams(("parallel",)),
        name="ffn_half_step",
    )(x2, g, w1, w3, w2)


def _group_rms(z, r_ref, gain):
    w = z.shape[-1]
    r = r_ref[0:w, 0:w]
    hi, lo = _bf16_pieces(z * z, 2)
    ss = _dot(hi, r) + _dot(lo, r)
    return z * lax.rsqrt(ss * (1.0 / 64.0) + EPS) * gain


def _proj_kernel(x_ref, g_ref, w_ref, wt_ref, r_ref, blk_ref, gq_ref, gk_ref, gnq_ref, gks_ref, gkw_ref,
                 dq_ref, dk_ref, nq_ref, kcvc_ref, kaug_ref, vta_ref, vtb_ref, gate_ref):
    h = _rms(x_ref[...], g_ref[...]).astype(BF16)
    z = _dot(h, w_ref[...])
    zt = _dot_nt(wt_ref[...], h)
    scale = DA_HEAD_DIM ** -0.5 * LOG2E
    dq_ref[...] = (_group_rms(z[:, 0:512], r_ref, gq_ref[...]) * scale).astype(BF16)
    dk_ref[...] = _group_rms(z[:, 512:1024], r_ref, gk_ref[...]).astype(BF16)
    nq_ref[...] = (_group_rms(z[:, 1024:1536], r_ref, gnq_ref[...]) * scale).astype(BF16)
    kcvc_ref[...] = z[:, 1536:1792]
    ks = _group_rms(z[:, 1792:2048], r_ref, gks_ref[...])
    kw = _group_rms(z[:, 2048:2304], r_ref, gkw_ref[...])
    blk = blk_ref[...].astype(F32)
    for g in range(NSA_G):
        kaug_ref[0, 0, g] = (ks[:, g * LANES:(g + 1) * LANES] + blk).astype(BF16)
        kaug_ref[0, 1, g] = kw[:, g * LANES:(g + 1) * LANES].astype(BF16)
    ones = jnp.ones((ONES_ROWS, zt.shape[1]), BF16)
    for hd in range(DA_HEADS):
        vta_ref[0, hd * DA_VT_ROWS:hd * DA_VT_ROWS + DA_V_DIM] = zt[hd * DA_V_DIM:(hd + 1) * DA_V_DIM].astype(BF16)
        vta_ref[0, hd * DA_VT_ROWS + DA_V_DIM:(hd + 1) * DA_VT_ROWS] = ones
    for blk4 in range(2 * NSA_G):
        src = DA_WIDTH + (blk4 % 2) * NSA_KV + (blk4 // 2) * NSA_D
        vtb_ref[0, blk4 * NSA_VT_ROWS:blk4 * NSA_VT_ROWS + NSA_D] = zt[src:src + NSA_D].astype(BF16)
        vtb_ref[0, blk4 * NSA_VT_ROWS + NSA_D:(blk4 + 1) * NSA_VT_ROWS] = ones
    gate_ref[0] = jax.nn.sigmoid(zt[DA_WIDTH + 2 * NSA_KV:COL_ROWS])


def _proj(x2, g, w, wt, r, blk, gq, gk, gnq, gks, gkw, b, s):
    n, d = x2.shape
    tm = TILE
    nt = s // tm
    row = lambda wd: pl.BlockSpec((tm, wd), lambda i: (i, 0))
    col = lambda rows: pl.BlockSpec((1, rows, tm), lambda i: (i, 0, 0))
    return pl.pallas_call(
        _proj_kernel,
        grid=(n // tm,),
        in_specs=[row(d), _const_spec((1, d)), _const_spec(w.shape), _const_spec(wt.shape),
                  _const_spec(r.shape), pl.BlockSpec((tm, LANES), lambda i: (i % nt, 0)),
                  _const_spec((1, 512)), _const_spec((1, 512)), _const_spec((1, 512)),
                  _const_spec((1, 256)), _const_spec((1, 256))],
        out_specs=[row(512), row(512), row(512), row(256),
                   pl.BlockSpec((1, 2, NSA_G, tm, LANES), lambda i: (i // nt, 0, 0, i % nt, 0)),
                   col(DA_HEADS * DA_VT_ROWS), col(2 * NSA_G * NSA_VT_ROWS), col(NSA_G * GATE_ROWS)],
        out_shape=[
            jax.ShapeDtypeStruct((n, 512), BF16),
            jax.ShapeDtypeStruct((n, 512), BF16),
            jax.ShapeDtypeStruct((n, 512), BF16),
            jax.ShapeDtypeStruct((n, 256), F32),
            jax.ShapeDtypeStruct((b, 2, NSA_G, s, LANES), BF16),
            jax.ShapeDtypeStruct((n // tm, DA_HEADS * DA_VT_ROWS, tm), BF16),
            jax.ShapeDtypeStruct((n // tm, 2 * NSA_G * NSA_VT_ROWS, tm), BF16),
            jax.ShapeDtypeStruct((n // tm, NSA_G * GATE_ROWS, tm), F32),
        ],
        compiler_params=_cparams(("parallel",)),
        name="norm_in_proj",
    )(x2, g, w, wt, r, blk, gq, gk, gnq, gks, gkw)


def _cmp_kernel(ck_ref, cv_ref, pek_ref, pev_ref, w1k_ref, w1v_ref, w2k_ref, w2vt_ref, gk_ref, kc_ref, vct_ref):
    def hidden(c, pe_ref, w1_ref):
        lo = _dot((c + pe_ref[0:1, :]).astype(BF16), w1_ref[0])
        hi = _dot((c + pe_ref[1:2, :]).astype(BF16), w1_ref[1])
        nrow = hi.shape[0]
        return jax.nn.gelu(lo + pltpu.roll(hi, nrow - 1, axis=0)).astype(BF16)
    kc = _dot(hidden(ck_ref[0, 0], pek_ref, w1k_ref), w2k_ref[...])
    ms = jnp.sum(kc * kc, axis=-1, keepdims=True) * (1.0 / NSA_D)
    kc_ref[0, 0] = (kc * lax.rsqrt(ms + EPS) * gk_ref[...]).astype(BF16)
    vct_ref[0, 0] = _dot_nt(w2vt_ref[...], hidden(cv_ref[0, 0], pev_ref, w1v_ref)).astype(BF16)


def _compress(ck, cv, pek, pev, w1k, w1v, w2k, w2vt, gk):
    b, g, nc, cw = ck.shape
    chunk = pl.BlockSpec((1, 1, nc, cw), lambda i, j: (i, j, 0, 0))
    return pl.pallas_call(
        _cmp_kernel,
        grid=(b, g),
        in_specs=[chunk, chunk, _const_spec(pek.shape), _const_spec(pev.shape),
                  _const_spec(w1k.shape), _const_spec(w1v.shape),
                  _const_spec(w2k.shape), _const_spec(w2vt.shape), _const_spec(gk.shape)],
        out_specs=[pl.BlockSpec((1, 1, nc, LANES), lambda i, j: (i, j, 0, 0)),
                   pl.BlockSpec((1, 1, NSA_D, nc), lambda i, j: (i, j, 0, 0))],
        out_shape=[jax.ShapeDtypeStruct((b, g, nc, LANES), BF16),
                   jax.ShapeDtypeStruct((b, g, NSA_D, nc), BF16)],
        compiler_params=_cparams(("parallel", "parallel")),
        name="block_compress",
    )(ck, cv, pek, pev, w1k, w1v, w2k, w2vt, gk)


def _flash_list(n, qk_fn, vt_fn, bias_fn, state_fn, s_ref, p_ref, alpha_ref):
    width = s_ref.shape[2]

    def pv(i, slot):
        acc_ref = state_fn(i)[1]
        acc_ref[...] = alpha_ref[slot] * acc_ref[...] + _dot(vt_fn(i), p_ref[slot])

    def softmax(i, slot):
        m_ref = state_fn(i)[0]
        for c in range(width // CW):
            cs = slice(c * CW, (c + 1) * CW)
            s = s_ref[slot, :, cs] + bias_fn(i, cs)
            m_old = m_ref[:, cs]
            m_new = jnp.maximum(m_old, jnp.max(s, axis=0, keepdims=True))
            p_ref[slot, :, cs] = jnp.exp2(s - m_new).astype(BF16)
            alpha_ref[slot, :, cs] = jnp.exp2(m_old - m_new)
            m_ref[:, cs] = m_new

    s_ref[0] = qk_fn(0)
    softmax(0, 0)
    s_ref[1] = qk_fn(1)

    def body(g, carry):
        first = 1 + FLASH_UNROLL * g
        for u in range(FLASH_UNROLL):
            slot = (u + 1) % 2
            pv(first + u - 1, 1 - slot)
            softmax(first + u, slot)
            s_ref[1 - slot] = qk_fn(first + u + 1)
        return carry

    groups = (n - 1 + FLASH_UNROLL - 1) // FLASH_UNROLL
    lax.fori_loop(0, groups, body, 0)
    pv(FLASH_UNROLL * groups, 0)


def _to_rows(xt):
    t = xt.shape[1]
    return jnp.concatenate([xt[:, c * LANES:(c + 1) * LANES].T for c in range(t // LANES)], axis=0)


def _da_kernel(lam_init, nd, nqt, qt_of, kt_of, q_ref, k_ref, vt_ref, tab_ref, lamv_ref, gs_ref, o_ref,
               qq_ref, s_ref, p_ref, alpha_ref, m_ref, acc_ref):
    t = TILE
    lane = lax.broadcasted_iota(jnp.int32, (t, LANES), 1)
    for qt in range(nqt):
        q = q_ref[0, qt * t:(qt + 1) * t, :].astype(F32)
        qq_ref[qt, 0:t, :] = jnp.where(lane < DA_HEAD_DIM, q, 0.0).astype(BF16)
        qq_ref[qt, t:2 * t, :] = jnp.where(lane >= DA_HEAD_DIM, q, 0.0).astype(BF16)
    m_ref[...] = jnp.full(m_ref.shape, NEG, F32)
    acc_ref[...] = jnp.zeros(acc_ref.shape, F32)

    key_tile = lambda i: jnp.minimum(kt_of[i], nqt - 1)

    def qk(i):
        return _dot_nt(k_ref[0, pl.ds(pl.multiple_of(key_tile(i) * t, t), t), :], qq_ref[qt_of[i]])

    def bias(i, cs):
        d = jnp.clip(qt_of[i] - kt_of[i], -1, nd)
        return tab_ref[0, d + 1, :, slice(cs.start % t, cs.start % t + CW)]

    _flash_list(nqt * (nqt + 1) // 2, qk, lambda i: vt_ref[0, key_tile(i)], bias,
                lambda i: (m_ref.at[qt_of[i]], acc_ref.at[qt_of[i]]), s_ref, p_ref, alpha_ref)

    lv = lamv_ref[...]
    lam = (jnp.exp(jnp.sum(lv[0:1] * lv[1:2], axis=-1, keepdims=True))
           - jnp.exp(jnp.sum(lv[2:3] * lv[3:4], axis=-1, keepdims=True)) + lam_init)

    def finish(qt, carry):
        ot = acc_ref[qt, 0:DA_V_DIM, :] / acc_ref[qt, DA_V_DIM:DA_V_DIM + 1, :]
        y = _to_rows(ot[:, 0:t] - lam * ot[:, t:2 * t])
        o_ref[0, pl.ds(pl.multiple_of(qt * t, t), t), :] = (_rms(y, gs_ref[...]) * (1.0 - lam_init)).astype(BF16)
        return carry

    lax.fori_loop(0, nqt, finish, 0)


def _diff_attention(dq, dk, vt, tab, lamv, gs, lam_init):
    b, s, _ = dq.shape
    t = TILE
    nqt = s // t
    nd = tab.shape[1] - 2
    pairs = [(qt, kt) for qt in range(nqt) for kt in range(qt + 1)] + [(nqt - 1, nqt)] * (FLASH_UNROLL + 2)
    qt_of = jnp.asarray(np.array([p[0] for p in pairs], np.int32))
    kt_of = jnp.asarray(np.array([p[1] for p in pairs], np.int32))
    seq = pl.BlockSpec((1, s, LANES), lambda i, h, *_: (i, 0, h))
    return pl.pallas_call(
        functools.partial(_da_kernel, lam_init, nd, nqt),
        grid_spec=pltpu.PrefetchScalarGridSpec(
            num_scalar_prefetch=2,
            grid=(b, DA_HEADS),
            in_specs=[
                seq, seq,
                pl.BlockSpec((1, nqt, DA_VT_ROWS, t), lambda i, h, *_: (i, 0, h, 0)),
                pl.BlockSpec((1,) + tab.shape[1:], lambda i, h, *_: (h, 0, 0, 0)),
                pl.BlockSpec(lamv.shape, lambda i, h, *_: (0, 0)),
                pl.BlockSpec(gs.shape, lambda i, h, *_: (0, 0)),
            ],
            out_specs=seq,
            scratch_shapes=[
                pltpu.VMEM((nqt, 2 * t, LANES), BF16),
                pltpu.VMEM((2, t, 2 * t), F32),
                pltpu.VMEM((2, t, 2 * t), BF16),
                pltpu.VMEM((2, 1, 2 * t), F32),
                pltpu.VMEM((nqt, 1, 2 * t), F32),
                pltpu.VMEM((nqt, DA_VT_ROWS, 2 * t), F32),
            ]),
        out_shape=jax.ShapeDtypeStruct((b, s, DA_WIDTH), BF16),
        compiler_params=_cparams(("parallel", "parallel")),
        name="diff_attention",
    )(qt_of, kt_of, dq, dk, vt, tab, lamv, gs)


def _nsa_kernel(nd, nqt, q_ref, kaug_ref, vt_ref, kc_ref, vct_ref, longc_ref, tab_ref, gate_ref,
                selq_ref, place_ref, ovlt_ref, o_ref,
                qz_ref, qs_ref, s_ref, p_ref, alpha_ref, m_ref, acc_ref, comb_ref, psum_ref):
    t = TILE
    halves = t // CW
    nchunk = NSA_P * halves
    qi = pl.program_id(2)
    win_tiles = WINDOW // t
    win_edge = nd + 2

    def gate_row(r):
        return jnp.concatenate([gate_ref[0, 3 * p + r:3 * p + r + 1, :] for p in range(NSA_P)], axis=1)

    qt = q_ref[0]
    for p in range(NSA_P):
        qz_ref[p * t:(p + 1) * t, :] = _dot(qt, selq_ref[p]).astype(BF16)

    kc = kc_ref[0, 0]
    vct = vct_ref[0, 0]
    nc = kc.shape[0]
    cstart = pl.multiple_of((nqt - 1 - qi) * (t // CMP_STRIDE), t // CMP_STRIDE)
    for c in range(nchunk):
        p, half = divmod(c, halves)
        cs = slice(c * CW, (c + 1) * CW)
        hs = slice(half * CW, (half + 1) * CW)
        s = _dot_nt(kc, qz_ref[cs, :]) + longc_ref[0, pl.ds(cstart, nc), cs]
        m = jnp.max(s, axis=0, keepdims=True)
        e = jnp.exp2(s - m)
        pc = jnp.where(m > 0.5 * NEG, e / jnp.sum(e, axis=0, keepdims=True), 0.0)
        comb_ref[:, cs] = gate_ref[0, 3 * p:3 * p + 1, hs] * _dot(vct, pc.astype(BF16))
        if p == 0:
            psum_ref[:, hs] = pc
        else:
            psum_ref[:, hs] += pc

    ovlt = ovlt_ref[...]
    imp = None
    for piece in _bf16_pieces(psum_ref[...], 3):
        d = _dot(ovlt, piece)
        imp = d if imp is None else imp + d
    nblk = imp.shape[0]
    jrow = lax.broadcasted_iota(jnp.int32, (nblk, t), 0)
    cur = (qi * t + lax.broadcasted_iota(jnp.int32, (nblk, t), 1)) // SEL_BLOCK
    forced = (jrow == 0) | (jrow == cur) | (jrow == cur - 1)
    score = jnp.where(jrow > cur, -SEL_FORCE, imp + jnp.where(forced, SEL_FORCE, 0.0))
    lowest = jnp.float32(-3e38)
    jrow_f = jrow.astype(F32)
    nsel = jnp.ones((nblk, t), F32)
    for _ in range(SEL_TOPK):
        mx = jnp.max(score, axis=0, keepdims=True)
        first = jnp.min(jnp.where(score == mx, jrow_f, 2.0 * LANES), axis=0, keepdims=True)
        hit = jrow_f == first
        nsel = jnp.where(hit, 0.0, nsel)
        score = jnp.where(hit, lowest, score)
    parts = [jnp.zeros((NSA_D, t), F32), nsel]
    if LANES - NSA_D - nblk:
        parts.append(jnp.zeros((LANES - NSA_D - nblk, t), F32))
    nsel_rows = _to_rows(jnp.concatenate(parts, axis=0))
    for p in range(NSA_P):
        rs = slice(p * t, (p + 1) * t)
        qs_ref[rs, :] = (qz_ref[rs, :].astype(F32) + nsel_rows).astype(BF16)

    m_ref[...] = jnp.full(m_ref.shape, NEG, F32)
    acc_ref[...] = jnp.zeros(acc_ref.shape, F32)
    nwin = win_tiles + 1

    def entry(i):
        is_win = jnp.asarray(i < nwin)
        kt = jnp.where(is_win, qi - win_tiles + i, i - nwin)
        return is_win, kt, qi - kt

    def qk(i):
        is_win, kt, _ = entry(i)
        rows = pl.ds(pl.multiple_of(jnp.clip(kt, 0, qi) * t, t), t)
        return _dot_nt(kaug_ref[0, is_win.astype(jnp.int32), 0, rows, :], qs_ref[...])

    def vt(i):
        is_win, kt, _ = entry(i)
        rows = pl.ds(pl.multiple_of(is_win.astype(jnp.int32) * NSA_VT_ROWS, NSA_VT_ROWS), NSA_VT_ROWS)
        return vt_ref[0, jnp.clip(kt, 0, qi), rows, :]

    def bias(i, cs):
        is_win, kt, d = entry(i)
        idx = jnp.where(is_win & (d == win_tiles), win_edge, jnp.clip(d, -1, nd) + 1)
        return tab_ref[0, jnp.where(kt < 0, 0, idx), :, cs]

    def state(i):
        st = jnp.asarray(i < nwin).astype(jnp.int32)
        return m_ref.at[st], acc_ref.at[st]

    _flash_list(nwin + qi + 1, qk, vt, bias, state, s_ref, p_ref, alpha_ref)

    def branch_out(st):
        return acc_ref[st, 0:NSA_D, :] / acc_ref[st, NSA_D:NSA_D + 1, :]

    comb = comb_ref[...] + gate_row(1) * branch_out(0) + gate_row(2) * branch_out(1)

    pad = jnp.zeros((LANES - NSA_D, t), F32)
    out = None
    for p in range(NSA_P):
        rows = _to_rows(jnp.concatenate([comb[:, p * t:(p + 1) * t], pad], axis=0)).astype(BF16)
        d = _dot(rows, place_ref[p])
        out = d if out is None else out + d
    o_ref[0] = out.astype(BF16)


def _nsa_attention(nq, kaug, vt, kc, vct, longc, tab, gates, selq, place, ovlt):
    b, s, _ = nq.shape
    t = TILE
    nqt = s // t
    nc = kc.shape[2]
    nd = tab.shape[1] - 3
    gw = NSA_P * NSA_D
    return pl.pallas_call(
        functools.partial(_nsa_kernel, nd, nqt),
        grid=(b, NSA_G, nqt),
        in_specs=[
            pl.BlockSpec((1, t, gw), lambda i, g, j: (i, j, g)),
            pl.BlockSpec((1, 2, 1, s, LANES), lambda i, g, j: (i, 0, g, 0, 0)),
            pl.BlockSpec((1, nqt, 2 * NSA_VT_ROWS, t), lambda i, g, j: (i, 0, g, 0)),
            pl.BlockSpec((1, 1, nc, LANES), lambda i, g, j: (i, g, 0, 0)),
            pl.BlockSpec((1, 1, NSA_D, nc), lambda i, g, j: (i, g, 0, 0)),
            pl.BlockSpec((1,) + longc.shape[1:], lambda i, g, j: (g, 0, 0)),
            pl.BlockSpec((1,) + tab.shape[1:], lambda i, g, j: (g, 0, 0, 0)),
            pl.BlockSpec((1, GATE_ROWS, t), lambda i, g, j: (i * nqt + j, g, 0)),
            _const_spec(selq.shape), _const_spec(place.shape), _const_spec(ovlt.shape),
        ],
        out_specs=pl.BlockSpec((1, t, gw), lambda i, g, j: (i, j, g)),
        out_shape=jax.ShapeDtypeStruct((b, s, NSA_WIDTH), BF16),
        scratch_shapes=[
            pltpu.VMEM((NSA_P * t, LANES), BF16),
            pltpu.VMEM((NSA_P * t, LANES), BF16),
            pltpu.VMEM((2, t, NSA_P * t), F32),
            pltpu.VMEM((2, t, NSA_P * t), BF16),
            pltpu.VMEM((2, 1, NSA_P * t), F32),
            pltpu.VMEM((2, 1, NSA_P * t), F32),
            pltpu.VMEM((2, NSA_VT_ROWS, NSA_P * t), F32),
            pltpu.VMEM((NSA_D, NSA_P * t), F32),
            pltpu.VMEM((nc, t), F32),
        ],
        compiler_params=_cparams(("parallel", "parallel", "arbitrary")),
        name="nsa_attention",
    )(nq, kaug, vt, kc, vct, longc, tab, gates, selq, place, ovlt)


def _merge_kernel(x_ref, ya_ref, yb_ref, g_ref, wmg_ref, pa_ref, pb_ref, wo_ref, o_ref):
    x = x_ref[...]
    h = _rms(x, g_ref[...]).astype(BF16)
    mg = jax.nn.sigmoid(_dot(h, wmg_ref[...]))
    merged = mg[:, 0:D_MODEL] * _dot(ya_ref[...], pa_ref[...]) + mg[:, D_MODEL:] * _dot(yb_ref[...], pb_ref[...])
    o_ref[...] = x + _dot(merged.astype(BF16), wo_ref[...])


def _merge(x2, ya, yb, g, wmg, pa, pb, wo):
    n, d = x2.shape
    tm = TILE
    row = lambda wd: pl.BlockSpec((tm, wd), lambda i: (i, 0))
    return pl.pallas_call(
        _merge_kernel,
        grid=(n // tm,),
        in_specs=[row(d), row(DA_WIDTH), row(NSA_WIDTH), _const_spec((1, d)),
                  _const_spec(wmg.shape), _const_spec(pa.shape), _const_spec(pb.shape), _const_spec(wo.shape)],
        out_specs=row(d),
        out_shape=jax.ShapeDtypeStruct((n, d), F32),
        compiler_params=_cparams(("parallel",)),
        name="gated_merge_out_proj",
    )(x2, ya, yb, g, wmg, pa, pb, wo)


def _rel_bucket(dist):
    n = jnp.maximum(dist, 0)
    max_exact = N_BUCKETS // 2
    nf = jnp.maximum(n, 1).astype(jnp.float32)
    large = max_exact + (jnp.log(nf / max_exact) / math.log(MAX_DISTANCE / max_exact)
                         * (N_BUCKETS - max_exact)).astype(jnp.int32)
    large = jnp.minimum(large, N_BUCKETS - 1)
    return jnp.where(n < max_exact, n, large)


def _bias_of(table, dist):
    bucket = _rel_bucket(dist)
    col = lambda v: v.reshape((-1,) + (1,) * dist.ndim)
    out = jnp.zeros((table.shape[1],) + dist.shape, F32)
    for bkt in range(N_BUCKETS):
        out = jnp.where(bucket == bkt, col(table[bkt].astype(F32)), out)
    return jnp.where(dist >= 0, out, NEG)


def _far_tiles(t):
    return -(-(MAX_DISTANCE + t - 1) // t)


def _tile_bias(table, t, window_edge):
    nd = _far_tiles(t)
    j = jnp.arange(t)[:, None]
    i = jnp.arange(t)[None, :]
    dist = jnp.arange(-1, nd + 1)[:, None, None] * t + i - j
    bias = _bias_of(table, dist)
    if window_edge:
        we = WINDOW // t + 1
        edge = jnp.where(dist[we] < WINDOW, bias[:, we], NEG)
        bias = jnp.concatenate([bias, edge[:, None]], axis=1)
    return bias


def _cmp_bias(table, t, nqt):
    cpt = t // CMP_STRIDE
    r = jnp.arange((2 * nqt - 1) * cpt)[:, None]
    dist = (nqt - 1 - r // cpt) * t + jnp.arange(t)[None, :] - ((r % cpt) * CMP_STRIDE + CMP_BLOCK - 1)
    return _bias_of(table, dist)


def _lanes_by_head(x):
    x = x.reshape((NSA_G, NSA_P) + x.shape[1:])
    x = jnp.moveaxis(x, 1, -2)
    return x.reshape(x.shape[:-2] + (NSA_P * x.shape[-1],))


def _static_tables(s):
    nc = s // CMP_STRIDE
    n_cmp = nc - CMP_BLOCK // CMP_STRIDE + 1
    nblk = s // SEL_BLOCK
    selq = np.zeros((NSA_P, NSA_P * NSA_D, LANES), np.float32)
    place = np.zeros((NSA_P, LANES, NSA_P * NSA_D), np.float32)
    for p in range(NSA_P):
        for dd in range(NSA_D):
            selq[p, p * NSA_D + dd, dd] = 1.0
            place[p, dd, p * NSA_D + dd] = 1.0
    ovlt = np.zeros((nblk, nc), np.float32)
    for c in range(n_cmp):
        for jb in range(nblk):
            if c * CMP_STRIDE < (jb + 1) * SEL_BLOCK and c * CMP_STRIDE + CMP_BLOCK - 1 >= jb * SEL_BLOCK:
                ovlt[jb, c] = 1.0
    blk = np.zeros((s, LANES), np.float32)
    blk[np.arange(s), NSA_D + np.arange(s) // SEL_BLOCK] = NEG
    r = np.kron(np.eye(512 // 64, dtype=np.float32), np.ones((64, 64), np.float32))
    return (jnp.asarray(selq, BF16), jnp.asarray(place, BF16), jnp.asarray(ovlt, BF16),
            jnp.asarray(blk, BF16), jnp.asarray(r, BF16))


def kernel(x, w_in, w_branch_a, w_branch_b, w_out, norm_ffn1, norm_mix, norm_ffn2, ffn1_w1, ffn1_w3, ffn1_w2, ffn2_w1, ffn2_w3, ffn2_w2, da_q_gain, da_k_gain, da_lambda_q1, da_lambda_k1, da_lambda_q2, da_lambda_k2, da_subln_gain, nsa_q_gain, nsa_k_gain, cmp_pe_k, cmp_w1_k, cmp_w2_k, cmp_pe_v, cmp_w1_v, cmp_w2_v, rel_bias_table):
    b, s, d = x.shape
    depth = w_in.shape[0]
    t = TILE
    assert d == D_MODEL and s % t == 0 and WINDOW % t == 0 and s >= WINDOW
    assert SEL_TOPK <= s // SEL_BLOCK <= LANES - NSA_D
    n = b * s
    nqt = s // t
    nc = s // CMP_STRIDE
    half = CMP_STRIDE * NSA_D

    selq, place, ovlt, blk, r512 = _static_tables(s)
    tab_a = _tile_bias(rel_bias_table[:, :DA_HEADS], t, False) * LOG2E
    tab_b = _lanes_by_head(_tile_bias(rel_bias_table[:, DA_HEADS:], t, True)) * LOG2E
    longc = _lanes_by_head(_cmp_bias(rel_bias_table[:, DA_HEADS:], t, nqt)) * LOG2E

    x2 = x.reshape(n, d)
    row = lambda v: v.reshape(1, -1).astype(F32)
    tile = lambda v, k: jnp.tile(v.astype(F32), k).reshape(1, -1)
    padk = lambda w: jnp.pad(w.reshape(d, NSA_G, NSA_D), ((0, 0), (0, 0), (0, LANES - NSA_D))).reshape(d, NSA_G * LANES)
    for l in range(depth):
        x2 = _ffn(x2, row(norm_ffn1[l]), ffn1_w1[l].astype(BF16), ffn1_w3[l].astype(BF16), ffn1_w2[l].astype(BF16))

        wl = w_in[l]
        w_row = jnp.concatenate([wl[:, 0:1024], wl[:, 1536:2048], wl[:, 2048:2304],
                                 padk(wl[:, 2304:2432]), padk(wl[:, 2560:2688])], axis=1).astype(BF16)
        w_ng = jnp.pad(wl[:, 2816:2840].reshape(d, NSA_G, NSA_P * 3), ((0, 0), (0, 0), (0, GATE_ROWS - NSA_P * 3)))
        w_col = jnp.concatenate([wl[:, 1024:1536], wl[:, 2432:2560], wl[:, 2688:2816],
                                 w_ng.reshape(d, NSA_G * GATE_ROWS)], axis=1).T.astype(BF16)
        dq, dk, nq, kcvc, kaug, vta, vtb, gates = _proj(
            x2, row(norm_mix[l]), w_row, w_col, r512, blk,
            tile(da_q_gain[l], 8), tile(da_k_gain[l], 8), tile(nsa_q_gain[l], 8),
            tile(nsa_k_gain[l, 1], 4), tile(nsa_k_gain[l, 2], 4), b, s)
        vta = vta.reshape(b, nqt, DA_HEADS * DA_VT_ROWS, t)
        vtb = vtb.reshape(b, nqt, 2 * NSA_G * NSA_VT_ROWS, t)

        lam_init = 0.8 - 0.6 * math.exp(-0.3 * l)
        lamv = jnp.stack([da_lambda_q1[l], da_lambda_k1[l], da_lambda_q2[l], da_lambda_k2[l]]).astype(F32)
        ya = _diff_attention(dq.reshape(b, s, 512), dk.reshape(b, s, 512), vta,
                             tab_a, lamv, row(da_subln_gain[l]), lam_init)

        chunks = kcvc.reshape(b, nc, CMP_STRIDE, 2, NSA_G, NSA_D).transpose(3, 0, 4, 1, 2, 5).reshape(2, b, NSA_G, nc, half)
        kc, vct = _compress(chunks[0], chunks[1],
                            cmp_pe_k[l].reshape(2, half).astype(F32), cmp_pe_v[l].reshape(2, half).astype(F32),
                            cmp_w1_k[l].reshape(2, half, CMP_HIDDEN).astype(BF16),
                            cmp_w1_v[l].reshape(2, half, CMP_HIDDEN).astype(BF16),
                            jnp.pad(cmp_w2_k[l], ((0, 0), (0, LANES - NSA_D))).astype(BF16),
                            cmp_w2_v[l].T.astype(BF16),
                            jnp.pad(row(nsa_k_gain[l, 0]), ((0, 0), (0, LANES - NSA_D))))
        yb = _nsa_attention(nq.reshape(b, s, 512), kaug, vtb, kc, vct, longc, tab_b, gates, selq, place, ovlt)

        x2 = _merge(x2, ya.reshape(n, DA_WIDTH), yb.reshape(n, NSA_WIDTH), row(norm_mix[l]),
                    wl[:, 2840:].astype(BF16), w_branch_a[l].astype(BF16), w_branch_b[l].astype(BF16),
                    w_out[l].astype(BF16))
        x2 = _ffn(x2, row(norm_ffn2[l]), ffn2_w1[l].astype(BF16), ffn2_w3[l].astype(BF16), ffn2_w2[l].astype(BF16))
    return x2.reshape(b, s, d)
```

```python
import functools
import math

import numpy as np
import jax
import jax.numpy as jnp
from jax import lax
from jax.experimental import pallas as pl
from jax.experimental.pallas import tpu as pltpu

F32 = jnp.float32
BF16 = jnp.bfloat16

D_MODEL = 1024
DA_HEADS = 4
DA_HEAD_DIM = 64
DA_V_DIM = 2 * DA_HEAD_DIM
DA_WIDTH = DA_HEADS * DA_V_DIM
NSA_HEADS = 8
NSA_G = 2
NSA_P = NSA_HEADS // NSA_G
NSA_D = 64
NSA_WIDTH = NSA_HEADS * NSA_D
NSA_KV = NSA_G * NSA_D
CMP_BLOCK = 32
CMP_STRIDE = 16
CMP_HIDDEN = 128
SEL_BLOCK = 64
SEL_TOPK = 8
SEL_FORCE = 1e4
WINDOW = 512
N_BUCKETS = 32
MAX_DISTANCE = 1024
D_FF = 2816
EPS = 1e-6
NEG = -1e30

LANES = 128
VMEM_LIMIT = 52 * 1024 * 1024

TILE = 256
CW = LANES
FLASH_UNROLL = 4
GATE_ROWS = 16
ONES_ROWS = 16
DA_VT_ROWS = DA_V_DIM + ONES_ROWS
NSA_VT_ROWS = NSA_D + ONES_ROWS
LOG2E = math.log2(math.e)

COL_ROWS = DA_WIDTH + 2 * NSA_KV + NSA_G * GATE_ROWS


def _cparams(sem):
    return pltpu.CompilerParams(dimension_semantics=sem, vmem_limit_bytes=VMEM_LIMIT)


def _const_spec(shape):
    nd = len(shape)
    return pl.BlockSpec(shape, lambda *_: (0,) * nd)


def _rms(xf, g):
    ms = jnp.mean(xf * xf, axis=-1, keepdims=True)
    return xf * lax.rsqrt(ms + EPS) * g


def _dot(a, b):
    return jnp.dot(a, b, preferred_element_type=F32)


def _dot_nt(a, b):
    return lax.dot_general(a, b, (((1,), (1,)), ((), ())), preferred_element_type=F32)


def _bf16_pieces(x, parts):
    out = []
    r = x
    for i in range(parts):
        piece = r.astype(BF16)
        out.append(piece)
        if i + 1 < parts:
            r = r - piece.astype(F32)
    return out


def _ffn_kernel(x_ref, g_ref, w1_ref, w3_ref, w2_ref, o_ref):
    x = x_ref[...]
    h = _rms(x, g_ref[...]).astype(BF16)
    a = _dot(h, w1_ref[...])
    b = _dot(h, w3_ref[...])
    t = (jax.nn.silu(a) * b).astype(BF16)
    o_ref[...] = x + 0.5 * _dot(t, w2_ref[...])


def _ffn(x2, g, w1, w3, w2):
    n, d = x2.shape
    tm = TILE
    return pl.pallas_call(
        _ffn_kernel,
        grid=(n // tm,),
        in_specs=[
            pl.BlockSpec((tm, d), lambda i: (i, 0)),
            _const_spec((1, d)),
            _const_spec(w1.shape),
            _const_spec(w3.shape),
            _const_spec(w2.shape),
        ],
        out_specs=pl.BlockSpec((tm, d), lambda i: (i, 0)),
        out_shape=jax.ShapeDtypeStruct((n, d), F32),
        compiler_params=_cparams(("parallel",)),
        name="ffn_half_step",
    )(x2, g, w1, w3, w2)


def _group_rms(z, r_ref, gain):
    w = z.shape[-1]
    r = r_ref[0:w, 0:w]
    hi, lo = _bf16_pieces(z * z, 2)
    ss = _dot(hi, r) + _dot(lo, r)
    return z * lax.rsqrt(ss * (1.0 / 64.0) + EPS) * gain


def _proj_kernel(x_ref, g_ref, w_ref, wt_ref, r_ref, blk_ref, gq_ref, gk_ref, gnq_ref, gks_ref, gkw_ref,
                 dq_ref, dk_ref, nq_ref, kcvc_ref, kaug_ref, vta_ref, vtb_ref, gate_ref):
    h = _rms(x_ref[...], g_ref[...]).astype(BF16)
    z = _dot(h, w_ref[...])
    zt = _dot_nt(wt_ref[...], h)
    scale = DA_HEAD_DIM ** -0.5 * LOG2E
    dq_ref[...] = (_group_rms(z[:, 0:512], r_ref, gq_ref[...]) * scale).astype(BF16)
    dk_ref[...] = _group_rms(z[:, 512:1024], r_ref, gk_ref[...]).astype(BF16)
    for c0 in (1024, 1536):
        nq_ref[:, c0 - 1024:c0 - 512] = (_group_rms(z[:, c0:c0 + 512], r_ref, gnq_ref[...]) * scale).astype(BF16)
    kcvc_ref[...] = z[:, 2048:2560]
    ks = _group_rms(z[:, 2560:2816], r_ref, gks_ref[...])
    kw = _group_rms(z[:, 2816:3072], r_ref, gkw_ref[...])
    blk = blk_ref[...].astype(F32)
    for g in range(NSA_G):
        kaug_ref[0, 0, g] = (ks[:, g * LANES:(g + 1) * LANES] + blk).astype(BF16)
        kaug_ref[0, 1, g] = kw[:, g * LANES:(g + 1) * LANES].astype(BF16)
    ones = jnp.ones((ONES_ROWS, zt.shape[1]), BF16)
    for hd in range(DA_HEADS):
        vta_ref[0, hd * DA_VT_ROWS:hd * DA_VT_ROWS + DA_V_DIM] = zt[hd * DA_V_DIM:(hd + 1) * DA_V_DIM].astype(BF16)
        vta_ref[0, hd * DA_VT_ROWS + DA_V_DIM:(hd + 1) * DA_VT_ROWS] = ones
    for blk4 in range(2 * NSA_G):
        src = DA_WIDTH + (blk4 % 2) * NSA_KV + (blk4 // 2) * NSA_D
        vtb_ref[0, blk4 * NSA_VT_ROWS:blk4 * NSA_VT_ROWS + NSA_D] = zt[src:src + NSA_D].astype(BF16)
        vtb_ref[0, blk4 * NSA_VT_ROWS + NSA_D:(blk4 + 1) * NSA_VT_ROWS] = ones
    gate_ref[0] = jax.nn.sigmoid(zt[DA_WIDTH + 2 * NSA_KV:COL_ROWS])


def _proj(x2, g, w, wt, r, blk, gq, gk, gnq, gks, gkw, b, s):
    n, d = x2.shape
    tm = TILE
    nt = s // tm
    row = lambda wd: pl.BlockSpec((tm, wd), lambda i: (i, 0))
    col = lambda rows: pl.BlockSpec((1, rows, tm), lambda i: (i, 0, 0))
    return pl.pallas_call(
        _proj_kernel,
        grid=(n // tm,),
        in_specs=[row(d), _const_spec((1, d)), _const_spec(w.shape), _const_spec(wt.shape),
                  _const_spec(r.shape), pl.BlockSpec((tm, LANES), lambda i: (i % nt, 0)),
                  _const_spec((1, 512)), _const_spec((1, 512)), _const_spec((1, 512)),
                  _const_spec((1, 256)), _const_spec((1, 256))],
        out_specs=[row(512), row(512), row(NSA_HEADS * LANES), row(2 * NSA_G * LANES),
                   pl.BlockSpec((1, 2, NSA_G, tm, LANES), lambda i: (i // nt, 0, 0, i % nt, 0)),
                   col(DA_HEADS * DA_VT_ROWS), col(2 * NSA_G * NSA_VT_ROWS), col(NSA_G * GATE_ROWS)],
        out_shape=[
            jax.ShapeDtypeStruct((n, 512), BF16),
            jax.ShapeDtypeStruct((n, 512), BF16),
            jax.ShapeDtypeStruct((n, NSA_HEADS * LANES), BF16),
            jax.ShapeDtypeStruct((n, 2 * NSA_G * LANES), F32),
            jax.ShapeDtypeStruct((b, 2, NSA_G, s, LANES), BF16),
            jax.ShapeDtypeStruct((n // tm, DA_HEADS * DA_VT_ROWS, tm), BF16),
            jax.ShapeDtypeStruct((n // tm, 2 * NSA_G * NSA_VT_ROWS, tm), BF16),
            jax.ShapeDtypeStruct((n // tm, NSA_G * GATE_ROWS, tm), F32),
        ],
        compiler_params=_cparams(("parallel",)),
        name="norm_in_proj",
    )(x2, g, w, wt, r, blk, gq, gk, gnq, gks, gkw)


def _cmp_kernel(ck_ref, cv_ref, pek_ref, pev_ref, w1k_ref, w1v_ref, w2k_ref, w2vt_ref, gk_ref, kc_ref, vct_ref):
    nc = kc_ref.shape[2]

    def hidden(c_ref, pe_ref, w1_ref):
        lo = hi = None
        for t in range(CMP_STRIDE):
            x = c_ref[0, pl.ds(t, nc, stride=CMP_STRIDE), :]
            a = _dot((x + pe_ref[t:t + 1, :]).astype(BF16), w1_ref[t])
            b = _dot((x + pe_ref[CMP_STRIDE + t:CMP_STRIDE + t + 1, :]).astype(BF16), w1_ref[CMP_STRIDE + t])
            lo = a if lo is None else lo + a
            hi = b if hi is None else hi + b
        return jax.nn.gelu(lo + pltpu.roll(hi, nc - 1, axis=0)).astype(BF16)

    kc = _dot(hidden(ck_ref, pek_ref, w1k_ref), w2k_ref[...])
    ms = jnp.sum(kc * kc, axis=-1, keepdims=True) * (1.0 / NSA_D)
    kc_ref[0, 0] = (kc * lax.rsqrt(ms + EPS) * gk_ref[...]).astype(BF16)
    vct_ref[0, 0] = _dot_nt(w2vt_ref[...], hidden(cv_ref, pev_ref, w1v_ref)).astype(BF16)


def _compress(kcvc, pek, pev, w1k, w1v, w2k, w2vt, gk):
    b, s, _ = kcvc.shape
    g = NSA_G
    nc = s // CMP_STRIDE
    return pl.pallas_call(
        _cmp_kernel,
        grid=(b, g),
        in_specs=[pl.BlockSpec((1, s, LANES), lambda i, j: (i, 0, j)),
                  pl.BlockSpec((1, s, LANES), lambda i, j: (i, 0, NSA_G + j)),
                  _const_spec(pek.shape), _const_spec(pev.shape),
                  _const_spec(w1k.shape), _const_spec(w1v.shape),
                  _const_spec(w2k.shape), _const_spec(w2vt.shape), _const_spec(gk.shape)],
        out_specs=[pl.BlockSpec((1, 1, nc, LANES), lambda i, j: (i, j, 0, 0)),
                   pl.BlockSpec((1, 1, NSA_D, nc), lambda i, j: (i, j, 0, 0))],
        out_shape=[jax.ShapeDtypeStruct((b, g, nc, LANES), BF16),
                   jax.ShapeDtypeStruct((b, g, NSA_D, nc), BF16)],
        compiler_params=_cparams(("parallel", "parallel")),
        name="block_compress",
    )(kcvc, kcvc, pek, pev, w1k, w1v, w2k, w2vt, gk)


def _flash_list(n, qk_fn, vt_fn, bias_fn, state_fn, s_ref, p_ref, alpha_ref, before_loop=None):
    width = s_ref.shape[2]

    def pv(i, slot):
        acc_ref = state_fn(i)[1]
        acc_ref[...] = alpha_ref[slot] * acc_ref[...] + _dot(vt_fn(i), p_ref[slot])

    def softmax(i, slot):
        m_ref = state_fn(i)[0]
        for c in range(width // CW):
            cs = slice(c * CW, (c + 1) * CW)
            s = s_ref[slot, :, cs] + bias_fn(i, cs)
            m_old = m_ref[:, cs]
            m_new = jnp.maximum(m_old, jnp.max(s, axis=0, keepdims=True))
            p_ref[slot, :, cs] = jnp.exp2(s - m_new).astype(BF16)
            alpha_ref[slot, :, cs] = jnp.exp2(m_old - m_new)
            m_ref[:, cs] = m_new

    s_ref[0] = qk_fn(0)
    softmax(0, 0)
    s_ref[1] = qk_fn(1)
    if before_loop is not None:
        before_loop()

    def body(g, carry):
        first = 1 + FLASH_UNROLL * g
        for u in range(FLASH_UNROLL):
            slot = (u + 1) % 2
            pv(first + u - 1, 1 - slot)
            softmax(first + u, slot)
            s_ref[1 - slot] = qk_fn(first + u + 1)
        return carry

    groups = (n - 1 + FLASH_UNROLL - 1) // FLASH_UNROLL
    lax.fori_loop(0, groups, body, 0)
    pv(FLASH_UNROLL * groups, 0)


def _to_rows(xt):
    t = xt.shape[1]
    return jnp.concatenate([xt[:, c * LANES:(c + 1) * LANES].T for c in range(t // LANES)], axis=0)


def _da_kernel(lam_init, nd, nqt, qt_of, kt_of, q_ref, k_ref, vt_ref, tab_ref, lamv_ref, gs_ref, o_ref,
               qq_ref, s_ref, p_ref, alpha_ref, m_ref, acc_ref):
    t = TILE
    lane = lax.broadcasted_iota(jnp.int32, (t, LANES), 1)
    for qt in range(nqt):
        q = q_ref[0, qt * t:(qt + 1) * t, :].astype(F32)
        qq_ref[qt, 0:t, :] = jnp.where(lane < DA_HEAD_DIM, q, 0.0).astype(BF16)
        qq_ref[qt, t:2 * t, :] = jnp.where(lane >= DA_HEAD_DIM, q, 0.0).astype(BF16)
    m_ref[...] = jnp.full(m_ref.shape, NEG, F32)
    acc_ref[...] = jnp.zeros(acc_ref.shape, F32)

    key_tile = lambda i: jnp.minimum(kt_of[i], nqt - 1)

    def qk(i):
        return _dot_nt(k_ref[0, pl.ds(pl.multiple_of(key_tile(i) * t, t), t), :], qq_ref[qt_of[i]])

    def bias(i, cs):
        d = jnp.clip(qt_of[i] - kt_of[i], -1, nd)
        return tab_ref[0, d + 1, :, slice(cs.start % t, cs.start % t + CW)]

    _flash_list(nqt * (nqt + 1) // 2, qk, lambda i: vt_ref[0, key_tile(i)], bias,
                lambda i: (m_ref.at[qt_of[i]], acc_ref.at[qt_of[i]]), s_ref, p_ref, alpha_ref)

    lv = lamv_ref[...]
    lam = (jnp.exp(jnp.sum(lv[0:1] * lv[1:2], axis=-1, keepdims=True))
           - jnp.exp(jnp.sum(lv[2:3] * lv[3:4], axis=-1, keepdims=True)) + lam_init)

    def finish(qt, carry):
        ot = acc_ref[qt, 0:DA_V_DIM, :] / acc_ref[qt, DA_V_DIM:DA_V_DIM + 1, :]
        y = _to_rows(ot[:, 0:t] - lam * ot[:, t:2 * t])
        o_ref[0, pl.ds(pl.multiple_of(qt * t, t), t), :] = (_rms(y, gs_ref[...]) * (1.0 - lam_init)).astype(BF16)
        return carry

    lax.fori_loop(0, nqt, finish, 0)


def _diff_attention(dq, dk, vt, tab, lamv, gs, lam_init):
    b, s, _ = dq.shape
    t = TILE
    nqt = s // t
    nd = tab.shape[1] - 2
    pairs = [(qt, kt) for qt in range(nqt) for kt in range(qt + 1)] + [(nqt - 1, nqt)] * (FLASH_UNROLL + 2)
    qt_of = jnp.asarray(np.array([p[0] for p in pairs], np.int32))
    kt_of = jnp.asarray(np.array([p[1] for p in pairs], np.int32))
    seq = pl.BlockSpec((1, s, LANES), lambda i, h, *_: (i, 0, h))
    return pl.pallas_call(
        functools.partial(_da_kernel, lam_init, nd, nqt),
        grid_spec=pltpu.PrefetchScalarGridSpec(
            num_scalar_prefetch=2,
            grid=(b, DA_HEADS),
            in_specs=[
                seq, seq,
                pl.BlockSpec((1, nqt, DA_VT_ROWS, t), lambda i, h, *_: (i, 0, h, 0)),
                pl.BlockSpec((1,) + tab.shape[1:], lambda i, h, *_: (h, 0, 0, 0)),
                pl.BlockSpec(lamv.shape, lambda i, h, *_: (0, 0)),
                pl.BlockSpec(gs.shape, lambda i, h, *_: (0, 0)),
            ],
            out_specs=seq,
            scratch_shapes=[
                pltpu.VMEM((nqt, 2 * t, LANES), BF16),
                pltpu.VMEM((2, t, 2 * t), F32),
                pltpu.VMEM((2, t, 2 * t), BF16),
                pltpu.VMEM((2, 1, 2 * t), F32),
                pltpu.VMEM((nqt, 1, 2 * t), F32),
                pltpu.VMEM((nqt, DA_VT_ROWS, 2 * t), F32),
            ]),
        out_shape=jax.ShapeDtypeStruct((b, s, DA_WIDTH), BF16),
        compiler_params=_cparams(("parallel", "parallel")),
        name="diff_attention",
    )(qt_of, kt_of, dq, dk, vt, tab, lamv, gs)


def _nsa_kernel(nd, nqt, q_ref, kaug_ref, vt_ref, kc_ref, vct_ref, longc_ref, tab_ref, gate_ref,
                place_ref, ovlt_ref, o_ref,
                qz_ref, qs_ref, s_ref, p_ref, alpha_ref, m_ref, acc_ref, comb_ref, psum_ref, sc_ref, pc_ref):
    t = TILE
    halves = t // CW
    nchunk = NSA_P * halves
    qi = pl.program_id(2)
    win_tiles = WINDOW // t
    win_edge = nd + 2

    def gate_row(r):
        return jnp.concatenate([gate_ref[0, 3 * p + r:3 * p + r + 1, :] for p in range(NSA_P)], axis=1)

    for p in range(NSA_P):
        qz_ref[p * t:(p + 1) * t, :] = q_ref[0, :, p * LANES:(p + 1) * LANES]

    def select_blocks():
        nc = sc_ref.shape[0]
        cstart = pl.multiple_of((nqt - 1 - qi) * (t // CMP_STRIDE), t // CMP_STRIDE)
        sc_ref[...] = _dot_nt(kc_ref[0, 0], qz_ref[...])
        for c in range(nchunk):
            p, half = divmod(c, halves)
            cs = slice(c * CW, (c + 1) * CW)
            hs = slice(half * CW, (half + 1) * CW)
            s = sc_ref[:, cs] + longc_ref[0, pl.ds(cstart, nc), cs]
            m = jnp.max(s, axis=0, keepdims=True)
            e = jnp.exp2(s - m)
            pc = jnp.where(m > 0.5 * NEG, e / jnp.sum(e, axis=0, keepdims=True), 0.0)
            pc_ref[:, cs] = pc.astype(BF16)
            if p == 0:
                psum_ref[:, hs] = pc
            else:
                psum_ref[:, hs] += pc
        comb_ref[...] = gate_row(0) * _dot(vct_ref[0, 0], pc_ref[...])
        select_topk()

    def select_topk():
        ovlt = ovlt_ref[...]
        imp = None
        for piece in _bf16_pieces(psum_ref[...], 3):
            d = _dot(ovlt, piece)
            imp = d if imp is None else imp + d
        nblk = imp.shape[0]
        jrow = lax.broadcasted_iota(jnp.int32, (nblk, t), 0)
        cur = (qi * t + lax.broadcasted_iota(jnp.int32, (nblk, t), 1)) // SEL_BLOCK
        forced = (jrow == 0) | (jrow == cur) | (jrow == cur - 1)
        score = jnp.where(jrow > cur, -SEL_FORCE, imp + jnp.where(forced, SEL_FORCE, 0.0))
        lowest = jnp.float32(-3e38)
        jrow_f = jrow.astype(F32)
        nsel = jnp.ones((nblk, t), F32)
        for _ in range(SEL_TOPK):
            mx = jnp.max(score, axis=0, keepdims=True)
            first = jnp.min(jnp.where(score == mx, jrow_f, 2.0 * LANES), axis=0, keepdims=True)
            hit = jrow_f == first
            nsel = jnp.where(hit, 0.0, nsel)
            score = jnp.where(hit, lowest, score)
        parts = [jnp.zeros((NSA_D, t), F32), nsel]
        if LANES - NSA_D - nblk:
            parts.append(jnp.zeros((LANES - NSA_D - nblk, t), F32))
        nsel_rows = _to_rows(jnp.concatenate(parts, axis=0))
        for p in range(NSA_P):
            rs = slice(p * t, (p + 1) * t)
            qs_ref[rs, :] = (qz_ref[rs, :].astype(F32) + nsel_rows).astype(BF16)

    m_ref[...] = jnp.full(m_ref.shape, NEG, F32)
    acc_ref[...] = jnp.zeros(acc_ref.shape, F32)
    nwin = win_tiles + 1

    def entry(i):
        is_win = jnp.asarray(i < nwin)
        kt = jnp.where(is_win, qi - win_tiles + i, i - nwin)
        return is_win, kt, qi - kt

    def qk(i):
        is_win, kt, _ = entry(i)
        rows = pl.ds(pl.multiple_of(jnp.clip(kt, 0, qi) * t, t), t)
        q_all = qz_ref if isinstance(i, int) and i < nwin else qs_ref
        return _dot_nt(kaug_ref[0, is_win.astype(jnp.int32), 0, rows, :], q_all[...])

    def vt(i):
        is_win, kt, _ = entry(i)
        rows = pl.ds(pl.multiple_of(is_win.astype(jnp.int32) * NSA_VT_ROWS, NSA_VT_ROWS), NSA_VT_ROWS)
        return vt_ref[0, jnp.clip(kt, 0, qi), rows, :]

    def bias(i, cs):
        is_win, kt, d = entry(i)
        idx = jnp.where(is_win & (d == win_tiles), win_edge, jnp.clip(d, -1, nd) + 1)
        return tab_ref[0, jnp.where(kt < 0, 0, idx), :, cs]

    def state(i):
        st = jnp.asarray(i < nwin).astype(jnp.int32)
        return m_ref.at[st], acc_ref.at[st]

    _flash_list(nwin + qi + 1, qk, vt, bias, state, s_ref, p_ref, alpha_ref, before_loop=select_blocks)

    def branch_out(st):
        return acc_ref[st, 0:NSA_D, :] / acc_ref[st, NSA_D:NSA_D + 1, :]

    comb = comb_ref[...] + gate_row(1) * branch_out(0) + gate_row(2) * branch_out(1)

    pad = jnp.zeros((LANES - NSA_D, t), F32)
    out = None
    for p in range(NSA_P):
        rows = _to_rows(jnp.concatenate([comb[:, p * t:(p + 1) * t], pad], axis=0)).astype(BF16)
        d = _dot(rows, place_ref[p])
        out = d if out is None else out + d
    o_ref[0] = out.astype(BF16)


def _nsa_attention(nq, kaug, vt, kc, vct, longc, tab, gates, place, ovlt):
    b, s, _ = nq.shape
    t = TILE
    nqt = s // t
    nc = kc.shape[2]
    nd = tab.shape[1] - 3
    gw = NSA_P * NSA_D
    return pl.pallas_call(
        functools.partial(_nsa_kernel, nd, nqt),
        grid=(b, NSA_G, nqt),
        in_specs=[
            pl.BlockSpec((1, t, NSA_P * LANES), lambda i, g, j: (i, j, g)),
            pl.BlockSpec((1, 2, 1, s, LANES), lambda i, g, j: (i, 0, g, 0, 0)),
            pl.BlockSpec((1, nqt, 2 * NSA_VT_ROWS, t), lambda i, g, j: (i, 0, g, 0)),
            pl.BlockSpec((1, 1, nc, LANES), lambda i, g, j: (i, g, 0, 0)),
            pl.BlockSpec((1, 1, NSA_D, nc), lambda i, g, j: (i, g, 0, 0)),
            pl.BlockSpec((1,) + longc.shape[1:], lambda i, g, j: (g, 0, 0)),
            pl.BlockSpec((1,) + tab.shape[1:], lambda i, g, j: (g, 0, 0, 0)),
            pl.BlockSpec((1, GATE_ROWS, t), lambda i, g, j: (i * nqt + j, g, 0)),
            _const_spec(place.shape), _const_spec(ovlt.shape),
        ],
        out_specs=pl.BlockSpec((1, t, gw), lambda i, g, j: (i, j, g)),
        out_shape=jax.ShapeDtypeStruct((b, s, NSA_WIDTH), BF16),
        scratch_shapes=[
            pltpu.VMEM((NSA_P * t, LANES), BF16),
            pltpu.VMEM((NSA_P * t, LANES), BF16),
            pltpu.VMEM((2, t, NSA_P * t), F32),
            pltpu.VMEM((2, t, NSA_P * t), BF16),
            pltpu.VMEM((2, 1, NSA_P * t), F32),
            pltpu.VMEM((2, 1, NSA_P * t), F32),
            pltpu.VMEM((2, NSA_VT_ROWS, NSA_P * t), F32),
            pltpu.VMEM((NSA_D, NSA_P * t), F32),
            pltpu.VMEM((nc, t), F32),
            pltpu.VMEM((nc, NSA_P * t), F32),
            pltpu.VMEM((nc, NSA_P * t), BF16),
        ],
        compiler_params=_cparams(("parallel", "parallel", "arbitrary")),
        name="nsa_attention",
    )(nq, kaug, vt, kc, vct, longc, tab, gates, place, ovlt)


def _merge_kernel(x_ref, ya_ref, yb_ref, g_ref, wmg_ref, pa_ref, pb_ref, wo_ref, o_ref):
    x = x_ref[...]
    h = _rms(x, g_ref[...]).astype(BF16)
    mg = jax.nn.sigmoid(_dot(h, wmg_ref[...]))
    merged = mg[:, 0:D_MODEL] * _dot(ya_ref[...], pa_ref[...]) + mg[:, D_MODEL:] * _dot(yb_ref[...], pb_ref[...])
    o_ref[...] = x + _dot(merged.astype(BF16), wo_ref[...])


def _merge(x2, ya, yb, g, wmg, pa, pb, wo):
    n, d = x2.shape
    tm = TILE
    row = lambda wd: pl.BlockSpec((tm, wd), lambda i: (i, 0))
    return pl.pallas_call(
        _merge_kernel,
        grid=(n // tm,),
        in_specs=[row(d), row(DA_WIDTH), row(NSA_WIDTH), _const_spec((1, d)),
                  _const_spec(wmg.shape), _const_spec(pa.shape), _const_spec(pb.shape), _const_spec(wo.shape)],
        out_specs=row(d),
        out_shape=jax.ShapeDtypeStruct((n, d), F32),
        compiler_params=_cparams(("parallel",)),
        name="gated_merge_out_proj",
    )(x2, ya, yb, g, wmg, pa, pb, wo)


def _rel_bucket(dist):
    n = jnp.maximum(dist, 0)
    max_exact = N_BUCKETS // 2
    nf = jnp.maximum(n, 1).astype(jnp.float32)
    large = max_exact + (jnp.log(nf / max_exact) / math.log(MAX_DISTANCE / max_exact)
                         * (N_BUCKETS - max_exact)).astype(jnp.int32)
    large = jnp.minimum(large, N_BUCKETS - 1)
    return jnp.where(n < max_exact, n, large)


def _bias_of(table, dist):
    bucket = _rel_bucket(dist)
    col = lambda v: v.reshape((-1,) + (1,) * dist.ndim)
    out = jnp.zeros((table.shape[1],) + dist.shape, F32)
    for bkt in range(N_BUCKETS):
        out = jnp.where(bucket == bkt, col(table[bkt].astype(F32)), out)
    return jnp.where(dist >= 0, out, NEG)


def _far_tiles(t):
    return -(-(MAX_DISTANCE + t - 1) // t)


def _tile_bias(table, t, window_edge):
    nd = _far_tiles(t)
    j = jnp.arange(t)[:, None]
    i = jnp.arange(t)[None, :]
    dist = jnp.arange(-1, nd + 1)[:, None, None] * t + i - j
    bias = _bias_of(table, dist)
    if window_edge:
        we = WINDOW // t + 1
        edge = jnp.where(dist[we] < WINDOW, bias[:, we], NEG)
        bias = jnp.concatenate([bias, edge[:, None]], axis=1)
    return bias


def _cmp_bias(table, t, nqt):
    cpt = t // CMP_STRIDE
    r = jnp.arange((2 * nqt - 1) * cpt)[:, None]
    dist = (nqt - 1 - r // cpt) * t + jnp.arange(t)[None, :] - ((r % cpt) * CMP_STRIDE + CMP_BLOCK - 1)
    return _bias_of(table, dist)


def _lanes_by_head(x):
    x = x.reshape((NSA_G, NSA_P) + x.shape[1:])
    x = jnp.moveaxis(x, 1, -2)
    return x.reshape(x.shape[:-2] + (NSA_P * x.shape[-1],))


def _static_tables(s):
    nc = s // CMP_STRIDE
    n_cmp = nc - CMP_BLOCK // CMP_STRIDE + 1
    nblk = s // SEL_BLOCK
    place = np.zeros((NSA_P, LANES, NSA_P * NSA_D), np.float32)
    for p in range(NSA_P):
        for dd in range(NSA_D):
            place[p, dd, p * NSA_D + dd] = 1.0
    ovlt = np.zeros((nblk, nc), np.float32)
    for c in range(n_cmp):
        for jb in range(nblk):
            if c * CMP_STRIDE < (jb + 1) * SEL_BLOCK and c * CMP_STRIDE + CMP_BLOCK - 1 >= jb * SEL_BLOCK:
                ovlt[jb, c] = 1.0
    blk = np.zeros((s, LANES), np.float32)
    blk[np.arange(s), NSA_D + np.arange(s) // SEL_BLOCK] = NEG
    r = np.kron(np.eye(512 // 64, dtype=np.float32), np.ones((64, 64), np.float32))
    return jnp.asarray(place, BF16), jnp.asarray(ovlt, BF16), jnp.asarray(blk, BF16), jnp.asarray(r, BF16)


def _pad_blocks(w, axis=-1):
    w = jnp.moveaxis(w, axis, -1)
    w = w.reshape(w.shape[:-1] + (w.shape[-1] // NSA_D, NSA_D))
    w = jnp.pad(w, [(0, 0)] * (w.ndim - 1) + [(0, LANES - NSA_D)])
    return jnp.moveaxis(w.reshape(w.shape[:-2] + (-1,)), -1, axis)


def kernel(x, w_in, w_branch_a, w_branch_b, w_out, norm_ffn1, norm_mix, norm_ffn2, ffn1_w1, ffn1_w3, ffn1_w2, ffn2_w1, ffn2_w3, ffn2_w2, da_q_gain, da_k_gain, da_lambda_q1, da_lambda_k1, da_lambda_q2, da_lambda_k2, da_subln_gain, nsa_q_gain, nsa_k_gain, cmp_pe_k, cmp_w1_k, cmp_w2_k, cmp_pe_v, cmp_w1_v, cmp_w2_v, rel_bias_table):
    b, s, d = x.shape
    depth = w_in.shape[0]
    t = TILE
    assert d == D_MODEL and s % t == 0 and WINDOW % t == 0 and s >= WINDOW
    assert SEL_TOPK <= s // SEL_BLOCK <= LANES - NSA_D
    n = b * s
    nqt = s // t

    place, ovlt, blk, r512 = _static_tables(s)
    tab_a = _tile_bias(rel_bias_table[:, :DA_HEADS], t, False) * LOG2E
    tab_b = _lanes_by_head(_tile_bias(rel_bias_table[:, DA_HEADS:], t, True)) * LOG2E
    longc = _lanes_by_head(_cmp_bias(rel_bias_table[:, DA_HEADS:], t, nqt)) * LOG2E

    bf = lambda w: w.astype(BF16)
    f1w1, f1w3, f1w2, f2w1, f2w3, f2w2 = map(bf, (ffn1_w1, ffn1_w3, ffn1_w2, ffn2_w1, ffn2_w3, ffn2_w2))
    wba, wbb, wo = bf(w_branch_a), bf(w_branch_b), bf(w_out)
    w_row = bf(jnp.concatenate([w_in[:, :, 0:1024], _pad_blocks(w_in[:, :, 1536:2048]), _pad_blocks(w_in[:, :, 2048:2304]),
                                _pad_blocks(w_in[:, :, 2304:2432]), _pad_blocks(w_in[:, :, 2560:2688])], axis=2))
    w_ng = jnp.pad(w_in[:, :, 2816:2840].reshape(depth, d, NSA_G, NSA_P * 3),
                   ((0, 0), (0, 0), (0, 0), (0, GATE_ROWS - NSA_P * 3))).reshape(depth, d, NSA_G * GATE_ROWS)
    w_col = bf(jnp.swapaxes(jnp.concatenate([w_in[:, :, 1024:1536], w_in[:, :, 2432:2560], w_in[:, :, 2688:2816], w_ng],
                                            axis=2), 1, 2))
    w_mg = bf(w_in[:, :, 2840:])
    cw1 = lambda w: bf(_pad_blocks(w.reshape(depth, CMP_BLOCK, NSA_D, CMP_HIDDEN), axis=2))
    cw1k, cw1v = cw1(cmp_w1_k), cw1(cmp_w1_v)
    pek, pev = _pad_blocks(cmp_pe_k.astype(F32)), _pad_blocks(cmp_pe_v.astype(F32))
    cw2k = bf(_pad_blocks(cmp_w2_k))
    cw2vt = bf(jnp.swapaxes(cmp_w2_v, 1, 2))

    x2 = x.reshape(n, d)
    row = lambda v: v.reshape(1, -1).astype(F32)
    tile = lambda v, k: jnp.tile(v.astype(F32), k).reshape(1, -1)
    for l in range(depth):
        x2 = _ffn(x2, row(norm_ffn1[l]), f1w1[l], f1w3[l], f1w2[l])

        dq, dk, nq, kcvc, kaug, vta, vtb, gates = _proj(
            x2, row(norm_mix[l]), w_row[l], w_col[l], r512, blk,
            tile(da_q_gain[l], 8), tile(da_k_gain[l], 8), tile(nsa_q_gain[l], 8),
            tile(nsa_k_gain[l, 1], 4), tile(nsa_k_gain[l, 2], 4), b, s)
        vta = vta.reshape(b, nqt, DA_HEADS * DA_VT_ROWS, t)
        vtb = vtb.reshape(b, nqt, 2 * NSA_G * NSA_VT_ROWS, t)

        lam_init = 0.8 - 0.6 * math.exp(-0.3 * l)
        lamv = jnp.stack([da_lambda_q1[l], da_lambda_k1[l], da_lambda_q2[l], da_lambda_k2[l]]).astype(F32)
        ya = _diff_attention(dq.reshape(b, s, 512), dk.reshape(b, s, 512), vta,
                             tab_a, lamv, row(da_subln_gain[l]), lam_init)

        kc, vct = _compress(kcvc.reshape(b, s, 2 * NSA_G * LANES), pek[l], pev[l], cw1k[l], cw1v[l], cw2k[l], cw2vt[l],
                            _pad_blocks(row(nsa_k_gain[l, 0])))
        yb = _nsa_attention(nq.reshape(b, s, NSA_HEADS * LANES), kaug, vtb, kc, vct, longc, tab_b, gates, place, ovlt)

        x2 = _merge(x2, ya.reshape(n, DA_WIDTH), yb.reshape(n, NSA_WIDTH), row(norm_mix[l]),
                    w_mg[l], wba[l], wbb[l], wo[l])
        x2 = _ffn(x2, row(norm_ffn2[l]), f2w1[l], f2w3[l], f2w2[l])
    return x2.reshape(b, s, d)
```

```python
import functools
import math

import numpy as np
import jax
import jax.numpy as jnp
from jax import lax
from jax.experimental import pallas as pl
from jax.experimental.pallas import tpu as pltpu

F32 = jnp.float32
BF16 = jnp.bfloat16

D_MODEL = 1024
DA_HEADS = 4
DA_HEAD_DIM = 64
DA_V_DIM = 2 * DA_HEAD_DIM
DA_WIDTH = DA_HEADS * DA_V_DIM
NSA_HEADS = 8
NSA_G = 2
NSA_P = NSA_HEADS // NSA_G
NSA_D = 64
NSA_WIDTH = NSA_HEADS * NSA_D
NSA_KV = NSA_G * NSA_D
CMP_BLOCK = 32
CMP_STRIDE = 16
CMP_HIDDEN = 128
SEL_BLOCK = 64
SEL_TOPK = 8
SEL_FORCE = 1e4
WINDOW = 512
N_BUCKETS = 32
MAX_DISTANCE = 1024
D_FF = 2816
EPS = 1e-6
NEG = -1e30

LANES = 128
VMEM_LIMIT = 52 * 1024 * 1024

TILE = 256
CW = LANES
FLASH_UNROLL = 4
GATE_ROWS = 16
ONES_ROWS = 16
DA_VT_ROWS = DA_V_DIM + ONES_ROWS
NSA_VT_ROWS = NSA_D + ONES_ROWS
LOG2E = math.log2(math.e)

COL_ROWS = DA_WIDTH + 2 * NSA_KV + NSA_G * GATE_ROWS


def _cparams(sem):
    return pltpu.CompilerParams(dimension_semantics=sem, vmem_limit_bytes=VMEM_LIMIT)


def _const_spec(shape):
    nd = len(shape)
    return pl.BlockSpec(shape, lambda *_: (0,) * nd)


def _rms(xf, g):
    ms = jnp.mean(xf * xf, axis=-1, keepdims=True)
    return xf * lax.rsqrt(ms + EPS) * g


def _dot(a, b):
    return jnp.dot(a, b, preferred_element_type=F32)


def _dot_nt(a, b):
    return lax.dot_general(a, b, (((1,), (1,)), ((), ())), preferred_element_type=F32)


def _bf16_pieces(x, parts):
    out = []
    r = x
    for i in range(parts):
        piece = r.astype(BF16)
        out.append(piece)
        if i + 1 < parts:
            r = r - piece.astype(F32)
    return out


def _ffn_kernel(x_ref, g_ref, w1_ref, w3_ref, w2_ref, o_ref):
    x = x_ref[...]
    h = _rms(x, g_ref[...]).astype(BF16)
    a = _dot(h, w1_ref[...])
    b = _dot(h, w3_ref[...])
    t = (jax.nn.silu(a) * b).astype(BF16)
    o_ref[...] = x + 0.5 * _dot(t, w2_ref[...])


def _ffn(x2, g, w1, w3, w2):
    n, d = x2.shape
    tm = TILE
    return pl.pallas_call(
        _ffn_kernel,
        grid=(n // tm,),
        in_specs=[
            pl.BlockSpec((tm, d), lambda i: (i, 0)),
            _const_spec((1, d)),
            _const_spec(w1.shape),
            _const_spec(w3.shape),
            _const_spec(w2.shape),
        ],
        out_specs=pl.BlockSpec((tm, d), lambda i: (i, 0)),
        out_shape=jax.ShapeDtypeStruct((n, d), F32),
        compiler_params=_cparams(("parallel",)),
        name="ffn_half_step",
    )(x2, g, w1, w3, w2)


def _group_rms(z, r_ref, gain):
    w = z.shape[-1]
    ss = _dot((z * z).astype(BF16), r_ref[0:w, 0:w])
    return z * lax.rsqrt(ss * (1.0 / 64.0) + EPS) * gain


def _proj_kernel(x_ref, g_ref, w_ref, wt_ref, r_ref, blk_ref, gq_ref, gk_ref, gnq_ref, gks_ref, gkw_ref,
                 dq_ref, dk_ref, nq_ref, kcvc_ref, kaug_ref, vta_ref, vtb_ref, gate_ref):
    h = _rms(x_ref[...], g_ref[...]).astype(BF16)
    z = _dot(h, w_ref[...])
    zt = _dot_nt(wt_ref[...], h)
    scale = DA_HEAD_DIM ** -0.5 * LOG2E
    dq_ref[...] = (_group_rms(z[:, 0:512], r_ref, gq_ref[...]) * scale).astype(BF16)
    dk_ref[...] = _group_rms(z[:, 512:1024], r_ref, gk_ref[...]).astype(BF16)
    for c0 in (1024, 1536):
        nq_ref[:, c0 - 1024:c0 - 512] = (_group_rms(z[:, c0:c0 + 512], r_ref, gnq_ref[...]) * scale).astype(BF16)
    kcvc_ref[...] = z[:, 2048:2560]
    ks = _group_rms(z[:, 2560:2816], r_ref, gks_ref[...])
    kw = _group_rms(z[:, 2816:3072], r_ref, gkw_ref[...])
    blk = blk_ref[...].astype(F32)
    for g in range(NSA_G):
        kaug_ref[0, 0, g] = (ks[:, g * LANES:(g + 1) * LANES] + blk).astype(BF16)
        kaug_ref[0, 1, g] = kw[:, g * LANES:(g + 1) * LANES].astype(BF16)
    ones = jnp.ones((ONES_ROWS, zt.shape[1]), BF16)
    for hd in range(DA_HEADS):
        vta_ref[0, hd * DA_VT_ROWS:hd * DA_VT_ROWS + DA_V_DIM] = zt[hd * DA_V_DIM:(hd + 1) * DA_V_DIM].astype(BF16)
        vta_ref[0, hd * DA_VT_ROWS + DA_V_DIM:(hd + 1) * DA_VT_ROWS] = ones
    for blk4 in range(2 * NSA_G):
        src = DA_WIDTH + (blk4 % 2) * NSA_KV + (blk4 // 2) * NSA_D
        vtb_ref[0, blk4 * NSA_VT_ROWS:blk4 * NSA_VT_ROWS + NSA_D] = zt[src:src + NSA_D].astype(BF16)
        vtb_ref[0, blk4 * NSA_VT_ROWS + NSA_D:(blk4 + 1) * NSA_VT_ROWS] = ones
    gate_ref[0] = jax.nn.sigmoid(zt[DA_WIDTH + 2 * NSA_KV:COL_ROWS])


def _proj(x2, g, w, wt, r, blk, gq, gk, gnq, gks, gkw, b, s):
    n, d = x2.shape
    tm = TILE
    nt = s // tm
    row = lambda wd: pl.BlockSpec((tm, wd), lambda i: (i, 0))
    col = lambda rows: pl.BlockSpec((1, rows, tm), lambda i: (i, 0, 0))
    return pl.pallas_call(
        _proj_kernel,
        grid=(n // tm,),
        in_specs=[row(d), _const_spec((1, d)), _const_spec(w.shape), _const_spec(wt.shape),
                  _const_spec(r.shape), pl.BlockSpec((tm, LANES), lambda i: (i % nt, 0)),
                  _const_spec((1, 512)), _const_spec((1, 512)), _const_spec((1, 512)),
                  _const_spec((1, 256)), _const_spec((1, 256))],
        out_specs=[row(512), row(512), row(NSA_HEADS * LANES), row(2 * NSA_G * LANES),
                   pl.BlockSpec((1, 2, NSA_G, tm, LANES), lambda i: (i // nt, 0, 0, i % nt, 0)),
                   col(DA_HEADS * DA_VT_ROWS), col(2 * NSA_G * NSA_VT_ROWS), col(NSA_G * GATE_ROWS)],
        out_shape=[
            jax.ShapeDtypeStruct((n, 512), BF16),
            jax.ShapeDtypeStruct((n, 512), BF16),
            jax.ShapeDtypeStruct((n, NSA_HEADS * LANES), BF16),
            jax.ShapeDtypeStruct((n, 2 * NSA_G * LANES), F32),
            jax.ShapeDtypeStruct((b, 2, NSA_G, s, LANES), BF16),
            jax.ShapeDtypeStruct((n // tm, DA_HEADS * DA_VT_ROWS, tm), BF16),
            jax.ShapeDtypeStruct((n // tm, 2 * NSA_G * NSA_VT_ROWS, tm), BF16),
            jax.ShapeDtypeStruct((n // tm, NSA_G * GATE_ROWS, tm), F32),
        ],
        compiler_params=_cparams(("parallel",)),
        name="norm_in_proj",
    )(x2, g, w, wt, r, blk, gq, gk, gnq, gks, gkw)


def _cmp_kernel(ck_ref, cv_ref, pek_ref, pev_ref, w1k_ref, w1v_ref, w2k_ref, w2vt_ref, gk_ref, kc_ref, vct_ref):
    nc = kc_ref.shape[2]

    def hidden(c_ref, pe_ref, w1_ref):
        lo = hi = None
        for t in range(CMP_STRIDE):
            x = c_ref[0, pl.ds(t, nc, stride=CMP_STRIDE), :]
            a = _dot((x + pe_ref[t:t + 1, :]).astype(BF16), w1_ref[t])
            b = _dot((x + pe_ref[CMP_STRIDE + t:CMP_STRIDE + t + 1, :]).astype(BF16), w1_ref[CMP_STRIDE + t])
            lo = a if lo is None else lo + a
            hi = b if hi is None else hi + b
        return jax.nn.gelu(lo + pltpu.roll(hi, nc - 1, axis=0)).astype(BF16)

    kc = _dot(hidden(ck_ref, pek_ref, w1k_ref), w2k_ref[...])
    ms = jnp.sum(kc * kc, axis=-1, keepdims=True) * (1.0 / NSA_D)
    kc_ref[0, 0] = (kc * lax.rsqrt(ms + EPS) * gk_ref[...]).astype(BF16)
    vct_ref[0, 0] = _dot_nt(w2vt_ref[...], hidden(cv_ref, pev_ref, w1v_ref)).astype(BF16)


def _compress(kcvc, pek, pev, w1k, w1v, w2k, w2vt, gk):
    b, s, _ = kcvc.shape
    g = NSA_G
    nc = s // CMP_STRIDE
    return pl.pallas_call(
        _cmp_kernel,
        grid=(b, g),
        in_specs=[pl.BlockSpec((1, s, LANES), lambda i, j: (i, 0, j)),
                  pl.BlockSpec((1, s, LANES), lambda i, j: (i, 0, NSA_G + j)),
                  _const_spec(pek.shape), _const_spec(pev.shape),
                  _const_spec(w1k.shape), _const_spec(w1v.shape),
                  _const_spec(w2k.shape), _const_spec(w2vt.shape), _const_spec(gk.shape)],
        out_specs=[pl.BlockSpec((1, 1, nc, LANES), lambda i, j: (i, j, 0, 0)),
                   pl.BlockSpec((1, 1, NSA_D, nc), lambda i, j: (i, j, 0, 0))],
        out_shape=[jax.ShapeDtypeStruct((b, g, nc, LANES), BF16),
                   jax.ShapeDtypeStruct((b, g, NSA_D, nc), BF16)],
        compiler_params=_cparams(("parallel", "parallel")),
        name="block_compress",
    )(kcvc, kcvc, pek, pev, w1k, w1v, w2k, w2vt, gk)


def _flash_list(n, qk_fn, vt_fn, bias_fn, state_fn, s_ref, mx_ref, p_ref, alpha_ref, before_loop=None):
    width = s_ref.shape[2]
    chunks = [slice(c * CW, (c + 1) * CW) for c in range(width // CW)]

    def scores(i, slot):
        raw = qk_fn(i)
        for cs in chunks:
            sb = raw[:, cs] + bias_fn(i, cs)
            s_ref[slot, :, cs] = sb
            mx_ref[slot, :, cs] = jnp.max(sb, axis=0, keepdims=True)

    def pv(i, slot):
        acc_ref = state_fn(i)[1]
        acc_ref[...] = alpha_ref[slot] * acc_ref[...] + _dot(vt_fn(i), p_ref[slot])

    def softmax(i, slot):
        m_ref = state_fn(i)[0]
        m_old = m_ref[...]
        m_new = jnp.maximum(m_old, mx_ref[slot])
        alpha_ref[slot] = jnp.exp2(m_old - m_new)
        m_ref[...] = m_new
        for cs in chunks:
            p_ref[slot, :, cs] = jnp.exp2(s_ref[slot, :, cs] - m_new[:, cs]).astype(BF16)

    scores(0, 0)
    softmax(0, 0)
    scores(1, 1)
    if before_loop is not None:
        before_loop()

    def body(g, carry):
        first = 1 + FLASH_UNROLL * g
        for u in range(FLASH_UNROLL):
            slot = (u + 1) % 2
            pv(first + u - 1, 1 - slot)
            softmax(first + u, slot)
            scores(first + u + 1, 1 - slot)
        return carry

    groups = (n - 1 + FLASH_UNROLL - 1) // FLASH_UNROLL
    lax.fori_loop(0, groups, body, 0)
    pv(FLASH_UNROLL * groups, 0)


def _to_rows(xt):
    t = xt.shape[1]
    return jnp.concatenate([xt[:, c * LANES:(c + 1) * LANES].T for c in range(t // LANES)], axis=0)


def _da_kernel(lam_init, nd, nqt, qt_of, kt_of, q_ref, k_ref, vt_ref, tab_ref, lamv_ref, gs_ref, o_ref,
               qq_ref, s_ref, mx_ref, p_ref, alpha_ref, m_ref, acc_ref):
    t = TILE
    lane = lax.broadcasted_iota(jnp.int32, (t, LANES), 1)
    for qt in range(nqt):
        q = q_ref[0, qt * t:(qt + 1) * t, :].astype(F32)
        qq_ref[qt, 0:t, :] = jnp.where(lane < DA_HEAD_DIM, q, 0.0).astype(BF16)
        qq_ref[qt, t:2 * t, :] = jnp.where(lane >= DA_HEAD_DIM, q, 0.0).astype(BF16)
    m_ref[...] = jnp.full(m_ref.shape, NEG, F32)
    acc_ref[...] = jnp.zeros(acc_ref.shape, F32)

    key_tile = lambda i: jnp.minimum(kt_of[i], nqt - 1)

    def qk(i):
        return _dot_nt(k_ref[0, pl.ds(pl.multiple_of(key_tile(i) * t, t), t), :], qq_ref[qt_of[i]])

    def bias(i, cs):
        d = jnp.clip(qt_of[i] - kt_of[i], -1, nd)
        return tab_ref[0, d + 1, :, slice(cs.start % t, cs.start % t + CW)]

    _flash_list(nqt * (nqt + 1) // 2, qk, lambda i: vt_ref[0, key_tile(i)], bias,
                lambda i: (m_ref.at[qt_of[i]], acc_ref.at[qt_of[i]]), s_ref, mx_ref, p_ref, alpha_ref)

    lv = lamv_ref[...]
    lam = (jnp.exp(jnp.sum(lv[0:1] * lv[1:2], axis=-1, keepdims=True))
           - jnp.exp(jnp.sum(lv[2:3] * lv[3:4], axis=-1, keepdims=True)) + lam_init)

    def finish(qt, carry):
        ot = acc_ref[qt, 0:DA_V_DIM, :] / acc_ref[qt, DA_V_DIM:DA_V_DIM + 1, :]
        y = _to_rows(ot[:, 0:t] - lam * ot[:, t:2 * t])
        o_ref[0, pl.ds(pl.multiple_of(qt * t, t), t), :] = (_rms(y, gs_ref[...]) * (1.0 - lam_init)).astype(BF16)
        return carry

    lax.fori_loop(0, nqt, finish, 0)


def _diff_attention(dq, dk, vt, tab, lamv, gs, lam_init):
    b, s, _ = dq.shape
    t = TILE
    nqt = s // t
    nd = tab.shape[1] - 2
    pairs = [(qt, kt) for qt in range(nqt) for kt in range(qt + 1)] + [(nqt - 1, nqt)] * (FLASH_UNROLL + 2)
    qt_of = jnp.asarray(np.array([p[0] for p in pairs], np.int32))
    kt_of = jnp.asarray(np.array([p[1] for p in pairs], np.int32))
    seq = pl.BlockSpec((1, s, LANES), lambda i, h, *_: (i, 0, h))
    return pl.pallas_call(
        functools.partial(_da_kernel, lam_init, nd, nqt),
        grid_spec=pltpu.PrefetchScalarGridSpec(
            num_scalar_prefetch=2,
            grid=(b, DA_HEADS),
            in_specs=[
                seq, seq,
                pl.BlockSpec((1, nqt, DA_VT_ROWS, t), lambda i, h, *_: (i, 0, h, 0)),
                pl.BlockSpec((1,) + tab.shape[1:], lambda i, h, *_: (h, 0, 0, 0)),
                pl.BlockSpec(lamv.shape, lambda i, h, *_: (0, 0)),
                pl.BlockSpec(gs.shape, lambda i, h, *_: (0, 0)),
            ],
            out_specs=seq,
            scratch_shapes=[
                pltpu.VMEM((nqt, 2 * t, LANES), BF16),
                pltpu.VMEM((2, t, 2 * t), F32),
                pltpu.VMEM((2, 1, 2 * t), F32),
                pltpu.VMEM((2, t, 2 * t), BF16),
                pltpu.VMEM((2, 1, 2 * t), F32),
                pltpu.VMEM((nqt, 1, 2 * t), F32),
                pltpu.VMEM((nqt, DA_VT_ROWS, 2 * t), F32),
            ]),
        out_shape=jax.ShapeDtypeStruct((b, s, DA_WIDTH), BF16),
        compiler_params=_cparams(("parallel", "parallel")),
        name="diff_attention",
    )(qt_of, kt_of, dq, dk, vt, tab, lamv, gs)


def _nsa_kernel(nd, nqt, q_ref, kaug_ref, vt_ref, kc_ref, vct_ref, longc_ref, tab_ref, gate_ref,
                place_ref, ovlt_ref, o_ref,
                qz_ref, qs_ref, s_ref, mx_ref, p_ref, alpha_ref, m_ref, acc_ref, comb_ref, psum_ref, sc_ref, pc_ref):
    t = TILE
    halves = t // CW
    nchunk = NSA_P * halves
    qi = pl.program_id(2)
    win_tiles = WINDOW // t
    win_edge = nd + 2

    def gate_row(r):
        return jnp.concatenate([gate_ref[0, 3 * p + r:3 * p + r + 1, :] for p in range(NSA_P)], axis=1)

    for p in range(NSA_P):
        qz_ref[p * t:(p + 1) * t, :] = q_ref[0, :, p * LANES:(p + 1) * LANES]

    def select_blocks():
        nc = sc_ref.shape[0]
        cstart = pl.multiple_of((nqt - 1 - qi) * (t // CMP_STRIDE), t // CMP_STRIDE)
        sc_ref[...] = _dot_nt(kc_ref[0, 0], qz_ref[...])
        for c in range(nchunk):
            p, half = divmod(c, halves)
            cs = slice(c * CW, (c + 1) * CW)
            hs = slice(half * CW, (half + 1) * CW)
            s = sc_ref[:, cs] + longc_ref[0, pl.ds(cstart, nc), cs]
            m = jnp.max(s, axis=0, keepdims=True)
            e = jnp.exp2(s - m)
            pc = jnp.where(m > 0.5 * NEG, e / jnp.sum(e, axis=0, keepdims=True), 0.0)
            pc_ref[:, cs] = pc.astype(BF16)
            if p == 0:
                psum_ref[:, hs] = pc
            else:
                psum_ref[:, hs] += pc
        comb_ref[...] = gate_row(0) * _dot(vct_ref[0, 0], pc_ref[...])
        select_topk()

    def select_topk():
        ovlt = ovlt_ref[...]
        imp = None
        for piece in _bf16_pieces(psum_ref[...], 3):
            d = _dot(ovlt, piece)
            imp = d if imp is None else imp + d
        nblk = imp.shape[0]
        jrow = lax.broadcasted_iota(jnp.int32, (nblk, t), 0)
        cur = (qi * t + lax.broadcasted_iota(jnp.int32, (nblk, t), 1)) // SEL_BLOCK
        forced = (jrow == 0) | (jrow == cur) | (jrow == cur - 1)
        score = jnp.where(jrow > cur, -SEL_FORCE, imp + jnp.where(forced, SEL_FORCE, 0.0))
        lowest = jnp.float32(-3e38)
        jrow_f = jrow.astype(F32)
        nsel = jnp.ones((nblk, t), F32)
        for _ in range(SEL_TOPK):
            mx = jnp.max(score, axis=0, keepdims=True)
            first = jnp.min(jnp.where(score == mx, jrow_f, 2.0 * LANES), axis=0, keepdims=True)
            hit = jrow_f == first
            nsel = jnp.where(hit, 0.0, nsel)
            score = jnp.where(hit, lowest, score)
        parts = [jnp.zeros((NSA_D, t), F32), nsel]
        if LANES - NSA_D - nblk:
            parts.append(jnp.zeros((LANES - NSA_D - nblk, t), F32))
        nsel_rows = _to_rows(jnp.concatenate(parts, axis=0))
        for p in range(NSA_P):
            rs = slice(p * t, (p + 1) * t)
            qs_ref[rs, :] = (qz_ref[rs, :].astype(F32) + nsel_rows).astype(BF16)

    m_ref[...] = jnp.full(m_ref.shape, NEG, F32)
    acc_ref[...] = jnp.zeros(acc_ref.shape, F32)
    nwin = win_tiles + 1

    def entry(i):
        is_win = jnp.asarray(i < nwin)
        kt = jnp.where(is_win, qi - win_tiles + i, i - nwin)
        return is_win, kt, qi - kt

    def qk(i):
        is_win, kt, _ = entry(i)
        rows = pl.ds(pl.multiple_of(jnp.clip(kt, 0, qi) * t, t), t)
        q_all = qz_ref if isinstance(i, int) and i < nwin else qs_ref
        return _dot_nt(kaug_ref[0, is_win.astype(jnp.int32), 0, rows, :], q_all[...])

    def vt(i):
        is_win, kt, _ = entry(i)
        rows = pl.ds(pl.multiple_of(is_win.astype(jnp.int32) * NSA_VT_ROWS, NSA_VT_ROWS), NSA_VT_ROWS)
        return vt_ref[0, jnp.clip(kt, 0, qi), rows, :]

    def bias(i, cs):
        is_win, kt, d = entry(i)
        idx = jnp.where(is_win & (d == win_tiles), win_edge, jnp.clip(d, -1, nd) + 1)
        return tab_ref[0, jnp.where(kt < 0, 0, idx), :, cs]

    def state(i):
        st = jnp.asarray(i < nwin).astype(jnp.int32)
        return m_ref.at[st], acc_ref.at[st]

    _flash_list(nwin + qi + 1, qk, vt, bias, state, s_ref, mx_ref, p_ref, alpha_ref, before_loop=select_blocks)

    def branch_out(st):
        return acc_ref[st, 0:NSA_D, :] / acc_ref[st, NSA_D:NSA_D + 1, :]

    comb = comb_ref[...] + gate_row(1) * branch_out(0) + gate_row(2) * branch_out(1)

    pad = jnp.zeros((LANES - NSA_D, t), F32)
    out = None
    for p in range(NSA_P):
        rows = _to_rows(jnp.concatenate([comb[:, p * t:(p + 1) * t], pad], axis=0)).astype(BF16)
        d = _dot(rows, place_ref[p])
        out = d if out is None else out + d
    o_ref[0] = out.astype(BF16)


def _nsa_attention(nq, kaug, vt, kc, vct, longc, tab, gates, place, ovlt):
    b, s, _ = nq.shape
    t = TILE
    nqt = s // t
    nc = kc.shape[2]
    nd = tab.shape[1] - 3
    gw = NSA_P * NSA_D
    return pl.pallas_call(
        functools.partial(_nsa_kernel, nd, nqt),
        grid=(b, NSA_G, nqt),
        in_specs=[
            pl.BlockSpec((1, t, NSA_P * LANES), lambda i, g, j: (i, j, g)),
            pl.BlockSpec((1, 2, 1, s, LANES), lambda i, g, j: (i, 0, g, 0, 0)),
            pl.BlockSpec((1, nqt, 2 * NSA_VT_ROWS, t), lambda i, g, j: (i, 0, g, 0)),
            pl.BlockSpec((1, 1, nc, LANES), lambda i, g, j: (i, g, 0, 0)),
            pl.BlockSpec((1, 1, NSA_D, nc), lambda i, g, j: (i, g, 0, 0)),
            pl.BlockSpec((1,) + longc.shape[1:], lambda i, g, j: (g, 0, 0)),
            pl.BlockSpec((1,) + tab.shape[1:], lambda i, g, j: (g, 0, 0, 0)),
            pl.BlockSpec((1, GATE_ROWS, t), lambda i, g, j: (i * nqt + j, g, 0)),
            _const_spec(place.shape), _const_spec(ovlt.shape),
        ],
        out_specs=pl.BlockSpec((1, t, gw), lambda i, g, j: (i, j, g)),
        out_shape=jax.ShapeDtypeStruct((b, s, NSA_WIDTH), BF16),
        scratch_shapes=[
            pltpu.VMEM((NSA_P * t, LANES), BF16),
            pltpu.VMEM((NSA_P * t, LANES), BF16),
            pltpu.VMEM((2, t, NSA_P * t), F32),
            pltpu.VMEM((2, 1, NSA_P * t), F32),
            pltpu.VMEM((2, t, NSA_P * t), BF16),
            pltpu.VMEM((2, 1, NSA_P * t), F32),
            pltpu.VMEM((2, 1, NSA_P * t), F32),
            pltpu.VMEM((2, NSA_VT_ROWS, NSA_P * t), F32),
            pltpu.VMEM((NSA_D, NSA_P * t), F32),
            pltpu.VMEM((nc, t), F32),
            pltpu.VMEM((nc, NSA_P * t), F32),
            pltpu.VMEM((nc, NSA_P * t), BF16),
        ],
        compiler_params=_cparams(("parallel", "parallel", "arbitrary")),
        name="nsa_attention",
    )(nq, kaug, vt, kc, vct, longc, tab, gates, place, ovlt)


def _merge_kernel(x_ref, ya_ref, yb_ref, g_ref, wmg_ref, pa_ref, pb_ref, wo_ref, o_ref):
    x = x_ref[...]
    h = _rms(x, g_ref[...]).astype(BF16)
    mg = jax.nn.sigmoid(_dot(h, wmg_ref[...]))
    merged = mg[:, 0:D_MODEL] * _dot(ya_ref[...], pa_ref[...]) + mg[:, D_MODEL:] * _dot(yb_ref[...], pb_ref[...])
    o_ref[...] = x + _dot(merged.astype(BF16), wo_ref[...])


def _merge(x2, ya, yb, g, wmg, pa, pb, wo):
    n, d = x2.shape
    tm = TILE
    row = lambda wd: pl.BlockSpec((tm, wd), lambda i: (i, 0))
    return pl.pallas_call(
        _merge_kernel,
        grid=(n // tm,),
        in_specs=[row(d), row(DA_WIDTH), row(NSA_WIDTH), _const_spec((1, d)),
                  _const_spec(wmg.shape), _const_spec(pa.shape), _const_spec(pb.shape), _const_spec(wo.shape)],
        out_specs=row(d),
        out_shape=jax.ShapeDtypeStruct((n, d), F32),
        compiler_params=_cparams(("parallel",)),
        name="gated_merge_out_proj",
    )(x2, ya, yb, g, wmg, pa, pb, wo)


def _rel_bucket(dist):
    n = jnp.maximum(dist, 0)
    max_exact = N_BUCKETS // 2
    nf = jnp.maximum(n, 1).astype(jnp.float32)
    large = max_exact + (jnp.log(nf / max_exact) / math.log(MAX_DISTANCE / max_exact)
                         * (N_BUCKETS - max_exact)).astype(jnp.int32)
    large = jnp.minimum(large, N_BUCKETS - 1)
    return jnp.where(n < max_exact, n, large)


def _bias_of(table, dist):
    bucket = _rel_bucket(dist)
    col = lambda v: v.reshape((-1,) + (1,) * dist.ndim)
    out = jnp.zeros((table.shape[1],) + dist.shape, F32)
    for bkt in range(N_BUCKETS):
        out = jnp.where(bucket == bkt, col(table[bkt].astype(F32)), out)
    return jnp.where(dist >= 0, out, NEG)


def _far_tiles(t):
    return -(-(MAX_DISTANCE + t - 1) // t)


def _tile_bias(table, t, window_edge):
    nd = _far_tiles(t)
    d0 = 2 * t - 1
    width = (nd + 3) * t - 1
    v = _bias_of(table, jnp.arange(width) - d0)
    w = jnp.concatenate([v, jnp.zeros((v.shape[0], 1), F32)], axis=1)
    m = jnp.tile(w, (1, t))[:, :t * width].reshape(-1, t, width)
    bias = jnp.stack([m[:, :, d * t + d0:d * t + d0 + t] for d in range(-1, nd + 1)], axis=1)
    if window_edge:
        we = WINDOW // t
        j = jnp.arange(t)[:, None]
        i = jnp.arange(t)[None, :]
        edge = jnp.where(we * t + i - j < WINDOW, bias[:, we + 1], NEG)
        bias = jnp.concatenate([bias, edge[:, None]], axis=1)
    return bias


def _cmp_bias(table, t, nqt):
    cpt = t // CMP_STRIDE
    r = jnp.arange((2 * nqt - 1) * cpt)[:, None]
    dist = (nqt - 1 - r // cpt) * t + jnp.arange(t)[None, :] - ((r % cpt) * CMP_STRIDE + CMP_BLOCK - 1)
    return _bias_of(table, dist)


def _lanes_by_head(x):
    x = x.reshape((NSA_G, NSA_P) + x.shape[1:])
    x = jnp.moveaxis(x, 1, -2)
    return x.reshape(x.shape[:-2] + (NSA_P * x.shape[-1],))


def _static_tables(s):
    nc = s // CMP_STRIDE
    n_cmp = nc - CMP_BLOCK // CMP_STRIDE + 1
    nblk = s // SEL_BLOCK
    place = np.zeros((NSA_P, LANES, NSA_P * NSA_D), np.float32)
    for p in range(NSA_P):
        for dd in range(NSA_D):
            place[p, dd, p * NSA_D + dd] = 1.0
    ovlt = np.zeros((nblk, nc), np.float32)
    for c in range(n_cmp):
        for jb in range(nblk):
            if c * CMP_STRIDE < (jb + 1) * SEL_BLOCK and c * CMP_STRIDE + CMP_BLOCK - 1 >= jb * SEL_BLOCK:
                ovlt[jb, c] = 1.0
    blk = np.zeros((s, LANES), np.float32)
    blk[np.arange(s), NSA_D + np.arange(s) // SEL_BLOCK] = NEG
    r = np.kron(np.eye(512 // 64, dtype=np.float32), np.ones((64, 64), np.float32))
    return jnp.asarray(place, BF16), jnp.asarray(ovlt, BF16), jnp.asarray(blk, BF16), jnp.asarray(r, BF16)


def _pad_blocks(w, axis=-1):
    w = jnp.moveaxis(w, axis, -1)
    w = w.reshape(w.shape[:-1] + (w.shape[-1] // NSA_D, NSA_D))
    w = jnp.pad(w, [(0, 0)] * (w.ndim - 1) + [(0, LANES - NSA_D)])
    return jnp.moveaxis(w.reshape(w.shape[:-2] + (-1,)), -1, axis)


def kernel(x, w_in, w_branch_a, w_branch_b, w_out, norm_ffn1, norm_mix, norm_ffn2, ffn1_w1, ffn1_w3, ffn1_w2, ffn2_w1, ffn2_w3, ffn2_w2, da_q_gain, da_k_gain, da_lambda_q1, da_lambda_k1, da_lambda_q2, da_lambda_k2, da_subln_gain, nsa_q_gain, nsa_k_gain, cmp_pe_k, cmp_w1_k, cmp_w2_k, cmp_pe_v, cmp_w1_v, cmp_w2_v, rel_bias_table):
    b, s, d = x.shape
    depth = w_in.shape[0]
    t = TILE
    assert d == D_MODEL and s % t == 0 and WINDOW % t == 0 and s >= WINDOW
    assert SEL_TOPK <= s // SEL_BLOCK <= LANES - NSA_D
    n = b * s
    nqt = s // t

    place, ovlt, blk, r512 = _static_tables(s)
    tab_a = _tile_bias(rel_bias_table[:, :DA_HEADS], t, False) * LOG2E
    tab_b = _lanes_by_head(_tile_bias(rel_bias_table[:, DA_HEADS:], t, True)) * LOG2E
    longc = _lanes_by_head(_cmp_bias(rel_bias_table[:, DA_HEADS:], t, nqt)) * LOG2E

    bf = lambda w: w.astype(BF16)
    f1w1, f1w3, f1w2, f2w1, f2w3, f2w2 = map(bf, (ffn1_w1, ffn1_w3, ffn1_w2, ffn2_w1, ffn2_w3, ffn2_w2))
    wba, wbb, wo = bf(w_branch_a), bf(w_branch_b), bf(w_out)
    w_row = bf(jnp.concatenate([w_in[:, :, 0:1024], _pad_blocks(w_in[:, :, 1536:2048]), _pad_blocks(w_in[:, :, 2048:2304]),
                                _pad_blocks(w_in[:, :, 2304:2432]), _pad_blocks(w_in[:, :, 2560:2688])], axis=2))
    w_ng = jnp.pad(w_in[:, :, 2816:2840].reshape(depth, d, NSA_G, NSA_P * 3),
                   ((0, 0), (0, 0), (0, 0), (0, GATE_ROWS - NSA_P * 3))).reshape(depth, d, NSA_G * GATE_ROWS)
    w_col = bf(jnp.swapaxes(jnp.concatenate([w_in[:, :, 1024:1536], w_in[:, :, 2432:2560], w_in[:, :, 2688:2816], w_ng],
                                            axis=2), 1, 2))
    w_mg = bf(w_in[:, :, 2840:])
    cw1 = lambda w: bf(_pad_blocks(w.reshape(depth, CMP_BLOCK, NSA_D, CMP_HIDDEN), axis=2))
    cw1k, cw1v = cw1(cmp_w1_k), cw1(cmp_w1_v)
    pek, pev = _pad_blocks(cmp_pe_k.astype(F32)), _pad_blocks(cmp_pe_v.astype(F32))
    cw2k = bf(_pad_blocks(cmp_w2_k))
    cw2vt = bf(jnp.swapaxes(cmp_w2_v, 1, 2))

    x2 = x.reshape(n, d)
    row = lambda v: v.reshape(1, -1).astype(F32)
    tile = lambda v, k: jnp.tile(v.astype(F32), k).reshape(1, -1)
    for l in range(depth):
        x2 = _ffn(x2, row(norm_ffn1[l]), f1w1[l], f1w3[l], f1w2[l])

        dq, dk, nq, kcvc, kaug, vta, vtb, gates = _proj(
            x2, row(norm_mix[l]), w_row[l], w_col[l], r512, blk,
            tile(da_q_gain[l], 8), tile(da_k_gain[l], 8), tile(nsa_q_gain[l], 8),
            tile(nsa_k_gain[l, 1], 4), tile(nsa_k_gain[l, 2], 4), b, s)
        vta = vta.reshape(b, nqt, DA_HEADS * DA_VT_ROWS, t)
        vtb = vtb.reshape(b, nqt, 2 * NSA_G * NSA_VT_ROWS, t)

        lam_init = 0.8 - 0.6 * math.exp(-0.3 * l)
        lamv = jnp.stack([da_lambda_q1[l], da_lambda_k1[l], da_lambda_q2[l], da_lambda_k2[l]]).astype(F32)
        ya = _diff_attention(dq.reshape(b, s, 512), dk.reshape(b, s, 512), vta,
                             tab_a, lamv, row(da_subln_gain[l]), lam_init)

        kc, vct = _compress(kcvc.reshape(b, s, 2 * NSA_G * LANES), pek[l], pev[l], cw1k[l], cw1v[l], cw2k[l], cw2vt[l],
                            _pad_blocks(row(nsa_k_gain[l, 0])))
        yb = _nsa_attention(nq.reshape(b, s, NSA_HEADS * LANES), kaug, vtb, kc, vct, longc, tab_b, gates, place, ovlt)

        x2 = _merge(x2, ya.reshape(n, DA_WIDTH), yb.reshape(n, NSA_WIDTH), row(norm_mix[l]),
                    w_mg[l], wba[l], wbb[l], wo[l])
        x2 = _ffn(x2, row(norm_ffn2[l]), f2w1[l], f2w3[l], f2w2[l])
    return x2.reshape(b, s, d)
```

```python
import functools
import math

import numpy as np
import jax
import jax.numpy as jnp
from jax import lax
from jax.experimental import pallas as pl
from jax.experimental.pallas import tpu as pltpu

F32 = jnp.float32
BF16 = jnp.bfloat16

D_MODEL = 1024
DA_HEADS = 4
DA_HEAD_DIM = 64
DA_V_DIM = 2 * DA_HEAD_DIM
DA_WIDTH = DA_HEADS * DA_V_DIM
NSA_HEADS = 8
NSA_G = 2
NSA_P = NSA_HEADS // NSA_G
NSA_D = 64
NSA_WIDTH = NSA_HEADS * NSA_D
NSA_KV = NSA_G * NSA_D
CMP_BLOCK = 32
CMP_STRIDE = 16
CMP_HIDDEN = 128
SEL_BLOCK = 64
SEL_TOPK = 8
SEL_FORCE = 1e4
WINDOW = 512
N_BUCKETS = 32
MAX_DISTANCE = 1024
D_FF = 2816
EPS = 1e-6
NEG = -1e30

LANES = 128
VMEM_LIMIT = 52 * 1024 * 1024

TILE = 256
CW = LANES
FLASH_UNROLL = 4
GATE_ROWS = 16
ONES_ROWS = 16
DA_VT_ROWS = DA_V_DIM + ONES_ROWS
NSA_VT_ROWS = NSA_D + ONES_ROWS
LOG2E = math.log2(math.e)

COL_ROWS = DA_WIDTH + 2 * NSA_KV + NSA_G * GATE_ROWS


def _cparams(sem):
    return pltpu.CompilerParams(dimension_semantics=sem, vmem_limit_bytes=VMEM_LIMIT)


def _const_spec(shape):
    nd = len(shape)
    return pl.BlockSpec(shape, lambda *_: (0,) * nd)


def _rms(xf, g):
    ms = jnp.mean(xf * xf, axis=-1, keepdims=True)
    return xf * lax.rsqrt(ms + EPS) * g


def _dot(a, b):
    return jnp.dot(a, b, preferred_element_type=F32)


def _dot_nt(a, b):
    return lax.dot_general(a, b, (((1,), (1,)), ((), ())), preferred_element_type=F32)


def _bf16_pieces(x, parts):
    out = []
    r = x
    for i in range(parts):
        piece = r.astype(BF16)
        out.append(piece)
        if i + 1 < parts:
            r = r - piece.astype(F32)
    return out


def _ffn_kernel(x_ref, g_ref, w1_ref, w3_ref, w2_ref, o_ref):
    x = x_ref[...]
    h = _rms(x, g_ref[...]).astype(BF16)
    a = _dot(h, w1_ref[...])
    b = _dot(h, w3_ref[...])
    t = (jax.nn.silu(a) * b).astype(BF16)
    o_ref[...] = x + 0.5 * _dot(t, w2_ref[...])


def _ffn(x2, g, w1, w3, w2):
    n, d = x2.shape
    tm = TILE
    return pl.pallas_call(
        _ffn_kernel,
        grid=(n // tm,),
        in_specs=[
            pl.BlockSpec((tm, d), lambda i: (i, 0)),
            _const_spec((1, d)),
            _const_spec(w1.shape),
            _const_spec(w3.shape),
            _const_spec(w2.shape),
        ],
        out_specs=pl.BlockSpec((tm, d), lambda i: (i, 0)),
        out_shape=jax.ShapeDtypeStruct((n, d), F32),
        compiler_params=_cparams(("parallel",)),
        name="ffn_half_step",
    )(x2, g, w1, w3, w2)


def _group_rms(z, r_ref, gain):
    w = z.shape[-1]
    ss = _dot((z * z).astype(BF16), r_ref[0:w, 0:w])
    return z * lax.rsqrt(ss * (1.0 / 64.0) + EPS) * gain


def _proj_kernel(x_ref, g_ref, w_ref, wt_ref, r_ref, blk_ref, gq_ref, gk_ref, gnq_ref, gks_ref, gkw_ref,
                 dq_ref, dk_ref, nq_ref, kcvc_ref, kaug_ref, vta_ref, vtb_ref, gate_ref):
    h = _rms(x_ref[...], g_ref[...]).astype(BF16)
    z = _dot(h, w_ref[...])
    zt = _dot_nt(wt_ref[...], h)
    scale = DA_HEAD_DIM ** -0.5 * LOG2E
    dq_ref[...] = (_group_rms(z[:, 0:512], r_ref, gq_ref[...]) * scale).astype(BF16)
    dk_ref[...] = _group_rms(z[:, 512:1024], r_ref, gk_ref[...]).astype(BF16)
    for c0 in (1024, 1536):
        nq_ref[:, c0 - 1024:c0 - 512] = (_group_rms(z[:, c0:c0 + 512], r_ref, gnq_ref[...]) * scale).astype(BF16)
    kcvc_ref[...] = z[:, 2048:2560]
    ks = _group_rms(z[:, 2560:2816], r_ref, gks_ref[...])
    kw = _group_rms(z[:, 2816:3072], r_ref, gkw_ref[...])
    blk = blk_ref[...].astype(F32)
    for g in range(NSA_G):
        kaug_ref[0, 0, g] = (ks[:, g * LANES:(g + 1) * LANES] + blk).astype(BF16)
        kaug_ref[0, 1, g] = kw[:, g * LANES:(g + 1) * LANES].astype(BF16)
    ones = jnp.ones((ONES_ROWS, zt.shape[1]), BF16)
    for hd in range(DA_HEADS):
        vta_ref[0, hd * DA_VT_ROWS:hd * DA_VT_ROWS + DA_V_DIM] = zt[hd * DA_V_DIM:(hd + 1) * DA_V_DIM].astype(BF16)
        vta_ref[0, hd * DA_VT_ROWS + DA_V_DIM:(hd + 1) * DA_VT_ROWS] = ones
    for blk4 in range(2 * NSA_G):
        src = DA_WIDTH + (blk4 % 2) * NSA_KV + (blk4 // 2) * NSA_D
        vtb_ref[0, blk4 * NSA_VT_ROWS:blk4 * NSA_VT_ROWS + NSA_D] = zt[src:src + NSA_D].astype(BF16)
        vtb_ref[0, blk4 * NSA_VT_ROWS + NSA_D:(blk4 + 1) * NSA_VT_ROWS] = ones
    gate_ref[0] = jax.nn.sigmoid(zt[DA_WIDTH + 2 * NSA_KV:COL_ROWS])


def _proj(x2, g, w, wt, r, blk, gq, gk, gnq, gks, gkw, b, s):
    n, d = x2.shape
    tm = TILE
    nt = s // tm
    row = lambda wd: pl.BlockSpec((tm, wd), lambda i: (i, 0))
    col = lambda rows: pl.BlockSpec((1, rows, tm), lambda i: (i, 0, 0))
    return pl.pallas_call(
        _proj_kernel,
        grid=(n // tm,),
        in_specs=[row(d), _const_spec((1, d)), _const_spec(w.shape), _const_spec(wt.shape),
                  _const_spec(r.shape), pl.BlockSpec((tm, LANES), lambda i: (i % nt, 0)),
                  _const_spec((1, 512)), _const_spec((1, 512)), _const_spec((1, 512)),
                  _const_spec((1, 256)), _const_spec((1, 256))],
        out_specs=[row(512), row(512), row(NSA_HEADS * LANES), row(2 * NSA_G * LANES),
                   pl.BlockSpec((1, 2, NSA_G, tm, LANES), lambda i: (i // nt, 0, 0, i % nt, 0)),
                   col(DA_HEADS * DA_VT_ROWS), col(2 * NSA_G * NSA_VT_ROWS), col(NSA_G * GATE_ROWS)],
        out_shape=[
            jax.ShapeDtypeStruct((n, 512), BF16),
            jax.ShapeDtypeStruct((n, 512), BF16),
            jax.ShapeDtypeStruct((n, NSA_HEADS * LANES), BF16),
            jax.ShapeDtypeStruct((n, 2 * NSA_G * LANES), F32),
            jax.ShapeDtypeStruct((b, 2, NSA_G, s, LANES), BF16),
            jax.ShapeDtypeStruct((n // tm, DA_HEADS * DA_VT_ROWS, tm), BF16),
            jax.ShapeDtypeStruct((n // tm, 2 * NSA_G * NSA_VT_ROWS, tm), BF16),
            jax.ShapeDtypeStruct((n // tm, NSA_G * GATE_ROWS, tm), F32),
        ],
        compiler_params=_cparams(("parallel",)),
        name="norm_in_proj",
    )(x2, g, w, wt, r, blk, gq, gk, gnq, gks, gkw)


def _cmp_kernel(ck_ref, cv_ref, pek_ref, pev_ref, w1k_ref, w1v_ref, w2k_ref, w2vt_ref, gk_ref, kc_ref, vct_ref):
    nc = kc_ref.shape[2]

    def hidden(c_ref, pe_ref, w1_ref):
        lo = hi = None
        for t in range(CMP_STRIDE):
            x = c_ref[0, pl.ds(t, nc, stride=CMP_STRIDE), :]
            a = _dot((x + pe_ref[t:t + 1, :]).astype(BF16), w1_ref[t])
            b = _dot((x + pe_ref[CMP_STRIDE + t:CMP_STRIDE + t + 1, :]).astype(BF16), w1_ref[CMP_STRIDE + t])
            lo = a if lo is None else lo + a
            hi = b if hi is None else hi + b
        return jax.nn.gelu(lo + pltpu.roll(hi, nc - 1, axis=0)).astype(BF16)

    kc = _dot(hidden(ck_ref, pek_ref, w1k_ref), w2k_ref[...])
    ms = jnp.sum(kc * kc, axis=-1, keepdims=True) * (1.0 / NSA_D)
    kc_ref[0, 0] = (kc * lax.rsqrt(ms + EPS) * gk_ref[...]).astype(BF16)
    vct_ref[0, 0] = _dot_nt(w2vt_ref[...], hidden(cv_ref, pev_ref, w1v_ref)).astype(BF16)


def _compress(kcvc, pek, pev, w1k, w1v, w2k, w2vt, gk):
    b, s, _ = kcvc.shape
    g = NSA_G
    nc = s // CMP_STRIDE
    return pl.pallas_call(
        _cmp_kernel,
        grid=(b, g),
        in_specs=[pl.BlockSpec((1, s, LANES), lambda i, j: (i, 0, j)),
                  pl.BlockSpec((1, s, LANES), lambda i, j: (i, 0, NSA_G + j)),
                  _const_spec(pek.shape), _const_spec(pev.shape),
                  _const_spec(w1k.shape), _const_spec(w1v.shape),
                  _const_spec(w2k.shape), _const_spec(w2vt.shape), _const_spec(gk.shape)],
        out_specs=[pl.BlockSpec((1, 1, nc, LANES), lambda i, j: (i, j, 0, 0)),
                   pl.BlockSpec((1, 1, NSA_D, nc), lambda i, j: (i, j, 0, 0))],
        out_shape=[jax.ShapeDtypeStruct((b, g, nc, LANES), BF16),
                   jax.ShapeDtypeStruct((b, g, NSA_D, nc), BF16)],
        compiler_params=_cparams(("parallel", "parallel")),
        name="block_compress",
    )(kcvc, kcvc, pek, pev, w1k, w1v, w2k, w2vt, gk)


def _flash_list(n, qk_fn, vt_fn, bias_fn, state_fn, s_ref, mx_ref, p_ref, alpha_ref, bias_with_scores,
                before_loop=None):
    width = s_ref.shape[2]
    chunks = [slice(c * CW, (c + 1) * CW) for c in range(width // CW)]

    def scores(i, slot):
        raw = qk_fn(i)
        if not bias_with_scores:
            s_ref[slot] = raw
            return
        for cs in chunks:
            sb = raw[:, cs] + bias_fn(i, cs)
            s_ref[slot, :, cs] = sb
            mx_ref[slot, :, cs] = jnp.max(sb, axis=0, keepdims=True)

    def pv(i, slot):
        acc_ref = state_fn(i)[1]
        acc_ref[...] = alpha_ref[slot] * acc_ref[...] + _dot(vt_fn(i), p_ref[slot])

    def softmax(i, slot):
        m_ref = state_fn(i)[0]
        if bias_with_scores:
            m_old = m_ref[...]
            m_new = jnp.maximum(m_old, mx_ref[slot])
            alpha_ref[slot] = jnp.exp2(m_old - m_new)
            m_ref[...] = m_new
            for cs in chunks:
                p_ref[slot, :, cs] = jnp.exp2(s_ref[slot, :, cs] - m_new[:, cs]).astype(BF16)
            return
        for cs in chunks:
            s = s_ref[slot, :, cs] + bias_fn(i, cs)
            m_old = m_ref[:, cs]
            m_new = jnp.maximum(m_old, jnp.max(s, axis=0, keepdims=True))
            p_ref[slot, :, cs] = jnp.exp2(s - m_new).astype(BF16)
            alpha_ref[slot, :, cs] = jnp.exp2(m_old - m_new)
            m_ref[:, cs] = m_new

    scores(0, 0)
    softmax(0, 0)
    scores(1, 1)
    if before_loop is not None:
        before_loop()

    def body(g, carry):
        first = 1 + FLASH_UNROLL * g
        for u in range(FLASH_UNROLL):
            slot = (u + 1) % 2
            pv(first + u - 1, 1 - slot)
            softmax(first + u, slot)
            scores(first + u + 1, 1 - slot)
        return carry

    groups = (n - 1 + FLASH_UNROLL - 1) // FLASH_UNROLL
    lax.fori_loop(0, groups, body, 0)
    pv(FLASH_UNROLL * groups, 0)


def _to_rows(xt):
    t = xt.shape[1]
    return jnp.concatenate([xt[:, c * LANES:(c + 1) * LANES].T for c in range(t // LANES)], axis=0)


def _da_kernel(lam_init, nd, nqt, qt_of, kt_of, q_ref, k_ref, vt_ref, tab_ref, lamv_ref, gs_ref, o_ref,
               qq_ref, s_ref, mx_ref, p_ref, alpha_ref, m_ref, acc_ref):
    t = TILE
    lane = lax.broadcasted_iota(jnp.int32, (t, LANES), 1)
    for qt in range(nqt):
        q = q_ref[0, qt * t:(qt + 1) * t, :].astype(F32)
        qq_ref[qt, 0:t, :] = jnp.where(lane < DA_HEAD_DIM, q, 0.0).astype(BF16)
        qq_ref[qt, t:2 * t, :] = jnp.where(lane >= DA_HEAD_DIM, q, 0.0).astype(BF16)
    m_ref[...] = jnp.full(m_ref.shape, NEG, F32)
    acc_ref[...] = jnp.zeros(acc_ref.shape, F32)

    key_tile = lambda i: jnp.minimum(kt_of[i], nqt - 1)

    def qk(i):
        return _dot_nt(k_ref[0, pl.ds(pl.multiple_of(key_tile(i) * t, t), t), :], qq_ref[qt_of[i]])

    def bias(i, cs):
        d = jnp.clip(qt_of[i] - kt_of[i], -1, nd)
        return tab_ref[0, d + 1, :, slice(cs.start % t, cs.start % t + CW)]

    _flash_list(nqt * (nqt + 1) // 2, qk, lambda i: vt_ref[0, key_tile(i)], bias,
                lambda i: (m_ref.at[qt_of[i]], acc_ref.at[qt_of[i]]), s_ref, mx_ref, p_ref, alpha_ref, True)

    lv = lamv_ref[...]
    lam = (jnp.exp(jnp.sum(lv[0:1] * lv[1:2], axis=-1, keepdims=True))
           - jnp.exp(jnp.sum(lv[2:3] * lv[3:4], axis=-1, keepdims=True)) + lam_init)

    for qt in range(nqt):
        ot = acc_ref[qt, 0:DA_V_DIM, :] / acc_ref[qt, DA_V_DIM:DA_V_DIM + 1, :]
        y = _to_rows(ot[:, 0:t] - lam * ot[:, t:2 * t])
        o_ref[0, qt * t:(qt + 1) * t, :] = (_rms(y, gs_ref[...]) * (1.0 - lam_init)).astype(BF16)


def _diff_attention(dq, dk, vt, tab, lamv, gs, lam_init):
    b, s, _ = dq.shape
    t = TILE
    nqt = s // t
    nd = tab.shape[1] - 2
    pairs = [(qt, kt) for qt in range(nqt) for kt in range(qt + 1)] + [(nqt - 1, nqt)] * (FLASH_UNROLL + 2)
    qt_of = jnp.asarray(np.array([p[0] for p in pairs], np.int32))
    kt_of = jnp.asarray(np.array([p[1] for p in pairs], np.int32))
    seq = pl.BlockSpec((1, s, LANES), lambda i, h, *_: (i, 0, h))
    return pl.pallas_call(
        functools.partial(_da_kernel, lam_init, nd, nqt),
        grid_spec=pltpu.PrefetchScalarGridSpec(
            num_scalar_prefetch=2,
            grid=(b, DA_HEADS),
            in_specs=[
                seq, seq,
                pl.BlockSpec((1, nqt, DA_VT_ROWS, t), lambda i, h, *_: (i, 0, h, 0)),
                pl.BlockSpec((1,) + tab.shape[1:], lambda i, h, *_: (h, 0, 0, 0)),
                pl.BlockSpec(lamv.shape, lambda i, h, *_: (0, 0)),
                pl.BlockSpec(gs.shape, lambda i, h, *_: (0, 0)),
            ],
            out_specs=seq,
            scratch_shapes=[
                pltpu.VMEM((nqt, 2 * t, LANES), BF16),
                pltpu.VMEM((2, t, 2 * t), F32),
                pltpu.VMEM((2, 1, 2 * t), F32),
                pltpu.VMEM((2, t, 2 * t), BF16),
                pltpu.VMEM((2, 1, 2 * t), F32),
                pltpu.VMEM((nqt, 1, 2 * t), F32),
                pltpu.VMEM((nqt, DA_VT_ROWS, 2 * t), F32),
            ]),
        out_shape=jax.ShapeDtypeStruct((b, s, DA_WIDTH), BF16),
        compiler_params=_cparams(("parallel", "parallel")),
        name="diff_attention",
    )(qt_of, kt_of, dq, dk, vt, tab, lamv, gs)


def _nsa_kernel(nd, nqt, q_ref, kaug_ref, vt_ref, kc_ref, vct_ref, longc_ref, tab_ref, gate_ref,
                place_ref, ovlt_ref, o_ref,
                qz_ref, qs_ref, s_ref, p_ref, alpha_ref, m_ref, acc_ref, comb_ref, psum_ref, sc_ref, pc_ref):
    t = TILE
    halves = t // CW
    nchunk = NSA_P * halves
    qi = pl.program_id(2)
    win_tiles = WINDOW // t
    win_edge = nd + 2

    def gate_row(r):
        return jnp.concatenate([gate_ref[0, 3 * p + r:3 * p + r + 1, :] for p in range(NSA_P)], axis=1)

    for p in range(NSA_P):
        qz_ref[p * t:(p + 1) * t, :] = q_ref[0, :, p * LANES:(p + 1) * LANES]

    def select_blocks():
        nc = sc_ref.shape[0]
        cstart = pl.multiple_of((nqt - 1 - qi) * (t // CMP_STRIDE), t // CMP_STRIDE)
        sc_ref[...] = _dot_nt(kc_ref[0, 0], qz_ref[...])
        for c in range(nchunk):
            p, half = divmod(c, halves)
            cs = slice(c * CW, (c + 1) * CW)
            hs = slice(half * CW, (half + 1) * CW)
            s = sc_ref[:, cs] + longc_ref[0, pl.ds(cstart, nc), cs]
            m = jnp.max(s, axis=0, keepdims=True)
            e = jnp.exp2(s - m)
            pc = jnp.where(m > 0.5 * NEG, e / jnp.sum(e, axis=0, keepdims=True), 0.0)
            pc_ref[:, cs] = pc.astype(BF16)
            if p == 0:
                psum_ref[:, hs] = pc
            else:
                psum_ref[:, hs] += pc
        comb_ref[...] = gate_row(0) * _dot(vct_ref[0, 0], pc_ref[...])
        select_topk()

    def select_topk():
        ovlt = ovlt_ref[...]
        imp = None
        for piece in _bf16_pieces(psum_ref[...], 3):
            d = _dot(ovlt, piece)
            imp = d if imp is None else imp + d
        nblk = imp.shape[0]
        jrow = lax.broadcasted_iota(jnp.int32, (nblk, t), 0)
        cur = (qi * t + lax.broadcasted_iota(jnp.int32, (nblk, t), 1)) // SEL_BLOCK
        forced = (jrow == 0) | (jrow == cur) | (jrow == cur - 1)
        score = jnp.where(jrow > cur, -SEL_FORCE, imp + jnp.where(forced, SEL_FORCE, 0.0))
        lowest = jnp.float32(-3e38)
        jrow_f = jrow.astype(F32)
        nsel = jnp.ones((nblk, t), F32)
        for _ in range(SEL_TOPK):
            mx = jnp.max(score, axis=0, keepdims=True)
            first = jnp.min(jnp.where(score == mx, jrow_f, 2.0 * LANES), axis=0, keepdims=True)
            hit = jrow_f == first
            nsel = jnp.where(hit, 0.0, nsel)
            score = jnp.where(hit, lowest, score)
        parts = [jnp.zeros((NSA_D, t), F32), nsel]
        if LANES - NSA_D - nblk:
            parts.append(jnp.zeros((LANES - NSA_D - nblk, t), F32))
        nsel_rows = _to_rows(jnp.concatenate(parts, axis=0))
        for p in range(NSA_P):
            rs = slice(p * t, (p + 1) * t)
            qs_ref[rs, :] = (qz_ref[rs, :].astype(F32) + nsel_rows).astype(BF16)

    m_ref[...] = jnp.full(m_ref.shape, NEG, F32)
    acc_ref[...] = jnp.zeros(acc_ref.shape, F32)
    nwin = win_tiles + 1

    def entry(i):
        is_win = jnp.asarray(i < nwin)
        kt = jnp.where(is_win, qi - win_tiles + i, i - nwin)
        return is_win, kt, qi - kt

    def qk(i):
        is_win, kt, _ = entry(i)
        rows = pl.ds(pl.multiple_of(jnp.clip(kt, 0, qi) * t, t), t)
        q_all = qz_ref if isinstance(i, int) and i < nwin else qs_ref
        return _dot_nt(kaug_ref[0, is_win.astype(jnp.int32), 0, rows, :], q_all[...])

    def vt(i):
        is_win, kt, _ = entry(i)
        rows = pl.ds(pl.multiple_of(is_win.astype(jnp.int32) * NSA_VT_ROWS, NSA_VT_ROWS), NSA_VT_ROWS)
        return vt_ref[0, jnp.clip(kt, 0, qi), rows, :]

    def bias(i, cs):
        is_win, kt, d = entry(i)
        idx = jnp.where(is_win & (d == win_tiles), win_edge, jnp.clip(d, -1, nd) + 1)
        return tab_ref[0, jnp.where(kt < 0, 0, idx), :, cs]

    def state(i):
        st = jnp.asarray(i < nwin).astype(jnp.int32)
        return m_ref.at[st], acc_ref.at[st]

    _flash_list(nwin + qi + 1, qk, vt, bias, state, s_ref, None, p_ref, alpha_ref, False,
                before_loop=select_blocks)

    def branch_out(st):
        return acc_ref[st, 0:NSA_D, :] / acc_ref[st, NSA_D:NSA_D + 1, :]

    comb = comb_ref[...] + gate_row(1) * branch_out(0) + gate_row(2) * branch_out(1)

    pad = jnp.zeros((LANES - NSA_D, t), F32)
    out = None
    for p in range(NSA_P):
        rows = _to_rows(jnp.concatenate([comb[:, p * t:(p + 1) * t], pad], axis=0)).astype(BF16)
        d = _dot(rows, place_ref[p])
        out = d if out is None else out + d
    o_ref[0] = out.astype(BF16)


def _nsa_attention(nq, kaug, vt, kc, vct, longc, tab, gates, place, ovlt):
    b, s, _ = nq.shape
    t = TILE
    nqt = s // t
    nc = kc.shape[2]
    nd = tab.shape[1] - 3
    gw = NSA_P * NSA_D
    return pl.pallas_call(
        functools.partial(_nsa_kernel, nd, nqt),
        grid=(b, NSA_G, nqt),
        in_specs=[
            pl.BlockSpec((1, t, NSA_P * LANES), lambda i, g, j: (i, j, g)),
            pl.BlockSpec((1, 2, 1, s, LANES), lambda i, g, j: (i, 0, g, 0, 0)),
            pl.BlockSpec((1, nqt, 2 * NSA_VT_ROWS, t), lambda i, g, j: (i, 0, g, 0)),
            pl.BlockSpec((1, 1, nc, LANES), lambda i, g, j: (i, g, 0, 0)),
            pl.BlockSpec((1, 1, NSA_D, nc), lambda i, g, j: (i, g, 0, 0)),
            pl.BlockSpec((1,) + longc.shape[1:], lambda i, g, j: (g, 0, 0)),
            pl.BlockSpec((1,) + tab.shape[1:], lambda i, g, j: (g, 0, 0, 0)),
            pl.BlockSpec((1, GATE_ROWS, t), lambda i, g, j: (i * nqt + j, g, 0)),
            _const_spec(place.shape), _const_spec(ovlt.shape),
        ],
        out_specs=pl.BlockSpec((1, t, gw), lambda i, g, j: (i, j, g)),
        out_shape=jax.ShapeDtypeStruct((b, s, NSA_WIDTH), BF16),
        scratch_shapes=[
            pltpu.VMEM((NSA_P * t, LANES), BF16),
            pltpu.VMEM((NSA_P * t, LANES), BF16),
            pltpu.VMEM((2, t, NSA_P * t), F32),
            pltpu.VMEM((2, t, NSA_P * t), BF16),
            pltpu.VMEM((2, 1, NSA_P * t), F32),
            pltpu.VMEM((2, 1, NSA_P * t), F32),
            pltpu.VMEM((2, NSA_VT_ROWS, NSA_P * t), F32),
            pltpu.VMEM((NSA_D, NSA_P * t), F32),
            pltpu.VMEM((nc, t), F32),
            pltpu.VMEM((nc, NSA_P * t), F32),
            pltpu.VMEM((nc, NSA_P * t), BF16),
        ],
        compiler_params=_cparams(("parallel", "parallel", "arbitrary")),
        name="nsa_attention",
    )(nq, kaug, vt, kc, vct, longc, tab, gates, place, ovlt)


def _merge_kernel(x_ref, ya_ref, yb_ref, g_ref, wmg_ref, pa_ref, pb_ref, wo_ref, o_ref):
    x = x_ref[...]
    h = _rms(x, g_ref[...]).astype(BF16)
    mg = jax.nn.sigmoid(_dot(h, wmg_ref[...]))
    merged = mg[:, 0:D_MODEL] * _dot(ya_ref[...], pa_ref[...]) + mg[:, D_MODEL:] * _dot(yb_ref[...], pb_ref[...])
    o_ref[...] = x + _dot(merged.astype(BF16), wo_ref[...])


def _merge(x2, ya, yb, g, wmg, pa, pb, wo):
    n, d = x2.shape
    tm = TILE
    row = lambda wd: pl.BlockSpec((tm, wd), lambda i: (i, 0))
    return pl.pallas_call(
        _merge_kernel,
        grid=(n // tm,),
        in_specs=[row(d), row(DA_WIDTH), row(NSA_WIDTH), _const_spec((1, d)),
                  _const_spec(wmg.shape), _const_spec(pa.shape), _const_spec(pb.shape), _const_spec(wo.shape)],
        out_specs=row(d),
        out_shape=jax.ShapeDtypeStruct((n, d), F32),
        compiler_params=_cparams(("parallel",)),
        name="gated_merge_out_proj",
    )(x2, ya, yb, g, wmg, pa, pb, wo)


def _rel_bucket(dist):
    n = jnp.maximum(dist, 0)
    max_exact = N_BUCKETS // 2
    nf = jnp.maximum(n, 1).astype(jnp.float32)
    large = max_exact + (jnp.log(nf / max_exact) / math.log(MAX_DISTANCE / max_exact)
                         * (N_BUCKETS - max_exact)).astype(jnp.int32)
    large = jnp.minimum(large, N_BUCKETS - 1)
    return jnp.where(n < max_exact, n, large)


def _bias_of(table, dist):
    bucket = _rel_bucket(dist)
    col = lambda v: v.reshape((-1,) + (1,) * dist.ndim)
    out = jnp.zeros((table.shape[1],) + dist.shape, F32)
    for bkt in range(N_BUCKETS):
        out = jnp.where(bucket == bkt, col(table[bkt].astype(F32)), out)
    return jnp.where(dist >= 0, out, NEG)


def _far_tiles(t):
    return -(-(MAX_DISTANCE + t - 1) // t)


def _tile_bias(table, t, window_edge):
    nd = _far_tiles(t)
    d0 = 2 * t - 1
    width = (nd + 3) * t - 1
    v = _bias_of(table, jnp.arange(width) - d0)
    w = jnp.concatenate([v, jnp.zeros((v.shape[0], 1), F32)], axis=1)
    m = jnp.tile(w, (1, t))[:, :t * width].reshape(-1, t, width)
    bias = jnp.stack([m[:, :, d * t + d0:d * t + d0 + t] for d in range(-1, nd + 1)], axis=1)
    if window_edge:
        we = WINDOW // t
        j = jnp.arange(t)[:, None]
        i = jnp.arange(t)[None, :]
        edge = jnp.where(we * t + i - j < WINDOW, bias[:, we + 1], NEG)
        bias = jnp.concatenate([bias, edge[:, None]], axis=1)
    return bias


def _cmp_bias(table, t, nqt):
    cpt = t // CMP_STRIDE
    r = jnp.arange((2 * nqt - 1) * cpt)[:, None]
    dist = (nqt - 1 - r // cpt) * t + jnp.arange(t)[None, :] - ((r % cpt) * CMP_STRIDE + CMP_BLOCK - 1)
    return _bias_of(table, dist)


def _lanes_by_head(x):
    x = x.reshape((NSA_G, NSA_P) + x.shape[1:])
    x = jnp.moveaxis(x, 1, -2)
    return x.reshape(x.shape[:-2] + (NSA_P * x.shape[-1],))


def _static_tables(s):
    nc = s // CMP_STRIDE
    n_cmp = nc - CMP_BLOCK // CMP_STRIDE + 1
    nblk = s // SEL_BLOCK
    place = np.zeros((NSA_P, LANES, NSA_P * NSA_D), np.float32)
    for p in range(NSA_P):
        for dd in range(NSA_D):
            place[p, dd, p * NSA_D + dd] = 1.0
    ovlt = np.zeros((nblk, nc), np.float32)
    for c in range(n_cmp):
        for jb in range(nblk):
            if c * CMP_STRIDE < (jb + 1) * SEL_BLOCK and c * CMP_STRIDE + CMP_BLOCK - 1 >= jb * SEL_BLOCK:
                ovlt[jb, c] = 1.0
    blk = np.zeros((s, LANES), np.float32)
    blk[np.arange(s), NSA_D + np.arange(s) // SEL_BLOCK] = NEG
    r = np.kron(np.eye(512 // 64, dtype=np.float32), np.ones((64, 64), np.float32))
    return jnp.asarray(place, BF16), jnp.asarray(ovlt, BF16), jnp.asarray(blk, BF16), jnp.asarray(r, BF16)


def _pad_blocks(w, axis=-1):
    w = jnp.moveaxis(w, axis, -1)
    w = w.reshape(w.shape[:-1] + (w.shape[-1] // NSA_D, NSA_D))
    w = jnp.pad(w, [(0, 0)] * (w.ndim - 1) + [(0, LANES - NSA_D)])
    return jnp.moveaxis(w.reshape(w.shape[:-2] + (-1,)), -1, axis)


def kernel(x, w_in, w_branch_a, w_branch_b, w_out, norm_ffn1, norm_mix, norm_ffn2, ffn1_w1, ffn1_w3, ffn1_w2, ffn2_w1, ffn2_w3, ffn2_w2, da_q_gain, da_k_gain, da_lambda_q1, da_lambda_k1, da_lambda_q2, da_lambda_k2, da_subln_gain, nsa_q_gain, nsa_k_gain, cmp_pe_k, cmp_w1_k, cmp_w2_k, cmp_pe_v, cmp_w1_v, cmp_w2_v, rel_bias_table):
    b, s, d = x.shape
    depth = w_in.shape[0]
    t = TILE
    assert d == D_MODEL and s % t == 0 and WINDOW % t == 0 and s >= WINDOW
    assert SEL_TOPK <= s // SEL_BLOCK <= LANES - NSA_D
    n = b * s
    nqt = s // t

    place, ovlt, blk, r512 = _static_tables(s)
    tab_a = _tile_bias(rel_bias_table[:, :DA_HEADS], t, False) * LOG2E
    tab_b = _lanes_by_head(_tile_bias(rel_bias_table[:, DA_HEADS:], t, True)) * LOG2E
    longc = _lanes_by_head(_cmp_bias(rel_bias_table[:, DA_HEADS:], t, nqt)) * LOG2E

    bf = lambda w: w.astype(BF16)
    f1w1, f1w3, f1w2, f2w1, f2w3, f2w2 = map(bf, (ffn1_w1, ffn1_w3, ffn1_w2, ffn2_w1, ffn2_w3, ffn2_w2))
    wba, wbb, wo = bf(w_branch_a), bf(w_branch_b), bf(w_out)
    w_row = bf(jnp.concatenate([w_in[:, :, 0:1024], _pad_blocks(w_in[:, :, 1536:2048]), _pad_blocks(w_in[:, :, 2048:2304]),
                                _pad_blocks(w_in[:, :, 2304:2432]), _pad_blocks(w_in[:, :, 2560:2688])], axis=2))
    w_ng = jnp.pad(w_in[:, :, 2816:2840].reshape(depth, d, NSA_G, NSA_P * 3),
                   ((0, 0), (0, 0), (0, 0), (0, GATE_ROWS - NSA_P * 3))).reshape(depth, d, NSA_G * GATE_ROWS)
    w_col = bf(jnp.swapaxes(jnp.concatenate([w_in[:, :, 1024:1536], w_in[:, :, 2432:2560], w_in[:, :, 2688:2816], w_ng],
                                            axis=2), 1, 2))
    w_mg = bf(w_in[:, :, 2840:])
    cw1 = lambda w: bf(_pad_blocks(w.reshape(depth, CMP_BLOCK, NSA_D, CMP_HIDDEN), axis=2))
    cw1k, cw1v = cw1(cmp_w1_k), cw1(cmp_w1_v)
    pek, pev = _pad_blocks(cmp_pe_k.astype(F32)), _pad_blocks(cmp_pe_v.astype(F32))
    cw2k = bf(_pad_blocks(cmp_w2_k))
    cw2vt = bf(jnp.swapaxes(cmp_w2_v, 1, 2))

    x2 = x.reshape(n, d)
    row = lambda v: v.reshape(1, -1).astype(F32)
    tile = lambda v, k: jnp.tile(v.astype(F32), k).reshape(1, -1)
    for l in range(depth):
        x2 = _ffn(x2, row(norm_ffn1[l]), f1w1[l], f1w3[l], f1w2[l])

        dq, dk, nq, kcvc, kaug, vta, vtb, gates = _proj(
            x2, row(norm_mix[l]), w_row[l], w_col[l], r512, blk,
            tile(da_q_gain[l], 8), tile(da_k_gain[l], 8), tile(nsa_q_gain[l], 8),
            tile(nsa_k_gain[l, 1], 4), tile(nsa_k_gain[l, 2], 4), b, s)
        vta = vta.reshape(b, nqt, DA_HEADS * DA_VT_ROWS, t)
        vtb = vtb.reshape(b, nqt, 2 * NSA_G * NSA_VT_ROWS, t)

        lam_init = 0.8 - 0.6 * math.exp(-0.3 * l)
        lamv = jnp.stack([da_lambda_q1[l], da_lambda_k1[l], da_lambda_q2[l], da_lambda_k2[l]]).astype(F32)
        ya = _diff_attention(dq.reshape(b, s, 512), dk.reshape(b, s, 512), vta,
                             tab_a, lamv, row(da_subln_gain[l]), lam_init)

        kc, vct = _compress(kcvc.reshape(b, s, 2 * NSA_G * LANES), pek[l], pev[l], cw1k[l], cw1v[l], cw2k[l], cw2vt[l],
                            _pad_blocks(row(nsa_k_gain[l, 0])))
        yb = _nsa_attention(nq.reshape(b, s, NSA_HEADS * LANES), kaug, vtb, kc, vct, longc, tab_b, gates, place, ovlt)

        x2 = _merge(x2, ya.reshape(n, DA_WIDTH), yb.reshape(n, NSA_WIDTH), row(norm_mix[l]),
                    w_mg[l], wba[l], wbb[l], wo[l])
        x2 = _ffn(x2, row(norm_ffn2[l]), f2w1[l], f2w3[l], f2w2[l])
    return x2.reshape(b, s, d)
```

```python
import functools
import math

import numpy as np
import jax
import jax.numpy as jnp
from jax import lax
from jax.experimental import pallas as pl
from jax.experimental.pallas import tpu as pltpu

F32 = jnp.float32
BF16 = jnp.bfloat16

D_MODEL = 1024
DA_HEADS = 4
DA_HEAD_DIM = 64
DA_V_DIM = 2 * DA_HEAD_DIM
DA_WIDTH = DA_HEADS * DA_V_DIM
NSA_HEADS = 8
NSA_G = 2
NSA_P = NSA_HEADS // NSA_G
NSA_D = 64
NSA_WIDTH = NSA_HEADS * NSA_D
NSA_KV = NSA_G * NSA_D
CMP_BLOCK = 32
CMP_STRIDE = 16
CMP_HIDDEN = 128
SEL_BLOCK = 64
SEL_TOPK = 8
SEL_FORCE = 1e4
WINDOW = 512
N_BUCKETS = 32
MAX_DISTANCE = 1024
D_FF = 2816
EPS = 1e-6
NEG = -1e30

LANES = 128
VMEM_LIMIT = 52 * 1024 * 1024

TILE = 256
CW = LANES
FLASH_UNROLL = 4
GATE_ROWS = 16
ONES_ROWS = 16
DA_VT_ROWS = DA_V_DIM + ONES_ROWS
NSA_VT_ROWS = NSA_D + ONES_ROWS
LOG2E = math.log2(math.e)

COL_ROWS = DA_WIDTH + 2 * NSA_KV + NSA_G * GATE_ROWS


def _cparams(sem):
    return pltpu.CompilerParams(dimension_semantics=sem, vmem_limit_bytes=VMEM_LIMIT)


def _const_spec(shape):
    nd = len(shape)
    return pl.BlockSpec(shape, lambda *_: (0,) * nd)


def _rms(xf, g):
    ms = jnp.mean(xf * xf, axis=-1, keepdims=True)
    return xf * lax.rsqrt(ms + EPS) * g


def _dot(a, b):
    return jnp.dot(a, b, preferred_element_type=F32)


def _dot_nt(a, b):
    return lax.dot_general(a, b, (((1,), (1,)), ((), ())), preferred_element_type=F32)


def _bf16_pieces(x, parts):
    out = []
    r = x
    for i in range(parts):
        piece = r.astype(BF16)
        out.append(piece)
        if i + 1 < parts:
            r = r - piece.astype(F32)
    return out


def _ffn_kernel(x_ref, g_ref, w1_ref, w3_ref, w2_ref, o_ref):
    x = x_ref[...]
    h = _rms(x, g_ref[...]).astype(BF16)
    a = _dot(h, w1_ref[...])
    b = _dot(h, w3_ref[...])
    t = (jax.nn.silu(a) * b).astype(BF16)
    o_ref[...] = x + 0.5 * _dot(t, w2_ref[...])


def _ffn(x2, g, w1, w3, w2):
    n, d = x2.shape
    tm = TILE
    return pl.pallas_call(
        _ffn_kernel,
        grid=(n // tm,),
        in_specs=[
            pl.BlockSpec((tm, d), lambda i: (i, 0)),
            _const_spec((1, d)),
            _const_spec(w1.shape),
            _const_spec(w3.shape),
            _const_spec(w2.shape),
        ],
        out_specs=pl.BlockSpec((tm, d), lambda i: (i, 0)),
        out_shape=jax.ShapeDtypeStruct((n, d), F32),
        compiler_params=_cparams(("parallel",)),
        name="ffn_half_step",
    )(x2, g, w1, w3, w2)


def _group_rms(z, r_ref, gain):
    w = z.shape[-1]
    ss = _dot((z * z).astype(BF16), r_ref[0:w, 0:w])
    return z * lax.rsqrt(ss * (1.0 / 64.0) + EPS) * gain


def _proj_kernel(x_ref, g_ref, w_ref, wt_ref, r_ref, blk_ref, gq_ref, gk_ref, gnq_ref, gks_ref, gkw_ref,
                 dq_ref, dk_ref, nq_ref, kcvc_ref, kaug_ref, vta_ref, vtb_ref, gate_ref):
    h = _rms(x_ref[...], g_ref[...]).astype(BF16)
    z = _dot(h, w_ref[...])
    zt = _dot_nt(wt_ref[...], h)
    scale = DA_HEAD_DIM ** -0.5 * LOG2E
    dq_ref[...] = (_group_rms(z[:, 0:512], r_ref, gq_ref[...]) * scale).astype(BF16)
    dk_ref[...] = _group_rms(z[:, 512:1024], r_ref, gk_ref[...]).astype(BF16)
    low = lax.broadcasted_iota(jnp.int32, (z.shape[0], LANES), 1) < NSA_D

    def lane_blocks(v):
        out = []
        for c in range(v.shape[1] // LANES):
            pair = v[:, c * LANES:(c + 1) * LANES]
            out.append(jnp.where(low, pair, 0.0))
            out.append(jnp.where(low, pltpu.roll(pair, NSA_D, axis=1), 0.0))
        return out

    nq = _group_rms(z[:, 1024:1536], r_ref, gnq_ref[...]) * scale
    for hd, piece in enumerate(lane_blocks(nq)):
        nq_ref[:, hd * LANES:(hd + 1) * LANES] = piece.astype(BF16)
    for c, piece in enumerate(lane_blocks(z[:, 1536:1792])):
        kcvc_ref[:, c * LANES:(c + 1) * LANES] = piece
    blk = blk_ref[...].astype(F32)
    for g, piece in enumerate(lane_blocks(_group_rms(z[:, 1792:1920], r_ref, gks_ref[...]))):
        kaug_ref[0, 0, g] = (piece + blk).astype(BF16)
    for g, piece in enumerate(lane_blocks(_group_rms(z[:, 1920:2048], r_ref, gkw_ref[...]))):
        kaug_ref[0, 1, g] = piece.astype(BF16)
    ones = jnp.ones((ONES_ROWS, zt.shape[1]), BF16)
    for hd in range(DA_HEADS):
        vta_ref[0, hd * DA_VT_ROWS:hd * DA_VT_ROWS + DA_V_DIM] = zt[hd * DA_V_DIM:(hd + 1) * DA_V_DIM].astype(BF16)
        vta_ref[0, hd * DA_VT_ROWS + DA_V_DIM:(hd + 1) * DA_VT_ROWS] = ones
    for blk4 in range(2 * NSA_G):
        src = DA_WIDTH + (blk4 % 2) * NSA_KV + (blk4 // 2) * NSA_D
        vtb_ref[0, blk4 * NSA_VT_ROWS:blk4 * NSA_VT_ROWS + NSA_D] = zt[src:src + NSA_D].astype(BF16)
        vtb_ref[0, blk4 * NSA_VT_ROWS + NSA_D:(blk4 + 1) * NSA_VT_ROWS] = ones
    gate_ref[0] = jax.nn.sigmoid(zt[DA_WIDTH + 2 * NSA_KV:COL_ROWS])


def _proj(x2, g, w, wt, r, blk, gq, gk, gnq, gks, gkw, b, s):
    n, d = x2.shape
    tm = TILE
    nt = s // tm
    row = lambda wd: pl.BlockSpec((tm, wd), lambda i: (i, 0))
    col = lambda rows: pl.BlockSpec((1, rows, tm), lambda i: (i, 0, 0))
    return pl.pallas_call(
        _proj_kernel,
        grid=(n // tm,),
        in_specs=[row(d), _const_spec((1, d)), _const_spec(w.shape), _const_spec(wt.shape),
                  _const_spec(r.shape), pl.BlockSpec((tm, LANES), lambda i: (i % nt, 0)),
                  _const_spec((1, 512)), _const_spec((1, 512)), _const_spec((1, 512)),
                  _const_spec((1, NSA_KV)), _const_spec((1, NSA_KV))],
        out_specs=[row(512), row(512), row(NSA_HEADS * LANES), row(2 * NSA_G * LANES),
                   pl.BlockSpec((1, 2, NSA_G, tm, LANES), lambda i: (i // nt, 0, 0, i % nt, 0)),
                   col(DA_HEADS * DA_VT_ROWS), col(2 * NSA_G * NSA_VT_ROWS), col(NSA_G * GATE_ROWS)],
        out_shape=[
            jax.ShapeDtypeStruct((n, 512), BF16),
            jax.ShapeDtypeStruct((n, 512), BF16),
            jax.ShapeDtypeStruct((n, NSA_HEADS * LANES), BF16),
            jax.ShapeDtypeStruct((n, 2 * NSA_G * LANES), F32),
            jax.ShapeDtypeStruct((b, 2, NSA_G, s, LANES), BF16),
            jax.ShapeDtypeStruct((n // tm, DA_HEADS * DA_VT_ROWS, tm), BF16),
            jax.ShapeDtypeStruct((n // tm, 2 * NSA_G * NSA_VT_ROWS, tm), BF16),
            jax.ShapeDtypeStruct((n // tm, NSA_G * GATE_ROWS, tm), F32),
        ],
        compiler_params=_cparams(("parallel",)),
        name="norm_in_proj",
    )(x2, g, w, wt, r, blk, gq, gk, gnq, gks, gkw)


def _cmp_kernel(ck_ref, cv_ref, pek_ref, pev_ref, w1k_ref, w1v_ref, w2k_ref, w2vt_ref, gk_ref, kc_ref, vct_ref):
    nc = kc_ref.shape[2]

    def hidden(c_ref, pe_ref, w1_ref):
        lo = hi = None
        for t in range(CMP_STRIDE):
            x = c_ref[0, pl.ds(t, nc, stride=CMP_STRIDE), :]
            a = _dot((x + pe_ref[t:t + 1, :]).astype(BF16), w1_ref[t])
            b = _dot((x + pe_ref[CMP_STRIDE + t:CMP_STRIDE + t + 1, :]).astype(BF16), w1_ref[CMP_STRIDE + t])
            lo = a if lo is None else lo + a
            hi = b if hi is None else hi + b
        return jax.nn.gelu(lo + pltpu.roll(hi, nc - 1, axis=0)).astype(BF16)

    kc = _dot(hidden(ck_ref, pek_ref, w1k_ref), w2k_ref[...])
    ms = jnp.sum(kc * kc, axis=-1, keepdims=True) * (1.0 / NSA_D)
    kc_ref[0, 0] = (kc * lax.rsqrt(ms + EPS) * gk_ref[...]).astype(BF16)
    vct_ref[0, 0] = _dot_nt(w2vt_ref[...], hidden(cv_ref, pev_ref, w1v_ref)).astype(BF16)


def _compress(kcvc, pek, pev, w1k, w1v, w2k, w2vt, gk):
    b, s, _ = kcvc.shape
    g = NSA_G
    nc = s // CMP_STRIDE
    return pl.pallas_call(
        _cmp_kernel,
        grid=(b, g),
        in_specs=[pl.BlockSpec((1, s, LANES), lambda i, j: (i, 0, j)),
                  pl.BlockSpec((1, s, LANES), lambda i, j: (i, 0, NSA_G + j)),
                  _const_spec(pek.shape), _const_spec(pev.shape),
                  _const_spec(w1k.shape), _const_spec(w1v.shape),
                  _const_spec(w2k.shape), _const_spec(w2vt.shape), _const_spec(gk.shape)],
        out_specs=[pl.BlockSpec((1, 1, nc, LANES), lambda i, j: (i, j, 0, 0)),
                   pl.BlockSpec((1, 1, NSA_D, nc), lambda i, j: (i, j, 0, 0))],
        out_shape=[jax.ShapeDtypeStruct((b, g, nc, LANES), BF16),
                   jax.ShapeDtypeStruct((b, g, NSA_D, nc), BF16)],
        compiler_params=_cparams(("parallel", "parallel")),
        name="block_compress",
    )(kcvc, kcvc, pek, pev, w1k, w1v, w2k, w2vt, gk)


def _flash_list(n, qk_fn, vt_fn, bias_fn, state_fn, s_ref, mx_ref, p_ref, alpha_ref, bias_with_scores,
                before_loop=None):
    width = s_ref.shape[2]
    chunks = [slice(c * CW, (c + 1) * CW) for c in range(width // CW)]

    def scores(i, slot):
        raw = qk_fn(i)
        if not bias_with_scores:
            s_ref[slot] = raw
            return
        for cs in chunks:
            sb = raw[:, cs] + bias_fn(i, cs)
            s_ref[slot, :, cs] = sb
            mx_ref[slot, :, cs] = jnp.max(sb, axis=0, keepdims=True)

    def pv(i, slot):
        acc_ref = state_fn(i)[1]
        acc_ref[...] = alpha_ref[slot] * acc_ref[...] + _dot(vt_fn(i), p_ref[slot])

    def softmax(i, slot):
        m_ref = state_fn(i)[0]
        if bias_with_scores:
            m_old = m_ref[...]
            m_new = jnp.maximum(m_old, mx_ref[slot])
            alpha_ref[slot] = jnp.exp2(m_old - m_new)
            m_ref[...] = m_new
            for cs in chunks:
                p_ref[slot, :, cs] = jnp.exp2(s_ref[slot, :, cs] - m_new[:, cs]).astype(BF16)
            return
        for cs in chunks:
            s = s_ref[slot, :, cs] + bias_fn(i, cs)
            m_old = m_ref[:, cs]
            m_new = jnp.maximum(m_old, jnp.max(s, axis=0, keepdims=True))
            p_ref[slot, :, cs] = jnp.exp2(s - m_new).astype(BF16)
            alpha_ref[slot, :, cs] = jnp.exp2(m_old - m_new)
            m_ref[:, cs] = m_new

    scores(0, 0)
    softmax(0, 0)
    scores(1, 1)
    if before_loop is not None:
        before_loop()

    def body(g, carry):
        first = 1 + FLASH_UNROLL * g
        for u in range(FLASH_UNROLL):
            slot = (u + 1) % 2
            pv(first + u - 1, 1 - slot)
            softmax(first + u, slot)
            scores(first + u + 1, 1 - slot)
        return carry

    groups = (n - 1 + FLASH_UNROLL - 1) // FLASH_UNROLL
    lax.fori_loop(0, groups, body, 0)
    pv(FLASH_UNROLL * groups, 0)


def _to_rows(xt):
    t = xt.shape[1]
    return jnp.concatenate([xt[:, c * LANES:(c + 1) * LANES].T for c in range(t // LANES)], axis=0)


def _da_kernel(lam_init, nd, nqt, qt_of, kt_of, q_ref, k_ref, vt_ref, tab_ref, lamv_ref, gs_ref, o_ref,
               qq_ref, s_ref, mx_ref, p_ref, alpha_ref, m_ref, acc_ref):
    t = TILE
    lane = lax.broadcasted_iota(jnp.int32, (t, LANES), 1)
    for qt in range(nqt):
        q = q_ref[0, qt * t:(qt + 1) * t, :].astype(F32)
        qq_ref[qt, 0:t, :] = jnp.where(lane < DA_HEAD_DIM, q, 0.0).astype(BF16)
        qq_ref[qt, t:2 * t, :] = jnp.where(lane >= DA_HEAD_DIM, q, 0.0).astype(BF16)
    m_ref[...] = jnp.full(m_ref.shape, NEG, F32)
    acc_ref[...] = jnp.zeros(acc_ref.shape, F32)

    key_tile = lambda i: jnp.minimum(kt_of[i], nqt - 1)

    def qk(i):
        return _dot_nt(k_ref[0, pl.ds(pl.multiple_of(key_tile(i) * t, t), t), :], qq_ref[qt_of[i]])

    def bias(i, cs):
        d = jnp.clip(qt_of[i] - kt_of[i], -1, nd)
        return tab_ref[0, d + 1, :, slice(cs.start % t, cs.start % t + CW)]

    _flash_list(nqt * (nqt + 1) // 2, qk, lambda i: vt_ref[0, key_tile(i)], bias,
                lambda i: (m_ref.at[qt_of[i]], acc_ref.at[qt_of[i]]), s_ref, mx_ref, p_ref, alpha_ref, True)

    lv = lamv_ref[...]
    lam = (jnp.exp(jnp.sum(lv[0:1] * lv[1:2], axis=-1, keepdims=True))
           - jnp.exp(jnp.sum(lv[2:3] * lv[3:4], axis=-1, keepdims=True)) + lam_init)

    for qt in range(nqt):
        ot = acc_ref[qt, 0:DA_V_DIM, :] / acc_ref[qt, DA_V_DIM:DA_V_DIM + 1, :]
        y = _to_rows(ot[:, 0:t] - lam * ot[:, t:2 * t])
        o_ref[0, qt * t:(qt + 1) * t, :] = (_rms(y, gs_ref[...]) * (1.0 - lam_init)).astype(BF16)


def _diff_attention(dq, dk, vt, tab, lamv, gs, lam_init):
    b, s, _ = dq.shape
    t = TILE
    nqt = s // t
    nd = tab.shape[1] - 2
    pairs = [(qt, kt) for qt in range(nqt) for kt in range(qt + 1)] + [(nqt - 1, nqt)] * (FLASH_UNROLL + 2)
    qt_of = jnp.asarray(np.array([p[0] for p in pairs], np.int32))
    kt_of = jnp.asarray(np.array([p[1] for p in pairs], np.int32))
    seq = pl.BlockSpec((1, s, LANES), lambda i, h, *_: (i, 0, h))
    return pl.pallas_call(
        functools.partial(_da_kernel, lam_init, nd, nqt),
        grid_spec=pltpu.PrefetchScalarGridSpec(
            num_scalar_prefetch=2,
            grid=(b, DA_HEADS),
            in_specs=[
                seq, seq,
                pl.BlockSpec((1, nqt, DA_VT_ROWS, t), lambda i, h, *_: (i, 0, h, 0)),
                pl.BlockSpec((1,) + tab.shape[1:], lambda i, h, *_: (h, 0, 0, 0)),
                pl.BlockSpec(lamv.shape, lambda i, h, *_: (0, 0)),
                pl.BlockSpec(gs.shape, lambda i, h, *_: (0, 0)),
            ],
            out_specs=seq,
            scratch_shapes=[
                pltpu.VMEM((nqt, 2 * t, LANES), BF16),
                pltpu.VMEM((2, t, 2 * t), F32),
                pltpu.VMEM((2, 1, 2 * t), F32),
                pltpu.VMEM((2, t, 2 * t), BF16),
                pltpu.VMEM((2, 1, 2 * t), F32),
                pltpu.VMEM((nqt, 1, 2 * t), F32),
                pltpu.VMEM((nqt, DA_VT_ROWS, 2 * t), F32),
            ]),
        out_shape=jax.ShapeDtypeStruct((b, s, DA_WIDTH), BF16),
        compiler_params=_cparams(("parallel", "parallel")),
        name="diff_attention",
    )(qt_of, kt_of, dq, dk, vt, tab, lamv, gs)


def _nsa_kernel(nd, nqt, q_ref, kaug_ref, vt_ref, kc_ref, vct_ref, longc_ref, tab_ref, gate_ref,
                place_ref, ovlt_ref, o_ref,
                qz_ref, qs_ref, s_ref, p_ref, alpha_ref, m_ref, acc_ref, comb_ref, psum_ref, sc_ref, pc_ref):
    t = TILE
    halves = t // CW
    nchunk = NSA_P * halves
    qi = pl.program_id(2)
    win_tiles = WINDOW // t
    win_edge = nd + 2

    def gate_row(r):
        return jnp.concatenate([gate_ref[0, 3 * p + r:3 * p + r + 1, :] for p in range(NSA_P)], axis=1)

    for p in range(NSA_P):
        qz_ref[p * t:(p + 1) * t, :] = q_ref[0, :, p * LANES:(p + 1) * LANES]

    def select_blocks():
        nc = sc_ref.shape[0]
        cstart = pl.multiple_of((nqt - 1 - qi) * (t // CMP_STRIDE), t // CMP_STRIDE)
        sc_ref[...] = _dot_nt(kc_ref[0, 0], qz_ref[...])
        for c in range(nchunk):
            p, half = divmod(c, halves)
            cs = slice(c * CW, (c + 1) * CW)
            hs = slice(half * CW, (half + 1) * CW)
            s = sc_ref[:, cs] + longc_ref[0, pl.ds(cstart, nc), cs]
            m = jnp.max(s, axis=0, keepdims=True)
            e = jnp.exp2(s - m)
            pc = jnp.where(m > 0.5 * NEG, e / jnp.sum(e, axis=0, keepdims=True), 0.0)
            pc_ref[:, cs] = pc.astype(BF16)
            if p == 0:
                psum_ref[:, hs] = pc
            else:
                psum_ref[:, hs] += pc
        comb_ref[...] = gate_row(0) * _dot(vct_ref[0, 0], pc_ref[...])
        select_topk()

    def select_topk():
        ovlt = ovlt_ref[...]
        imp = None
        for piece in _bf16_pieces(psum_ref[...], 3):
            d = _dot(ovlt, piece)
            imp = d if imp is None else imp + d
        nblk = imp.shape[0]
        jrow = lax.broadcasted_iota(jnp.int32, (nblk, t), 0)
        cur = (qi * t + lax.broadcasted_iota(jnp.int32, (nblk, t), 1)) // SEL_BLOCK
        forced = (jrow == 0) | (jrow == cur) | (jrow == cur - 1)
        lowest = jnp.float32(-3e38)
        score = jnp.where(forced | (jrow > cur), lowest, imp)
        jrow_f = jrow.astype(F32)
        nsel = jnp.where(forced, 0.0, 1.0)
        for _ in range(SEL_TOPK - 3):
            mx = jnp.max(score, axis=0, keepdims=True)
            first = jnp.min(jnp.where(score == mx, jrow_f, 2.0 * LANES), axis=0, keepdims=True)
            hit = jrow_f == first
            nsel = jnp.where(hit, 0.0, nsel)
            score = jnp.where(hit, lowest, score)
        parts = [jnp.zeros((NSA_D, t), F32), nsel]
        if LANES - NSA_D - nblk:
            parts.append(jnp.zeros((LANES - NSA_D - nblk, t), F32))
        nsel_rows = _to_rows(jnp.concatenate(parts, axis=0))
        for p in range(NSA_P):
            rs = slice(p * t, (p + 1) * t)
            qs_ref[rs, :] = (qz_ref[rs, :].astype(F32) + nsel_rows).astype(BF16)

    m_ref[...] = jnp.full(m_ref.shape, NEG, F32)
    acc_ref[...] = jnp.zeros(acc_ref.shape, F32)
    nwin = win_tiles + 1

    def entry(i):
        is_win = jnp.asarray(i < nwin)
        kt = jnp.where(is_win, qi - win_tiles + i, i - nwin)
        return is_win, kt, qi - kt

    def qk(i):
        is_win, kt, _ = entry(i)
        rows = pl.ds(pl.multiple_of(jnp.clip(kt, 0, qi) * t, t), t)
        q_all = qz_ref if isinstance(i, int) and i < nwin else qs_ref
        return _dot_nt(kaug_ref[0, is_win.astype(jnp.int32), 0, rows, :], q_all[...])

    def vt(i):
        is_win, kt, _ = entry(i)
        rows = pl.ds(pl.multiple_of(is_win.astype(jnp.int32) * NSA_VT_ROWS, NSA_VT_ROWS), NSA_VT_ROWS)
        return vt_ref[0, jnp.clip(kt, 0, qi), rows, :]

    def bias(i, cs):
        is_win, kt, d = entry(i)
        idx = jnp.where(is_win & (d == win_tiles), win_edge, jnp.clip(d, -1, nd) + 1)
        return tab_ref[0, jnp.where(kt < 0, 0, idx), :, cs]

    def state(i):
        st = jnp.asarray(i < nwin).astype(jnp.int32)
        return m_ref.at[st], acc_ref.at[st]

    _flash_list(nwin + qi + 1, qk, vt, bias, state, s_ref, None, p_ref, alpha_ref, False,
                before_loop=select_blocks)

    def branch_out(st):
        return acc_ref[st, 0:NSA_D, :] / acc_ref[st, NSA_D:NSA_D + 1, :]

    comb = comb_ref[...] + gate_row(1) * branch_out(0) + gate_row(2) * branch_out(1)

    pad = jnp.zeros((LANES - NSA_D, t), F32)
    out = None
    for p in range(NSA_P):
        rows = _to_rows(jnp.concatenate([comb[:, p * t:(p + 1) * t], pad], axis=0)).astype(BF16)
        d = _dot(rows, place_ref[p])
        out = d if out is None else out + d
    o_ref[0] = out.astype(BF16)


def _nsa_attention(nq, kaug, vt, kc, vct, longc, tab, gates, place, ovlt):
    b, s, _ = nq.shape
    t = TILE
    nqt = s // t
    nc = kc.shape[2]
    nd = tab.shape[1] - 3
    gw = NSA_P * NSA_D
    return pl.pallas_call(
        functools.partial(_nsa_kernel, nd, nqt),
        grid=(b, NSA_G, nqt),
        in_specs=[
            pl.BlockSpec((1, t, NSA_P * LANES), lambda i, g, j: (i, j, g)),
            pl.BlockSpec((1, 2, 1, s, LANES), lambda i, g, j: (i, 0, g, 0, 0)),
            pl.BlockSpec((1, nqt, 2 * NSA_VT_ROWS, t), lambda i, g, j: (i, 0, g, 0)),
            pl.BlockSpec((1, 1, nc, LANES), lambda i, g, j: (i, g, 0, 0)),
            pl.BlockSpec((1, 1, NSA_D, nc), lambda i, g, j: (i, g, 0, 0)),
            pl.BlockSpec((1,) + longc.shape[1:], lambda i, g, j: (g, 0, 0)),
            pl.BlockSpec((1,) + tab.shape[1:], lambda i, g, j: (g, 0, 0, 0)),
            pl.BlockSpec((1, GATE_ROWS, t), lambda i, g, j: (i * nqt + j, g, 0)),
            _const_spec(place.shape), _const_spec(ovlt.shape),
        ],
        out_specs=pl.BlockSpec((1, t, gw), lambda i, g, j: (i, j, g)),
        out_shape=jax.ShapeDtypeStruct((b, s, NSA_WIDTH), BF16),
        scratch_shapes=[
            pltpu.VMEM((NSA_P * t, LANES), BF16),
            pltpu.VMEM((NSA_P * t, LANES), BF16),
            pltpu.VMEM((2, t, NSA_P * t), F32),
            pltpu.VMEM((2, t, NSA_P * t), BF16),
            pltpu.VMEM((2, 1, NSA_P * t), F32),
            pltpu.VMEM((2, 1, NSA_P * t), F32),
            pltpu.VMEM((2, NSA_VT_ROWS, NSA_P * t), F32),
            pltpu.VMEM((NSA_D, NSA_P * t), F32),
            pltpu.VMEM((nc, t), F32),
            pltpu.VMEM((nc, NSA_P * t), F32),
            pltpu.VMEM((nc, NSA_P * t), BF16),
        ],
        compiler_params=_cparams(("parallel", "parallel", "arbitrary")),
        name="nsa_attention",
    )(nq, kaug, vt, kc, vct, longc, tab, gates, place, ovlt)


def _merge_kernel(x_ref, ya_ref, yb_ref, g_ref, wmg_ref, pa_ref, pb_ref, wo_ref, o_ref):
    x = x_ref[...]
    h = _rms(x, g_ref[...]).astype(BF16)
    mg = jax.nn.sigmoid(_dot(h, wmg_ref[...]))
    merged = mg[:, 0:D_MODEL] * _dot(ya_ref[...], pa_ref[...]) + mg[:, D_MODEL:] * _dot(yb_ref[...], pb_ref[...])
    o_ref[...] = x + _dot(merged.astype(BF16), wo_ref[...])


def _merge(x2, ya, yb, g, wmg, pa, pb, wo):
    n, d = x2.shape
    tm = TILE
    row = lambda wd: pl.BlockSpec((tm, wd), lambda i: (i, 0))
    return pl.pallas_call(
        _merge_kernel,
        grid=(n // tm,),
        in_specs=[row(d), row(DA_WIDTH), row(NSA_WIDTH), _const_spec((1, d)),
                  _const_spec(wmg.shape), _const_spec(pa.shape), _const_spec(pb.shape), _const_spec(wo.shape)],
        out_specs=row(d),
        out_shape=jax.ShapeDtypeStruct((n, d), F32),
        compiler_params=_cparams(("parallel",)),
        name="gated_merge_out_proj",
    )(x2, ya, yb, g, wmg, pa, pb, wo)


def _rel_bucket(dist):
    n = jnp.maximum(dist, 0)
    max_exact = N_BUCKETS // 2
    nf = jnp.maximum(n, 1).astype(jnp.float32)
    large = max_exact + (jnp.log(nf / max_exact) / math.log(MAX_DISTANCE / max_exact)
                         * (N_BUCKETS - max_exact)).astype(jnp.int32)
    large = jnp.minimum(large, N_BUCKETS - 1)
    return jnp.where(n < max_exact, n, large)


def _bias_of(table, dist):
    bucket = _rel_bucket(dist)
    col = lambda v: v.reshape((-1,) + (1,) * dist.ndim)
    out = jnp.zeros((table.shape[1],) + dist.shape, F32)
    for bkt in range(N_BUCKETS):
        out = jnp.where(bucket == bkt, col(table[bkt].astype(F32)), out)
    return jnp.where(dist >= 0, out, NEG)


def _far_tiles(t):
    return -(-(MAX_DISTANCE + t - 1) // t)


def _tile_bias(table, t, window_edge):
    nd = _far_tiles(t)
    d0 = 2 * t - 1
    width = (nd + 3) * t - 1
    v = _bias_of(table, jnp.arange(width) - d0)
    w = jnp.concatenate([v, jnp.zeros((v.shape[0], 1), F32)], axis=1)
    m = jnp.tile(w, (1, t))[:, :t * width].reshape(-1, t, width)
    bias = jnp.stack([m[:, :, d * t + d0:d * t + d0 + t] for d in range(-1, nd + 1)], axis=1)
    if window_edge:
        we = WINDOW // t
        j = jnp.arange(t)[:, None]
        i = jnp.arange(t)[None, :]
        edge = jnp.where(we * t + i - j < WINDOW, bias[:, we + 1], NEG)
        bias = jnp.concatenate([bias, edge[:, None]], axis=1)
    return bias


def _cmp_bias(table, t, nqt):
    cpt = t // CMP_STRIDE
    r = jnp.arange((2 * nqt - 1) * cpt)[:, None]
    dist = (nqt - 1 - r // cpt) * t + jnp.arange(t)[None, :] - ((r % cpt) * CMP_STRIDE + CMP_BLOCK - 1)
    return _bias_of(table, dist)


def _lanes_by_head(x):
    x = x.reshape((NSA_G, NSA_P) + x.shape[1:])
    x = jnp.moveaxis(x, 1, -2)
    return x.reshape(x.shape[:-2] + (NSA_P * x.shape[-1],))


def _static_tables(s):
    nc = s // CMP_STRIDE
    n_cmp = nc - CMP_BLOCK // CMP_STRIDE + 1
    nblk = s // SEL_BLOCK
    place = np.zeros((NSA_P, LANES, NSA_P * NSA_D), np.float32)
    for p in range(NSA_P):
        for dd in range(NSA_D):
            place[p, dd, p * NSA_D + dd] = 1.0
    ovlt = np.zeros((nblk, nc), np.float32)
    for c in range(n_cmp):
        for jb in range(nblk):
            if c * CMP_STRIDE < (jb + 1) * SEL_BLOCK and c * CMP_STRIDE + CMP_BLOCK - 1 >= jb * SEL_BLOCK:
                ovlt[jb, c] = 1.0
    blk = np.zeros((s, LANES), np.float32)
    blk[np.arange(s), NSA_D + np.arange(s) // SEL_BLOCK] = NEG
    r = np.kron(np.eye(512 // 64, dtype=np.float32), np.ones((64, 64), np.float32))
    return jnp.asarray(place, BF16), jnp.asarray(ovlt, BF16), jnp.asarray(blk, BF16), jnp.asarray(r, BF16)


def _pad_blocks(w, axis=-1):
    w = jnp.moveaxis(w, axis, -1)
    w = w.reshape(w.shape[:-1] + (w.shape[-1] // NSA_D, NSA_D))
    w = jnp.pad(w, [(0, 0)] * (w.ndim - 1) + [(0, LANES - NSA_D)])
    return jnp.moveaxis(w.reshape(w.shape[:-2] + (-1,)), -1, axis)


def kernel(x, w_in, w_branch_a, w_branch_b, w_out, norm_ffn1, norm_mix, norm_ffn2, ffn1_w1, ffn1_w3, ffn1_w2, ffn2_w1, ffn2_w3, ffn2_w2, da_q_gain, da_k_gain, da_lambda_q1, da_lambda_k1, da_lambda_q2, da_lambda_k2, da_subln_gain, nsa_q_gain, nsa_k_gain, cmp_pe_k, cmp_w1_k, cmp_w2_k, cmp_pe_v, cmp_w1_v, cmp_w2_v, rel_bias_table):
    b, s, d = x.shape
    depth = w_in.shape[0]
    t = TILE
    assert d == D_MODEL and s % t == 0 and WINDOW % t == 0 and s >= WINDOW
    assert SEL_TOPK <= s // SEL_BLOCK <= LANES - NSA_D
    n = b * s
    nqt = s // t

    place, ovlt, blk, r512 = _static_tables(s)
    tab_a = _tile_bias(rel_bias_table[:, :DA_HEADS], t, False) * LOG2E
    tab_b = _lanes_by_head(_tile_bias(rel_bias_table[:, DA_HEADS:], t, True)) * LOG2E
    longc = _lanes_by_head(_cmp_bias(rel_bias_table[:, DA_HEADS:], t, nqt)) * LOG2E

    bf = lambda w: w.astype(BF16)
    f1w1, f1w3, f1w2, f2w1, f2w3, f2w2 = map(bf, (ffn1_w1, ffn1_w3, ffn1_w2, ffn2_w1, ffn2_w3, ffn2_w2))
    wba, wbb, wo = bf(w_branch_a), bf(w_branch_b), bf(w_out)
    w_row = bf(jnp.concatenate([w_in[:, :, 0:1024], w_in[:, :, 1536:2304], w_in[:, :, 2304:2432], w_in[:, :, 2560:2688]],
                               axis=2))
    w_ng = jnp.pad(w_in[:, :, 2816:2840].reshape(depth, d, NSA_G, NSA_P * 3),
                   ((0, 0), (0, 0), (0, 0), (0, GATE_ROWS - NSA_P * 3))).reshape(depth, d, NSA_G * GATE_ROWS)
    w_col = bf(jnp.swapaxes(jnp.concatenate([w_in[:, :, 1024:1536], w_in[:, :, 2432:2560], w_in[:, :, 2688:2816], w_ng],
                                            axis=2), 1, 2))
    w_mg = bf(w_in[:, :, 2840:])
    cw1 = lambda w: bf(_pad_blocks(w.reshape(depth, CMP_BLOCK, NSA_D, CMP_HIDDEN), axis=2))
    cw1k, cw1v = cw1(cmp_w1_k), cw1(cmp_w1_v)
    pek, pev = _pad_blocks(cmp_pe_k.astype(F32)), _pad_blocks(cmp_pe_v.astype(F32))
    cw2k = bf(_pad_blocks(cmp_w2_k))
    cw2vt = bf(jnp.swapaxes(cmp_w2_v, 1, 2))

    x2 = x.reshape(n, d)
    row = lambda v: v.reshape(1, -1).astype(F32)
    tile = lambda v, k: jnp.tile(v.astype(F32), k).reshape(1, -1)
    for l in range(depth):
        x2 = _ffn(x2, row(norm_ffn1[l]), f1w1[l], f1w3[l], f1w2[l])

        dq, dk, nq, kcvc, kaug, vta, vtb, gates = _proj(
            x2, row(norm_mix[l]), w_row[l], w_col[l], r512, blk,
            tile(da_q_gain[l], 8), tile(da_k_gain[l], 8), tile(nsa_q_gain[l], 8),
            tile(nsa_k_gain[l, 1], NSA_G), tile(nsa_k_gain[l, 2], NSA_G), b, s)
        vta = vta.reshape(b, nqt, DA_HEADS * DA_VT_ROWS, t)
        vtb = vtb.reshape(b, nqt, 2 * NSA_G * NSA_VT_ROWS, t)

        lam_init = 0.8 - 0.6 * math.exp(-0.3 * l)
        lamv = jnp.stack([da_lambda_q1[l], da_lambda_k1[l], da_lambda_q2[l], da_lambda_k2[l]]).astype(F32)
        ya = _diff_attention(dq.reshape(b, s, 512), dk.reshape(b, s, 512), vta,
                             tab_a, lamv, row(da_subln_gain[l]), lam_init)

        kc, vct = _compress(kcvc.reshape(b, s, 2 * NSA_G * LANES), pek[l], pev[l], cw1k[l], cw1v[l], cw2k[l], cw2vt[l],
                            _pad_blocks(row(nsa_k_gain[l, 0])))
        yb = _nsa_attention(nq.reshape(b, s, NSA_HEADS * LANES), kaug, vtb, kc, vct, longc, tab_b, gates, place, ovlt)

        x2 = _merge(x2, ya.reshape(n, DA_WIDTH), yb.reshape(n, NSA_WIDTH), row(norm_mix[l]),
                    w_mg[l], wba[l], wbb[l], wo[l])
        x2 = _ffn(x2, row(norm_ffn2[l]), f2w1[l], f2w3[l], f2w2[l])
    return x2.reshape(b, s, d)
```

```python
import functools
import math

import numpy as np
import jax
import jax.numpy as jnp
from jax import lax
from jax.experimental import pallas as pl
from jax.experimental.pallas import tpu as pltpu

F32 = jnp.float32
BF16 = jnp.bfloat16

D_MODEL = 1024
DA_HEADS = 4
DA_HEAD_DIM = 64
DA_V_DIM = 2 * DA_HEAD_DIM
DA_WIDTH = DA_HEADS * DA_V_DIM
NSA_HEADS = 8
NSA_G = 2
NSA_P = NSA_HEADS // NSA_G
NSA_D = 64
NSA_WIDTH = NSA_HEADS * NSA_D
NSA_KV = NSA_G * NSA_D
CMP_BLOCK = 32
CMP_STRIDE = 16
CMP_HIDDEN = 128
SEL_BLOCK = 64
SEL_TOPK = 8
SEL_FORCED = 3
WINDOW = 512
N_BUCKETS = 32
MAX_DISTANCE = 1024
EPS = 1e-6
NEG = -1e30

LANES = 128
VMEM_LIMIT = 52 * 1024 * 1024

TILE = 256
FFN_ROWS = 512
CW = LANES
FLASH_UNROLL = 4
GATE_ROWS = 16
ONES_ROWS = 16
DA_VT_ROWS = DA_V_DIM + ONES_ROWS
NSA_VT_ROWS = NSA_D + ONES_ROWS
LOG2E = math.log2(math.e)

COL_ROWS = DA_WIDTH + 2 * NSA_KV + NSA_G * GATE_ROWS


def _cparams(sem):
    return pltpu.CompilerParams(dimension_semantics=sem, vmem_limit_bytes=VMEM_LIMIT)


def _const_spec(shape):
    nd = len(shape)
    return pl.BlockSpec(shape, lambda *_: (0,) * nd)


def _rms(xf, g):
    ms = jnp.mean(xf * xf, axis=-1, keepdims=True)
    return xf * lax.rsqrt(ms + EPS) * g


def _dot(a, b):
    return jnp.dot(a, b, preferred_element_type=F32)


def _dot_nt(a, b):
    return lax.dot_general(a, b, (((1,), (1,)), ((), ())), preferred_element_type=F32)


def _bf16_pieces(x, parts):
    out = []
    r = x
    for i in range(parts):
        piece = r.astype(BF16)
        out.append(piece)
        if i + 1 < parts:
            r = r - piece.astype(F32)
    return out


def _ffn_kernel(x_ref, g_ref, w1_ref, w3_ref, w2_ref, o_ref):
    x = x_ref[...]
    h = _rms(x, g_ref[...]).astype(BF16)
    a = _dot(h, w1_ref[...])
    b = _dot(h, w3_ref[...])
    t = (jax.nn.silu(a) * b).astype(BF16)
    o_ref[...] = x + 0.5 * _dot(t, w2_ref[...])


def _resident_spec(shape):
    nd = len(shape)
    return pl.BlockSpec(shape, lambda *_: (0,) * nd, pipeline_mode=pl.Buffered(1))


def _ffn(x2, g, w1, w3, w2):
    n, d = x2.shape
    tm = FFN_ROWS
    return pl.pallas_call(
        _ffn_kernel,
        grid=(n // tm,),
        in_specs=[
            pl.BlockSpec((tm, d), lambda i: (i, 0)),
            _const_spec((1, d)),
            _resident_spec(w1.shape),
            _resident_spec(w3.shape),
            _resident_spec(w2.shape),
        ],
        out_specs=pl.BlockSpec((tm, d), lambda i: (i, 0)),
        out_shape=jax.ShapeDtypeStruct((n, d), F32),
        compiler_params=_cparams(("parallel",)),
        name="ffn_half_step",
    )(x2, g, w1, w3, w2)


def _group_rms(z, r_ref, gain):
    w = z.shape[-1]
    ss = _dot((z * z).astype(BF16), r_ref[0:w, 0:w])
    return z * lax.rsqrt(ss * (1.0 / 64.0) + EPS) * gain


def _proj_kernel(x_ref, g_ref, w_ref, wt_ref, r_ref, blk_ref, gq_ref, gk_ref, gnq_ref, gks_ref, gkw_ref,
                 dq_ref, dk_ref, nq_ref, kcvc_ref, kaug_ref, vta_ref, vtb_ref, gate_ref):
    h = _rms(x_ref[...], g_ref[...]).astype(BF16)
    z = _dot(h, w_ref[...])
    zt = _dot_nt(wt_ref[...], h)
    scale = DA_HEAD_DIM ** -0.5 * LOG2E
    dq_ref[...] = (_group_rms(z[:, 0:512], r_ref, gq_ref[...]) * scale).astype(BF16)
    dk_ref[...] = _group_rms(z[:, 512:1024], r_ref, gk_ref[...]).astype(BF16)
    low = lax.broadcasted_iota(jnp.int32, (z.shape[0], LANES), 1) < NSA_D

    def lane_blocks(v):
        out = []
        for c in range(v.shape[1] // LANES):
            pair = v[:, c * LANES:(c + 1) * LANES]
            out.append(jnp.where(low, pair, 0.0))
            out.append(jnp.where(low, pltpu.roll(pair, NSA_D, axis=1), 0.0))
        return out

    nq = _group_rms(z[:, 1024:1536], r_ref, gnq_ref[...]) * scale
    for hd, piece in enumerate(lane_blocks(nq)):
        nq_ref[:, hd * LANES:(hd + 1) * LANES] = piece.astype(BF16)
    for c, piece in enumerate(lane_blocks(z[:, 1536:1792])):
        kcvc_ref[:, c * LANES:(c + 1) * LANES] = piece
    blk = blk_ref[...].astype(F32)
    for g, piece in enumerate(lane_blocks(_group_rms(z[:, 1792:1920], r_ref, gks_ref[...]))):
        kaug_ref[0, 0, g] = (piece + blk).astype(BF16)
    for g, piece in enumerate(lane_blocks(_group_rms(z[:, 1920:2048], r_ref, gkw_ref[...]))):
        kaug_ref[0, 1, g] = piece.astype(BF16)
    ones = jnp.ones((ONES_ROWS, zt.shape[1]), BF16)
    for hd in range(DA_HEADS):
        vta_ref[0, hd * DA_VT_ROWS:hd * DA_VT_ROWS + DA_V_DIM] = zt[hd * DA_V_DIM:(hd + 1) * DA_V_DIM].astype(BF16)
        vta_ref[0, hd * DA_VT_ROWS + DA_V_DIM:(hd + 1) * DA_VT_ROWS] = ones
    for blk4 in range(2 * NSA_G):
        src = DA_WIDTH + (blk4 % 2) * NSA_KV + (blk4 // 2) * NSA_D
        vtb_ref[0, blk4 * NSA_VT_ROWS:blk4 * NSA_VT_ROWS + NSA_D] = zt[src:src + NSA_D].astype(BF16)
        vtb_ref[0, blk4 * NSA_VT_ROWS + NSA_D:(blk4 + 1) * NSA_VT_ROWS] = ones
    gate_ref[0] = jax.nn.sigmoid(zt[DA_WIDTH + 2 * NSA_KV:COL_ROWS])


def _proj(x2, g, w, wt, r, blk, gq, gk, gnq, gks, gkw, b, s):
    n, d = x2.shape
    tm = TILE
    nt = s // tm
    row = lambda wd: pl.BlockSpec((tm, wd), lambda i: (i, 0))
    col = lambda rows: pl.BlockSpec((1, rows, tm), lambda i: (i, 0, 0))
    return pl.pallas_call(
        _proj_kernel,
        grid=(n // tm,),
        in_specs=[row(d), _const_spec((1, d)), _const_spec(w.shape), _const_spec(wt.shape),
                  _const_spec(r.shape), pl.BlockSpec((tm, LANES), lambda i: (i % nt, 0)),
                  _const_spec((1, 512)), _const_spec((1, 512)), _const_spec((1, 512)),
                  _const_spec((1, NSA_KV)), _const_spec((1, NSA_KV))],
        out_specs=[row(512), row(512), row(NSA_HEADS * LANES), row(2 * NSA_G * LANES),
                   pl.BlockSpec((1, 2, NSA_G, tm, LANES), lambda i: (i // nt, 0, 0, i % nt, 0)),
                   col(DA_HEADS * DA_VT_ROWS), col(2 * NSA_G * NSA_VT_ROWS), col(NSA_G * GATE_ROWS)],
        out_shape=[
            jax.ShapeDtypeStruct((n, 512), BF16),
            jax.ShapeDtypeStruct((n, 512), BF16),
            jax.ShapeDtypeStruct((n, NSA_HEADS * LANES), BF16),
            jax.ShapeDtypeStruct((n, 2 * NSA_G * LANES), F32),
            jax.ShapeDtypeStruct((b, 2, NSA_G, s, LANES), BF16),
            jax.ShapeDtypeStruct((n // tm, DA_HEADS * DA_VT_ROWS, tm), BF16),
            jax.ShapeDtypeStruct((n // tm, 2 * NSA_G * NSA_VT_ROWS, tm), BF16),
            jax.ShapeDtypeStruct((n // tm, NSA_G * GATE_ROWS, tm), F32),
        ],
        compiler_params=_cparams(("parallel",)),
        name="norm_in_proj",
    )(x2, g, w, wt, r, blk, gq, gk, gnq, gks, gkw)


def _cmp_kernel(ck_ref, cv_ref, pek_ref, pev_ref, w1k_ref, w1v_ref, w2k_ref, w2vt_ref, gk_ref, kc_ref, vct_ref):
    nc = kc_ref.shape[2]

    def hidden(c_ref, pe_ref, w1_ref):
        lo = hi = None
        for t in range(CMP_STRIDE):
            x = c_ref[0, pl.ds(t, nc, stride=CMP_STRIDE), :]
            a = _dot((x + pe_ref[t:t + 1, :]).astype(BF16), w1_ref[t])
            b = _dot((x + pe_ref[CMP_STRIDE + t:CMP_STRIDE + t + 1, :]).astype(BF16), w1_ref[CMP_STRIDE + t])
            lo = a if lo is None else lo + a
            hi = b if hi is None else hi + b
        return jax.nn.gelu(lo + pltpu.roll(hi, nc - 1, axis=0)).astype(BF16)

    kc = _dot(hidden(ck_ref, pek_ref, w1k_ref), w2k_ref[...])
    ms = jnp.sum(kc * kc, axis=-1, keepdims=True) * (1.0 / NSA_D)
    kc_ref[0, 0] = (kc * lax.rsqrt(ms + EPS) * gk_ref[...]).astype(BF16)
    vct_ref[0, 0] = _dot_nt(w2vt_ref[...], hidden(cv_ref, pev_ref, w1v_ref)).astype(BF16)


def _compress(kcvc, pek, pev, w1k, w1v, w2k, w2vt, gk):
    b, s, _ = kcvc.shape
    g = NSA_G
    nc = s // CMP_STRIDE
    return pl.pallas_call(
        _cmp_kernel,
        grid=(b, g),
        in_specs=[pl.BlockSpec((1, s, LANES), lambda i, j: (i, 0, j)),
                  pl.BlockSpec((1, s, LANES), lambda i, j: (i, 0, NSA_G + j)),
                  _const_spec(pek.shape), _const_spec(pev.shape),
                  _const_spec(w1k.shape), _const_spec(w1v.shape),
                  _const_spec(w2k.shape), _const_spec(w2vt.shape), _const_spec(gk.shape)],
        out_specs=[pl.BlockSpec((1, 1, nc, LANES), lambda i, j: (i, j, 0, 0)),
                   pl.BlockSpec((1, 1, NSA_D, nc), lambda i, j: (i, j, 0, 0))],
        out_shape=[jax.ShapeDtypeStruct((b, g, nc, LANES), BF16),
                   jax.ShapeDtypeStruct((b, g, NSA_D, nc), BF16)],
        compiler_params=_cparams(("parallel", "parallel")),
        name="block_compress",
    )(kcvc, kcvc, pek, pev, w1k, w1v, w2k, w2vt, gk)


def _flash_list(n, qk_fn, vt_fn, bias_fn, state_fn, s_ref, mx_ref, p_ref, alpha_ref, bias_with_scores,
                before_loop=None):
    width = s_ref.shape[2]
    chunks = [slice(c * CW, (c + 1) * CW) for c in range(width // CW)]

    def scores(i, slot):
        raw = qk_fn(i)
        if not bias_with_scores:
            s_ref[slot] = raw
            return
        for cs in chunks:
            sb = raw[:, cs] + bias_fn(i, cs)
            s_ref[slot, :, cs] = sb
            mx_ref[slot, :, cs] = jnp.max(sb, axis=0, keepdims=True)

    def pv(i, slot):
        acc_ref = state_fn(i)[1]
        acc_ref[...] = alpha_ref[slot] * acc_ref[...] + _dot(vt_fn(i), p_ref[slot])

    def softmax(i, slot):
        m_ref = state_fn(i)[0]
        if bias_with_scores:
            m_old = m_ref[...]
            m_new = jnp.maximum(m_old, mx_ref[slot])
            alpha_ref[slot] = jnp.exp2(m_old - m_new)
            m_ref[...] = m_new
            for cs in chunks:
                p_ref[slot, :, cs] = jnp.exp2(s_ref[slot, :, cs] - m_new[:, cs]).astype(BF16)
            return
        for cs in chunks:
            s = s_ref[slot, :, cs] + bias_fn(i, cs)
            m_old = m_ref[:, cs]
            m_new = jnp.maximum(m_old, jnp.max(s, axis=0, keepdims=True))
            p_ref[slot, :, cs] = jnp.exp2(s - m_new).astype(BF16)
            alpha_ref[slot, :, cs] = jnp.exp2(m_old - m_new)
            m_ref[:, cs] = m_new

    scores(0, 0)
    softmax(0, 0)
    scores(1, 1)
    if before_loop is not None:
        before_loop()

    def body(g, carry):
        first = 1 + FLASH_UNROLL * g
        for u in range(FLASH_UNROLL):
            slot = (u + 1) % 2
            pv(first + u - 1, 1 - slot)
            softmax(first + u, slot)
            scores(first + u + 1, 1 - slot)
        return carry

    groups = (n - 1 + FLASH_UNROLL - 1) // FLASH_UNROLL
    lax.fori_loop(0, groups, body, 0)
    pv(FLASH_UNROLL * groups, 0)


def _to_rows(xt):
    t = xt.shape[1]
    return jnp.concatenate([xt[:, c * LANES:(c + 1) * LANES].T for c in range(t // LANES)], axis=0)


def _da_kernel(lam_init, nd, nqt, qt_of, kt_of, q_ref, k_ref, vt_ref, tab_ref, lamv_ref, gs_ref, o_ref,
               qq_ref, s_ref, mx_ref, p_ref, alpha_ref, m_ref, acc_ref):
    t = TILE
    lane = lax.broadcasted_iota(jnp.int32, (t, LANES), 1)
    for qt in range(nqt):
        q = q_ref[0, qt * t:(qt + 1) * t, :].astype(F32)
        qq_ref[qt, 0:t, :] = jnp.where(lane < DA_HEAD_DIM, q, 0.0).astype(BF16)
        qq_ref[qt, t:2 * t, :] = jnp.where(lane >= DA_HEAD_DIM, q, 0.0).astype(BF16)
    m_ref[...] = jnp.full(m_ref.shape, NEG, F32)
    acc_ref[...] = jnp.zeros(acc_ref.shape, F32)

    key_tile = lambda i: jnp.minimum(kt_of[i], nqt - 1)

    def qk(i):
        return _dot_nt(k_ref[0, pl.ds(pl.multiple_of(key_tile(i) * t, t), t), :], qq_ref[qt_of[i]])

    def bias(i, cs):
        d = jnp.clip(qt_of[i] - kt_of[i], -1, nd)
        return tab_ref[0, d + 1, :, slice(cs.start % t, cs.start % t + CW)]

    _flash_list(nqt * (nqt + 1) // 2, qk, lambda i: vt_ref[0, key_tile(i)], bias,
                lambda i: (m_ref.at[qt_of[i]], acc_ref.at[qt_of[i]]), s_ref, mx_ref, p_ref, alpha_ref, True)

    lv = lamv_ref[...]
    lam = (jnp.exp(jnp.sum(lv[0:1] * lv[1:2], axis=-1, keepdims=True))
           - jnp.exp(jnp.sum(lv[2:3] * lv[3:4], axis=-1, keepdims=True)) + lam_init)

    for qt in range(nqt):
        ot = acc_ref[qt, 0:DA_V_DIM, :] / acc_ref[qt, DA_V_DIM:DA_V_DIM + 1, :]
        y = _to_rows(ot[:, 0:t] - lam * ot[:, t:2 * t])
        o_ref[0, qt * t:(qt + 1) * t, :] = (_rms(y, gs_ref[...]) * (1.0 - lam_init)).astype(BF16)


def _diff_attention(dq, dk, vt, tab, lamv, gs, lam_init):
    b, s, _ = dq.shape
    t = TILE
    nqt = s // t
    nd = tab.shape[1] - 2
    pairs = [(qt, kt) for qt in range(nqt) for kt in range(qt + 1)] + [(nqt - 1, nqt)] * (FLASH_UNROLL + 2)
    qt_of = jnp.asarray(np.array([p[0] for p in pairs], np.int32))
    kt_of = jnp.asarray(np.array([p[1] for p in pairs], np.int32))
    seq = pl.BlockSpec((1, s, LANES), lambda i, h, *_: (i, 0, h))
    return pl.pallas_call(
        functools.partial(_da_kernel, lam_init, nd, nqt),
        grid_spec=pltpu.PrefetchScalarGridSpec(
            num_scalar_prefetch=2,
            grid=(b, DA_HEADS),
            in_specs=[
                seq, seq,
                pl.BlockSpec((1, nqt, DA_VT_ROWS, t), lambda i, h, *_: (i, 0, h, 0)),
                pl.BlockSpec((1,) + tab.shape[1:], lambda i, h, *_: (h, 0, 0, 0)),
                pl.BlockSpec(lamv.shape, lambda i, h, *_: (0, 0)),
                pl.BlockSpec(gs.shape, lambda i, h, *_: (0, 0)),
            ],
            out_specs=seq,
            scratch_shapes=[
                pltpu.VMEM((nqt, 2 * t, LANES), BF16),
                pltpu.VMEM((2, t, 2 * t), F32),
                pltpu.VMEM((2, 1, 2 * t), F32),
                pltpu.VMEM((2, t, 2 * t), BF16),
                pltpu.VMEM((2, 1, 2 * t), F32),
                pltpu.VMEM((nqt, 1, 2 * t), F32),
                pltpu.VMEM((nqt, DA_VT_ROWS, 2 * t), F32),
            ]),
        out_shape=jax.ShapeDtypeStruct((b, s, DA_WIDTH), BF16),
        compiler_params=_cparams(("parallel", "parallel")),
        name="diff_attention",
    )(qt_of, kt_of, dq, dk, vt, tab, lamv, gs)


def _nsa_kernel(nd, nqt, q_ref, kaug_ref, vt_ref, kc_ref, vct_ref, longc_ref, tab_ref, gate_ref,
                place_ref, ovlt_ref, o_ref,
                qz_ref, qs_ref, s_ref, p_ref, alpha_ref, m_ref, acc_ref, comb_ref, psum_ref, sc_ref, pc_ref):
    t = TILE
    halves = t // CW
    nchunk = NSA_P * halves
    qi = pl.program_id(2)
    win_tiles = WINDOW // t
    win_edge = nd + 2

    def gate_row(r):
        return jnp.concatenate([gate_ref[0, 3 * p + r:3 * p + r + 1, :] for p in range(NSA_P)], axis=1)

    for p in range(NSA_P):
        qz_ref[p * t:(p + 1) * t, :] = q_ref[0, :, p * LANES:(p + 1) * LANES]

    def select_blocks():
        nc = sc_ref.shape[0]
        cstart = pl.multiple_of((nqt - 1 - qi) * (t // CMP_STRIDE), t // CMP_STRIDE)
        sc_ref[...] = _dot_nt(kc_ref[0, 0], qz_ref[...])
        for c in range(nchunk):
            p, half = divmod(c, halves)
            cs = slice(c * CW, (c + 1) * CW)
            hs = slice(half * CW, (half + 1) * CW)
            s = sc_ref[:, cs] + longc_ref[0, pl.ds(cstart, nc), cs]
            m = jnp.max(s, axis=0, keepdims=True)
            e = jnp.exp2(s - m)
            pc = jnp.where(m > 0.5 * NEG, e / jnp.sum(e, axis=0, keepdims=True), 0.0)
            pc_ref[:, cs] = pc.astype(BF16)
            if p == 0:
                psum_ref[:, hs] = pc
            else:
                psum_ref[:, hs] += pc
        comb_ref[...] = gate_row(0) * _dot(vct_ref[0, 0], pc_ref[...])
        select_topk()

    def select_topk():
        ovlt = ovlt_ref[...]
        imp = None
        for piece in _bf16_pieces(psum_ref[...], 3):
            d = _dot(ovlt, piece)
            imp = d if imp is None else imp + d
        nblk = imp.shape[0]
        jrow = lax.broadcasted_iota(jnp.int32, (nblk, t), 0)
        cur = (qi * t + lax.broadcasted_iota(jnp.int32, (nblk, t), 1)) // SEL_BLOCK
        forced = (jrow == 0) | (jrow == cur) | (jrow == cur - 1)
        lowest = jnp.float32(-3e38)
        score = jnp.where(forced | (jrow > cur), lowest, imp)
        jrow_f = jrow.astype(F32)
        nsel = jnp.where(forced, 0.0, 1.0)
        for _ in range(SEL_TOPK - SEL_FORCED):
            mx = jnp.max(score, axis=0, keepdims=True)
            first = jnp.min(jnp.where(score == mx, jrow_f, 2.0 * LANES), axis=0, keepdims=True)
            hit = jrow_f == first
            nsel = jnp.where(hit, 0.0, nsel)
            score = jnp.where(hit, lowest, score)
        parts = [jnp.zeros((NSA_D, t), F32), nsel]
        if LANES - NSA_D - nblk:
            parts.append(jnp.zeros((LANES - NSA_D - nblk, t), F32))
        nsel_rows = _to_rows(jnp.concatenate(parts, axis=0))
        for p in range(NSA_P):
            rs = slice(p * t, (p + 1) * t)
            qs_ref[rs, :] = (qz_ref[rs, :].astype(F32) + nsel_rows).astype(BF16)

    m_ref[...] = jnp.full(m_ref.shape, NEG, F32)
    acc_ref[...] = jnp.zeros(acc_ref.shape, F32)
    nwin = win_tiles + 1

    def entry(i):
        is_win = jnp.asarray(i < nwin)
        kt = jnp.where(is_win, qi - win_tiles + i, i - nwin)
        return is_win, kt, qi - kt

    def qk(i):
        is_win, kt, _ = entry(i)
        rows = pl.ds(pl.multiple_of(jnp.clip(kt, 0, qi) * t, t), t)
        q_all = qz_ref if isinstance(i, int) and i < nwin else qs_ref
        return _dot_nt(kaug_ref[0, is_win.astype(jnp.int32), 0, rows, :], q_all[...])

    def vt(i):
        is_win, kt, _ = entry(i)
        rows = pl.ds(pl.multiple_of(is_win.astype(jnp.int32) * NSA_VT_ROWS, NSA_VT_ROWS), NSA_VT_ROWS)
        return vt_ref[0, jnp.clip(kt, 0, qi), rows, :]

    def bias(i, cs):
        is_win, kt, d = entry(i)
        idx = jnp.where(is_win & (d == win_tiles), win_edge, jnp.clip(d, -1, nd) + 1)
        return tab_ref[0, jnp.where(kt < 0, 0, idx), :, cs]

    def state(i):
        st = jnp.asarray(i < nwin).astype(jnp.int32)
        return m_ref.at[st], acc_ref.at[st]

    _flash_list(nwin + qi + 1, qk, vt, bias, state, s_ref, None, p_ref, alpha_ref, False,
                before_loop=select_blocks)

    def branch_out(st):
        return acc_ref[st, 0:NSA_D, :] / acc_ref[st, NSA_D:NSA_D + 1, :]

    comb = comb_ref[...] + gate_row(1) * branch_out(0) + gate_row(2) * branch_out(1)

    pad = jnp.zeros((LANES - NSA_D, t), F32)
    out = None
    for p in range(NSA_P):
        rows = _to_rows(jnp.concatenate([comb[:, p * t:(p + 1) * t], pad], axis=0)).astype(BF16)
        d = _dot(rows, place_ref[p])
        out = d if out is None else out + d
    o_ref[0] = out.astype(BF16)


def _nsa_attention(nq, kaug, vt, kc, vct, longc, tab, gates, place, ovlt):
    b, s, _ = nq.shape
    t = TILE
    nqt = s // t
    nc = kc.shape[2]
    nd = tab.shape[1] - 3
    gw = NSA_P * NSA_D
    return pl.pallas_call(
        functools.partial(_nsa_kernel, nd, nqt),
        grid=(b, NSA_G, nqt),
        in_specs=[
            pl.BlockSpec((1, t, NSA_P * LANES), lambda i, g, j: (i, j, g)),
            pl.BlockSpec((1, 2, 1, s, LANES), lambda i, g, j: (i, 0, g, 0, 0)),
            pl.BlockSpec((1, nqt, 2 * NSA_VT_ROWS, t), lambda i, g, j: (i, 0, g, 0)),
            pl.BlockSpec((1, 1, nc, LANES), lambda i, g, j: (i, g, 0, 0)),
            pl.BlockSpec((1, 1, NSA_D, nc), lambda i, g, j: (i, g, 0, 0)),
            pl.BlockSpec((1,) + longc.shape[1:], lambda i, g, j: (g, 0, 0)),
            pl.BlockSpec((1,) + tab.shape[1:], lambda i, g, j: (g, 0, 0, 0)),
            pl.BlockSpec((1, GATE_ROWS, t), lambda i, g, j: (i * nqt + j, g, 0)),
            _const_spec(place.shape), _const_spec(ovlt.shape),
        ],
        out_specs=pl.BlockSpec((1, t, gw), lambda i, g, j: (i, j, g)),
        out_shape=jax.ShapeDtypeStruct((b, s, NSA_WIDTH), BF16),
        scratch_shapes=[
            pltpu.VMEM((NSA_P * t, LANES), BF16),
            pltpu.VMEM((NSA_P * t, LANES), BF16),
            pltpu.VMEM((2, t, NSA_P * t), F32),
            pltpu.VMEM((2, t, NSA_P * t), BF16),
            pltpu.VMEM((2, 1, NSA_P * t), F32),
            pltpu.VMEM((2, 1, NSA_P * t), F32),
            pltpu.VMEM((2, NSA_VT_ROWS, NSA_P * t), F32),
            pltpu.VMEM((NSA_D, NSA_P * t), F32),
            pltpu.VMEM((nc, t), F32),
            pltpu.VMEM((nc, NSA_P * t), F32),
            pltpu.VMEM((nc, NSA_P * t), BF16),
        ],
        compiler_params=_cparams(("parallel", "parallel", "arbitrary")),
        name="nsa_attention",
    )(nq, kaug, vt, kc, vct, longc, tab, gates, place, ovlt)


def _merge_kernel(x_ref, ya_ref, yb_ref, g_ref, wmg_ref, pa_ref, pb_ref, wo_ref, o_ref):
    x = x_ref[...]
    h = _rms(x, g_ref[...]).astype(BF16)
    mg = jax.nn.sigmoid(_dot(h, wmg_ref[...]))
    merged = mg[:, 0:D_MODEL] * _dot(ya_ref[...], pa_ref[...]) + mg[:, D_MODEL:] * _dot(yb_ref[...], pb_ref[...])
    o_ref[...] = x + _dot(merged.astype(BF16), wo_ref[...])


def _merge(x2, ya, yb, g, wmg, pa, pb, wo):
    n, d = x2.shape
    tm = TILE
    row = lambda wd: pl.BlockSpec((tm, wd), lambda i: (i, 0))
    return pl.pallas_call(
        _merge_kernel,
        grid=(n // tm,),
        in_specs=[row(d), row(DA_WIDTH), row(NSA_WIDTH), _const_spec((1, d)),
                  _const_spec(wmg.shape), _const_spec(pa.shape), _const_spec(pb.shape), _const_spec(wo.shape)],
        out_specs=row(d),
        out_shape=jax.ShapeDtypeStruct((n, d), F32),
        compiler_params=_cparams(("parallel",)),
        name="gated_merge_out_proj",
    )(x2, ya, yb, g, wmg, pa, pb, wo)


def _rel_bucket(dist):
    n = jnp.maximum(dist, 0)
    max_exact = N_BUCKETS // 2
    nf = jnp.maximum(n, 1).astype(jnp.float32)
    large = max_exact + (jnp.log(nf / max_exact) / math.log(MAX_DISTANCE / max_exact)
                         * (N_BUCKETS - max_exact)).astype(jnp.int32)
    large = jnp.minimum(large, N_BUCKETS - 1)
    return jnp.where(n < max_exact, n, large)


def _bias_of(table, dist):
    bucket = _rel_bucket(dist)
    col = lambda v: v.reshape((-1,) + (1,) * dist.ndim)
    out = jnp.zeros((table.shape[1],) + dist.shape, F32)
    for bkt in range(N_BUCKETS):
        out = jnp.where(bucket == bkt, col(table[bkt].astype(F32)), out)
    return jnp.where(dist >= 0, out, NEG)


def _far_tiles(t):
    return -(-(MAX_DISTANCE + t - 1) // t)


def _tile_bias(table, t, window_edge):
    nd = _far_tiles(t)
    d0 = 2 * t - 1
    width = (nd + 3) * t - 1
    v = _bias_of(table, jnp.arange(width) - d0)
    w = jnp.concatenate([v, jnp.zeros((v.shape[0], 1), F32)], axis=1)
    m = jnp.tile(w, (1, t))[:, :t * width].reshape(-1, t, width)
    bias = jnp.stack([m[:, :, d * t + d0:d * t + d0 + t] for d in range(-1, nd + 1)], axis=1)
    if window_edge:
        we = WINDOW // t
        j = jnp.arange(t)[:, None]
        i = jnp.arange(t)[None, :]
        edge = jnp.where(we * t + i - j < WINDOW, bias[:, we + 1], NEG)
        bias = jnp.concatenate([bias, edge[:, None]], axis=1)
    return bias


def _cmp_bias(table, t, nqt):
    cpt = t // CMP_STRIDE
    r = jnp.arange((2 * nqt - 1) * cpt)[:, None]
    dist = (nqt - 1 - r // cpt) * t + jnp.arange(t)[None, :] - ((r % cpt) * CMP_STRIDE + CMP_BLOCK - 1)
    return _bias_of(table, dist)


def _lanes_by_head(x):
    x = x.reshape((NSA_G, NSA_P) + x.shape[1:])
    x = jnp.moveaxis(x, 1, -2)
    return x.reshape(x.shape[:-2] + (NSA_P * x.shape[-1],))


def _static_tables(s):
    nc = s // CMP_STRIDE
    n_cmp = nc - CMP_BLOCK // CMP_STRIDE + 1
    nblk = s // SEL_BLOCK
    place = np.zeros((NSA_P, LANES, NSA_P * NSA_D), np.float32)
    for p in range(NSA_P):
        for dd in range(NSA_D):
            place[p, dd, p * NSA_D + dd] = 1.0
    ovlt = np.zeros((nblk, nc), np.float32)
    for c in range(n_cmp):
        for jb in range(nblk):
            if c * CMP_STRIDE < (jb + 1) * SEL_BLOCK and c * CMP_STRIDE + CMP_BLOCK - 1 >= jb * SEL_BLOCK:
                ovlt[jb, c] = 1.0
    blk = np.zeros((s, LANES), np.float32)
    blk[np.arange(s), NSA_D + np.arange(s) // SEL_BLOCK] = NEG
    r = np.kron(np.eye(512 // 64, dtype=np.float32), np.ones((64, 64), np.float32))
    return jnp.asarray(place, BF16), jnp.asarray(ovlt, BF16), jnp.asarray(blk, BF16), jnp.asarray(r, BF16)


def _pad_blocks(w, axis=-1):
    w = jnp.moveaxis(w, axis, -1)
    w = w.reshape(w.shape[:-1] + (w.shape[-1] // NSA_D, NSA_D))
    w = jnp.pad(w, [(0, 0)] * (w.ndim - 1) + [(0, LANES - NSA_D)])
    return jnp.moveaxis(w.reshape(w.shape[:-2] + (-1,)), -1, axis)


def kernel(x, w_in, w_branch_a, w_branch_b, w_out, norm_ffn1, norm_mix, norm_ffn2, ffn1_w1, ffn1_w3, ffn1_w2, ffn2_w1, ffn2_w3, ffn2_w2, da_q_gain, da_k_gain, da_lambda_q1, da_lambda_k1, da_lambda_q2, da_lambda_k2, da_subln_gain, nsa_q_gain, nsa_k_gain, cmp_pe_k, cmp_w1_k, cmp_w2_k, cmp_pe_v, cmp_w1_v, cmp_w2_v, rel_bias_table):
    b, s, d = x.shape
    depth = w_in.shape[0]
    t = TILE
    assert d == D_MODEL and s % t == 0 and WINDOW % t == 0 and s >= WINDOW and (b * s) % FFN_ROWS == 0
    assert SEL_TOPK <= s // SEL_BLOCK <= LANES - NSA_D
    n = b * s
    nqt = s // t

    place, ovlt, blk, r512 = _static_tables(s)
    tab_a = _tile_bias(rel_bias_table[:, :DA_HEADS], t, False) * LOG2E
    tab_b = _lanes_by_head(_tile_bias(rel_bias_table[:, DA_HEADS:], t, True)) * LOG2E
    longc = _lanes_by_head(_cmp_bias(rel_bias_table[:, DA_HEADS:], t, nqt)) * LOG2E

    bf = lambda w: w.astype(BF16)
    f1w1, f1w3, f1w2, f2w1, f2w3, f2w2 = map(bf, (ffn1_w1, ffn1_w3, ffn1_w2, ffn2_w1, ffn2_w3, ffn2_w2))
    wba, wbb, wo = bf(w_branch_a), bf(w_branch_b), bf(w_out)
    w_row = bf(jnp.concatenate([w_in[:, :, 0:1024], w_in[:, :, 1536:2304], w_in[:, :, 2304:2432], w_in[:, :, 2560:2688]],
                               axis=2))
    w_ng = jnp.pad(w_in[:, :, 2816:2840].reshape(depth, d, NSA_G, NSA_P * 3),
                   ((0, 0), (0, 0), (0, 0), (0, GATE_ROWS - NSA_P * 3))).reshape(depth, d, NSA_G * GATE_ROWS)
    w_col = bf(jnp.swapaxes(jnp.concatenate([w_in[:, :, 1024:1536], w_in[:, :, 2432:2560], w_in[:, :, 2688:2816], w_ng],
                                            axis=2), 1, 2))
    w_mg = bf(w_in[:, :, 2840:])
    cw1 = lambda w: bf(_pad_blocks(w.reshape(depth, CMP_BLOCK, NSA_D, CMP_HIDDEN), axis=2))
    cw1k, cw1v = cw1(cmp_w1_k), cw1(cmp_w1_v)
    pek, pev = _pad_blocks(cmp_pe_k.astype(F32)), _pad_blocks(cmp_pe_v.astype(F32))
    cw2k = bf(_pad_blocks(cmp_w2_k))
    cw2vt = bf(jnp.swapaxes(cmp_w2_v, 1, 2))

    x2 = x.reshape(n, d)
    row = lambda v: v.reshape(1, -1).astype(F32)
    tile = lambda v, k: jnp.tile(v.astype(F32), k).reshape(1, -1)
    for l in range(depth):
        x2 = _ffn(x2, row(norm_ffn1[l]), f1w1[l], f1w3[l], f1w2[l])

        dq, dk, nq, kcvc, kaug, vta, vtb, gates = _proj(
            x2, row(norm_mix[l]), w_row[l], w_col[l], r512, blk,
            tile(da_q_gain[l], 8), tile(da_k_gain[l], 8), tile(nsa_q_gain[l], 8),
            tile(nsa_k_gain[l, 1], NSA_G), tile(nsa_k_gain[l, 2], NSA_G), b, s)
        vta = vta.reshape(b, nqt, DA_HEADS * DA_VT_ROWS, t)
        vtb = vtb.reshape(b, nqt, 2 * NSA_G * NSA_VT_ROWS, t)

        lam_init = 0.8 - 0.6 * math.exp(-0.3 * l)
        lamv = jnp.stack([da_lambda_q1[l], da_lambda_k1[l], da_lambda_q2[l], da_lambda_k2[l]]).astype(F32)
        ya = _diff_attention(dq.reshape(b, s, 512), dk.reshape(b, s, 512), vta,
                             tab_a, lamv, row(da_subln_gain[l]), lam_init)

        kc, vct = _compress(kcvc.reshape(b, s, 2 * NSA_G * LANES), pek[l], pev[l], cw1k[l], cw1v[l], cw2k[l], cw2vt[l],
                            _pad_blocks(row(nsa_k_gain[l, 0])))
        yb = _nsa_attention(nq.reshape(b, s, NSA_HEADS * LANES), kaug, vtb, kc, vct, longc, tab_b, gates, place, ovlt)

        x2 = _merge(x2, ya.reshape(n, DA_WIDTH), yb.reshape(n, NSA_WIDTH), row(norm_mix[l]),
                    w_mg[l], wba[l], wbb[l], wo[l])
        x2 = _ffn(x2, row(norm_ffn2[l]), f2w1[l], f2w3[l], f2w2[l])
    return x2.reshape(b, s, d)
```

```python
import functools
import math

import numpy as np
import jax
import jax.numpy as jnp
from jax import lax
from jax.experimental import pallas as pl
from jax.experimental.pallas import tpu as pltpu

F32 = jnp.float32
BF16 = jnp.bfloat16

D_MODEL = 1024
DA_HEADS = 4
DA_HEAD_DIM = 64
DA_V_DIM = 2 * DA_HEAD_DIM
DA_WIDTH = DA_HEADS * DA_V_DIM
NSA_HEADS = 8
NSA_G = 2
NSA_P = NSA_HEADS // NSA_G
NSA_D = 64
NSA_WIDTH = NSA_HEADS * NSA_D
NSA_KV = NSA_G * NSA_D
CMP_BLOCK = 32
CMP_STRIDE = 16
CMP_HIDDEN = 128
SEL_BLOCK = 64
SEL_TOPK = 8
SEL_FORCED = 3
WINDOW = 512
N_BUCKETS = 32
MAX_DISTANCE = 1024
EPS = 1e-6
NEG = -1e30

LANES = 128
VMEM_LIMIT = 52 * 1024 * 1024

TILE = 256
FFN_ROWS = 512
CW = LANES
FLASH_UNROLL = 4
GATE_ROWS = 16
ONES_ROWS = 16
DA_VT_ROWS = DA_V_DIM + ONES_ROWS
NSA_VT_ROWS = NSA_D + ONES_ROWS
LOG2E = math.log2(math.e)

COL_ROWS = DA_WIDTH + 2 * NSA_KV + NSA_G * GATE_ROWS


def _cparams(sem):
    return pltpu.CompilerParams(dimension_semantics=sem, vmem_limit_bytes=VMEM_LIMIT)


def _const_spec(shape):
    nd = len(shape)
    return pl.BlockSpec(shape, lambda *_: (0,) * nd)


def _rms(xf, g):
    ms = jnp.mean(xf * xf, axis=-1, keepdims=True)
    return xf * lax.rsqrt(ms + EPS) * g


def _dot(a, b):
    return jnp.dot(a, b, preferred_element_type=F32)


def _dot_nt(a, b):
    return lax.dot_general(a, b, (((1,), (1,)), ((), ())), preferred_element_type=F32)


def _bf16_pieces(x, parts):
    out = []
    r = x
    for i in range(parts):
        piece = r.astype(BF16)
        out.append(piece)
        if i + 1 < parts:
            r = r - piece.astype(F32)
    return out


def _ffn_kernel(x_ref, g_ref, w1_ref, w3_ref, w2_ref, o_ref):
    x = x_ref[...]
    h = _rms(x, g_ref[...]).astype(BF16)
    a = _dot(h, w1_ref[...])
    b = _dot(h, w3_ref[...])
    t = (jax.nn.silu(a) * b).astype(BF16)
    o_ref[...] = x + 0.5 * _dot(t, w2_ref[...])


def _resident_spec(shape):
    nd = len(shape)
    return pl.BlockSpec(shape, lambda *_: (0,) * nd, pipeline_mode=pl.Buffered(1))


def _ffn(x2, g, w1, w3, w2):
    n, d = x2.shape
    tm = FFN_ROWS
    return pl.pallas_call(
        _ffn_kernel,
        grid=(n // tm,),
        in_specs=[
            pl.BlockSpec((tm, d), lambda i: (i, 0)),
            _const_spec((1, d)),
            _resident_spec(w1.shape),
            _resident_spec(w3.shape),
            _resident_spec(w2.shape),
        ],
        out_specs=pl.BlockSpec((tm, d), lambda i: (i, 0)),
        out_shape=jax.ShapeDtypeStruct((n, d), F32),
        compiler_params=_cparams(("parallel",)),
        name="ffn_half_step",
    )(x2, g, w1, w3, w2)


def _group_rms(z, r_ref, gain):
    w = z.shape[-1]
    ss = _dot((z * z).astype(BF16), r_ref[0:w, 0:w])
    return z * lax.rsqrt(ss * (1.0 / 64.0) + EPS) * gain


def _proj_kernel(x_ref, g_ref, w_ref, wt_ref, r_ref, blk_ref, gq_ref, gk_ref, gnq_ref, gks_ref, gkw_ref,
                 dq_ref, dk_ref, nq_ref, kcvc_ref, kaug_ref, vta_ref, vtb_ref, gate_ref):
    h = _rms(x_ref[...], g_ref[...]).astype(BF16)
    z = _dot(h, w_ref[...])
    zt = _dot_nt(wt_ref[...], h)
    scale = DA_HEAD_DIM ** -0.5 * LOG2E
    dq_ref[...] = (_group_rms(z[:, 0:512], r_ref, gq_ref[...]) * scale).astype(BF16)
    dk_ref[...] = _group_rms(z[:, 512:1024], r_ref, gk_ref[...]).astype(BF16)
    low = lax.broadcasted_iota(jnp.int32, (z.shape[0], LANES), 1) < NSA_D

    def lane_blocks(v):
        out = []
        for c in range(v.shape[1] // LANES):
            pair = v[:, c * LANES:(c + 1) * LANES]
            out.append(jnp.where(low, pair, 0.0))
            out.append(jnp.where(low, pltpu.roll(pair, NSA_D, axis=1), 0.0))
        return out

    nq = _group_rms(z[:, 1024:1536], r_ref, gnq_ref[...]) * scale
    for hd, piece in enumerate(lane_blocks(nq)):
        nq_ref[:, hd * LANES:(hd + 1) * LANES] = piece.astype(BF16)
    for c, piece in enumerate(lane_blocks(z[:, 1536:1792])):
        kcvc_ref[:, c * LANES:(c + 1) * LANES] = piece
    blk = blk_ref[...].astype(F32)
    for g, piece in enumerate(lane_blocks(_group_rms(z[:, 1792:1920], r_ref, gks_ref[...]))):
        kaug_ref[0, 0, g] = (piece + blk).astype(BF16)
    for g, piece in enumerate(lane_blocks(_group_rms(z[:, 1920:2048], r_ref, gkw_ref[...]))):
        kaug_ref[0, 1, g] = piece.astype(BF16)
    ones = jnp.ones((ONES_ROWS, zt.shape[1]), BF16)
    for hd in range(DA_HEADS):
        vta_ref[0, hd * DA_VT_ROWS:hd * DA_VT_ROWS + DA_V_DIM] = zt[hd * DA_V_DIM:(hd + 1) * DA_V_DIM].astype(BF16)
        vta_ref[0, hd * DA_VT_ROWS + DA_V_DIM:(hd + 1) * DA_VT_ROWS] = ones
    for blk4 in range(2 * NSA_G):
        src = DA_WIDTH + (blk4 % 2) * NSA_KV + (blk4 // 2) * NSA_D
        vtb_ref[0, blk4 * NSA_VT_ROWS:blk4 * NSA_VT_ROWS + NSA_D] = zt[src:src + NSA_D].astype(BF16)
        vtb_ref[0, blk4 * NSA_VT_ROWS + NSA_D:(blk4 + 1) * NSA_VT_ROWS] = ones
    gate_ref[0] = jax.nn.sigmoid(zt[DA_WIDTH + 2 * NSA_KV:COL_ROWS])


def _proj(x2, g, w, wt, r, blk, gq, gk, gnq, gks, gkw, b, s):
    n, d = x2.shape
    tm = TILE
    nt = s // tm
    row = lambda wd: pl.BlockSpec((tm, wd), lambda i: (i, 0))
    col = lambda rows: pl.BlockSpec((1, rows, tm), lambda i: (i, 0, 0))
    return pl.pallas_call(
        _proj_kernel,
        grid=(n // tm,),
        in_specs=[row(d), _const_spec((1, d)), _const_spec(w.shape), _const_spec(wt.shape),
                  _const_spec(r.shape), pl.BlockSpec((tm, LANES), lambda i: (i % nt, 0)),
                  _const_spec((1, 512)), _const_spec((1, 512)), _const_spec((1, 512)),
                  _const_spec((1, NSA_KV)), _const_spec((1, NSA_KV))],
        out_specs=[row(512), row(512), row(NSA_HEADS * LANES), row(2 * NSA_G * LANES),
                   pl.BlockSpec((1, 2, NSA_G, tm, LANES), lambda i: (i // nt, 0, 0, i % nt, 0)),
                   col(DA_HEADS * DA_VT_ROWS), col(2 * NSA_G * NSA_VT_ROWS), col(NSA_G * GATE_ROWS)],
        out_shape=[
            jax.ShapeDtypeStruct((n, 512), BF16),
            jax.ShapeDtypeStruct((n, 512), BF16),
            jax.ShapeDtypeStruct((n, NSA_HEADS * LANES), BF16),
            jax.ShapeDtypeStruct((n, 2 * NSA_G * LANES), F32),
            jax.ShapeDtypeStruct((b, 2, NSA_G, s, LANES), BF16),
            jax.ShapeDtypeStruct((n // tm, DA_HEADS * DA_VT_ROWS, tm), BF16),
            jax.ShapeDtypeStruct((n // tm, 2 * NSA_G * NSA_VT_ROWS, tm), BF16),
            jax.ShapeDtypeStruct((n // tm, NSA_G * GATE_ROWS, tm), F32),
        ],
        compiler_params=_cparams(("parallel",)),
        name="norm_in_proj",
    )(x2, g, w, wt, r, blk, gq, gk, gnq, gks, gkw)


def _cmp_kernel(ck_ref, cv_ref, pek_ref, pev_ref, w1k_ref, w1v_ref, w2k_ref, w2vt_ref, gk_ref, kc_ref, vct_ref):
    nc = kc_ref.shape[2]

    def hidden(c_ref, pe_ref, w1_ref):
        lo = hi = None
        for t in range(CMP_STRIDE):
            x = c_ref[0, pl.ds(t, nc, stride=CMP_STRIDE), :]
            a = _dot((x + pe_ref[t:t + 1, :]).astype(BF16), w1_ref[t])
            b = _dot((x + pe_ref[CMP_STRIDE + t:CMP_STRIDE + t + 1, :]).astype(BF16), w1_ref[CMP_STRIDE + t])
            lo = a if lo is None else lo + a
            hi = b if hi is None else hi + b
        return jax.nn.gelu(lo + pltpu.roll(hi, nc - 1, axis=0)).astype(BF16)

    kc = _dot(hidden(ck_ref, pek_ref, w1k_ref), w2k_ref[...])
    ms = jnp.sum(kc * kc, axis=-1, keepdims=True) * (1.0 / NSA_D)
    kc_ref[0, 0] = (kc * lax.rsqrt(ms + EPS) * gk_ref[...]).astype(BF16)
    vct_ref[0, 0] = _dot_nt(w2vt_ref[...], hidden(cv_ref, pev_ref, w1v_ref)).astype(BF16)


def _compress(kcvc, pek, pev, w1k, w1v, w2k, w2vt, gk):
    b, s, _ = kcvc.shape
    g = NSA_G
    nc = s // CMP_STRIDE
    return pl.pallas_call(
        _cmp_kernel,
        grid=(b, g),
        in_specs=[pl.BlockSpec((1, s, LANES), lambda i, j: (i, 0, j)),
                  pl.BlockSpec((1, s, LANES), lambda i, j: (i, 0, NSA_G + j)),
                  _const_spec(pek.shape), _const_spec(pev.shape),
                  _const_spec(w1k.shape), _const_spec(w1v.shape),
                  _const_spec(w2k.shape), _const_spec(w2vt.shape), _const_spec(gk.shape)],
        out_specs=[pl.BlockSpec((1, 1, nc, LANES), lambda i, j: (i, j, 0, 0)),
                   pl.BlockSpec((1, 1, NSA_D, nc), lambda i, j: (i, j, 0, 0))],
        out_shape=[jax.ShapeDtypeStruct((b, g, nc, LANES), BF16),
                   jax.ShapeDtypeStruct((b, g, NSA_D, nc), BF16)],
        compiler_params=_cparams(("parallel", "parallel")),
        name="block_compress",
    )(kcvc, kcvc, pek, pev, w1k, w1v, w2k, w2vt, gk)


def _flash_list(n, qk_fn, vt_fn, bias_fn, state_fn, s_ref, mx_ref, p_ref, alpha_ref, bias_with_scores,
                before_loop=None):
    width = s_ref.shape[2]
    chunks = [slice(c * CW, (c + 1) * CW) for c in range(width // CW)]

    def scores(i, slot):
        raw = qk_fn(i)
        if not bias_with_scores:
            s_ref[slot] = raw
            return
        for cs in chunks:
            sb = raw[:, cs] + bias_fn(i, cs)
            s_ref[slot, :, cs] = sb
            mx_ref[slot, :, cs] = jnp.max(sb, axis=0, keepdims=True)

    def pv(i, slot):
        acc_ref = state_fn(i)[1]
        acc_ref[...] = alpha_ref[slot] * acc_ref[...] + _dot(vt_fn(i), p_ref[slot])

    def softmax(i, slot):
        m_ref = state_fn(i)[0]
        if bias_with_scores:
            m_old = m_ref[...]
            m_new = jnp.maximum(m_old, mx_ref[slot])
            alpha_ref[slot] = jnp.exp2(m_old - m_new)
            m_ref[...] = m_new
            for cs in chunks:
                p_ref[slot, :, cs] = jnp.exp2(s_ref[slot, :, cs] - m_new[:, cs]).astype(BF16)
            return
        for cs in chunks:
            s = s_ref[slot, :, cs] + bias_fn(i, cs)
            m_old = m_ref[:, cs]
            m_new = jnp.maximum(m_old, jnp.max(s, axis=0, keepdims=True))
            p_ref[slot, :, cs] = jnp.exp2(s - m_new).astype(BF16)
            alpha_ref[slot, :, cs] = jnp.exp2(m_old - m_new)
            m_ref[:, cs] = m_new

    scores(0, 0)
    softmax(0, 0)
    scores(1, 1)
    if before_loop is not None:
        before_loop()

    def body(g, carry):
        first = 1 + FLASH_UNROLL * g
        for u in range(FLASH_UNROLL):
            slot = (u + 1) % 2
            pv(first + u - 1, 1 - slot)
            softmax(first + u, slot)
            scores(first + u + 1, 1 - slot)
        return carry

    groups = (n - 1 + FLASH_UNROLL - 1) // FLASH_UNROLL
    lax.fori_loop(0, groups, body, 0)
    pv(FLASH_UNROLL * groups, 0)


def _to_rows(xt):
    t = xt.shape[1]
    return jnp.concatenate([xt[:, c * LANES:(c + 1) * LANES].T for c in range(t // LANES)], axis=0)


def _da_kernel(lam_init, nd, nqt, qt_of, kt_of, q_ref, k_ref, vt_ref, tab_ref, lamv_ref, gs_ref, o_ref,
               qq_ref, s_ref, mx_ref, p_ref, alpha_ref, m_ref, acc_ref):
    t = TILE
    lane = lax.broadcasted_iota(jnp.int32, (t, LANES), 1)
    for qt in range(nqt):
        q = q_ref[0, qt * t:(qt + 1) * t, :].astype(F32)
        qq_ref[qt, 0:t, :] = jnp.where(lane < DA_HEAD_DIM, q, 0.0).astype(BF16)
        qq_ref[qt, t:2 * t, :] = jnp.where(lane >= DA_HEAD_DIM, q, 0.0).astype(BF16)
    m_ref[...] = jnp.full(m_ref.shape, NEG, F32)
    acc_ref[...] = jnp.zeros(acc_ref.shape, F32)

    key_tile = lambda i: jnp.minimum(kt_of[i], nqt - 1)

    def qk(i):
        return _dot_nt(k_ref[0, pl.ds(pl.multiple_of(key_tile(i) * t, t), t), :], qq_ref[qt_of[i]])

    def bias(i, cs):
        d = jnp.clip(qt_of[i] - kt_of[i], -1, nd)
        return tab_ref[0, d + 1, :, slice(cs.start % t, cs.start % t + CW)]

    _flash_list(nqt * (nqt + 1) // 2, qk, lambda i: vt_ref[0, key_tile(i)], bias,
                lambda i: (m_ref.at[qt_of[i]], acc_ref.at[qt_of[i]]), s_ref, mx_ref, p_ref, alpha_ref, True)

    lv = lamv_ref[...]
    lam = (jnp.exp(jnp.sum(lv[0:1] * lv[1:2], axis=-1, keepdims=True))
           - jnp.exp(jnp.sum(lv[2:3] * lv[3:4], axis=-1, keepdims=True)) + lam_init)

    for qt in range(nqt):
        ot = acc_ref[qt, 0:DA_V_DIM, :] / acc_ref[qt, DA_V_DIM:DA_V_DIM + 1, :]
        y = _to_rows(ot[:, 0:t] - lam * ot[:, t:2 * t])
        o_ref[0, qt * t:(qt + 1) * t, :] = (_rms(y, gs_ref[...]) * (1.0 - lam_init)).astype(BF16)


def _diff_attention(dq, dk, vt, tab, lamv, gs, lam_init):
    b, s, _ = dq.shape
    t = TILE
    nqt = s // t
    nd = tab.shape[1] - 2
    pairs = [(qt, kt) for qt in range(nqt) for kt in range(qt + 1)] + [(nqt - 1, nqt)] * (FLASH_UNROLL + 2)
    qt_of = jnp.asarray(np.array([p[0] for p in pairs], np.int32))
    kt_of = jnp.asarray(np.array([p[1] for p in pairs], np.int32))
    seq = pl.BlockSpec((1, s, LANES), lambda i, h, *_: (i, 0, h))
    return pl.pallas_call(
        functools.partial(_da_kernel, lam_init, nd, nqt),
        grid_spec=pltpu.PrefetchScalarGridSpec(
            num_scalar_prefetch=2,
            grid=(b, DA_HEADS),
            in_specs=[
                seq, seq,
                pl.BlockSpec((1, nqt, DA_VT_ROWS, t), lambda i, h, *_: (i, 0, h, 0)),
                pl.BlockSpec((1,) + tab.shape[1:], lambda i, h, *_: (h, 0, 0, 0)),
                pl.BlockSpec(lamv.shape, lambda i, h, *_: (0, 0)),
                pl.BlockSpec(gs.shape, lambda i, h, *_: (0, 0)),
            ],
            out_specs=seq,
            scratch_shapes=[
                pltpu.VMEM((nqt, 2 * t, LANES), BF16),
                pltpu.VMEM((2, t, 2 * t), F32),
                pltpu.VMEM((2, 1, 2 * t), F32),
                pltpu.VMEM((2, t, 2 * t), BF16),
                pltpu.VMEM((2, 1, 2 * t), F32),
                pltpu.VMEM((nqt, 1, 2 * t), F32),
                pltpu.VMEM((nqt, DA_VT_ROWS, 2 * t), F32),
            ]),
        out_shape=jax.ShapeDtypeStruct((b, s, DA_WIDTH), BF16),
        compiler_params=_cparams(("parallel", "parallel")),
        name="diff_attention",
    )(qt_of, kt_of, dq, dk, vt, tab, lamv, gs)


def _nsa_kernel(nd, nqt, q_ref, kaug_ref, vt_ref, kc_ref, vct_ref, longc_ref, tab_ref, gate_ref,
                ovlt_ref, o_ref,
                qz_ref, qs_ref, s_ref, p_ref, alpha_ref, m_ref, acc_ref, comb_ref, psum_ref, sc_ref, pc_ref):
    t = TILE
    halves = t // CW
    nchunk = NSA_P * halves
    qi = pl.program_id(2)
    win_tiles = WINDOW // t
    win_edge = nd + 2

    def gate_row(r):
        return jnp.concatenate([gate_ref[0, 3 * p + r:3 * p + r + 1, :] for p in range(NSA_P)], axis=1)

    for p in range(NSA_P):
        qz_ref[p * t:(p + 1) * t, :] = q_ref[0, :, p * LANES:(p + 1) * LANES]

    def select_blocks():
        nc = sc_ref.shape[0]
        cstart = pl.multiple_of((nqt - 1 - qi) * (t // CMP_STRIDE), t // CMP_STRIDE)
        sc_ref[...] = _dot_nt(kc_ref[0, 0], qz_ref[...])
        for c in range(nchunk):
            p, half = divmod(c, halves)
            cs = slice(c * CW, (c + 1) * CW)
            hs = slice(half * CW, (half + 1) * CW)
            s = sc_ref[:, cs] + longc_ref[0, pl.ds(cstart, nc), cs]
            m = jnp.max(s, axis=0, keepdims=True)
            e = jnp.exp2(s - m)
            pc = jnp.where(m > 0.5 * NEG, e / jnp.sum(e, axis=0, keepdims=True), 0.0)
            pc_ref[:, cs] = pc.astype(BF16)
            if p == 0:
                psum_ref[:, hs] = pc
            else:
                psum_ref[:, hs] += pc
        comb_ref[...] = gate_row(0) * _dot(vct_ref[0, 0], pc_ref[...])
        select_topk()

    def select_topk():
        ovlt = ovlt_ref[...]
        imp = None
        for piece in _bf16_pieces(psum_ref[...], 3):
            d = _dot(ovlt, piece)
            imp = d if imp is None else imp + d
        nblk = imp.shape[0]
        jrow = lax.broadcasted_iota(jnp.int32, (nblk, t), 0)
        cur = (qi * t + lax.broadcasted_iota(jnp.int32, (nblk, t), 1)) // SEL_BLOCK
        forced = (jrow == 0) | (jrow == cur) | (jrow == cur - 1)
        lowest = jnp.float32(-3e38)
        score = jnp.where(forced | (jrow > cur), lowest, imp)
        jrow_f = jrow.astype(F32)
        nsel = jnp.where(forced, 0.0, 1.0)
        for _ in range(SEL_TOPK - SEL_FORCED):
            mx = jnp.max(score, axis=0, keepdims=True)
            first = jnp.min(jnp.where(score == mx, jrow_f, 2.0 * LANES), axis=0, keepdims=True)
            hit = jrow_f == first
            nsel = jnp.where(hit, 0.0, nsel)
            score = jnp.where(hit, lowest, score)
        parts = [jnp.zeros((NSA_D, t), F32), nsel]
        if LANES - NSA_D - nblk:
            parts.append(jnp.zeros((LANES - NSA_D - nblk, t), F32))
        nsel_rows = _to_rows(jnp.concatenate(parts, axis=0))
        for p in range(NSA_P):
            rs = slice(p * t, (p + 1) * t)
            qs_ref[rs, :] = (qz_ref[rs, :].astype(F32) + nsel_rows).astype(BF16)

    m_ref[...] = jnp.full(m_ref.shape, NEG, F32)
    acc_ref[...] = jnp.zeros(acc_ref.shape, F32)
    nwin = win_tiles + 1

    def entry(i):
        is_win = jnp.asarray(i < nwin)
        kt = jnp.where(is_win, qi - win_tiles + i, i - nwin)
        return is_win, kt, qi - kt

    def qk(i):
        is_win, kt, _ = entry(i)
        rows = pl.ds(pl.multiple_of(jnp.clip(kt, 0, qi) * t, t), t)
        q_all = qz_ref if isinstance(i, int) and i < nwin else qs_ref
        return _dot_nt(kaug_ref[0, is_win.astype(jnp.int32), 0, rows, :], q_all[...])

    def vt(i):
        is_win, kt, _ = entry(i)
        rows = pl.ds(pl.multiple_of(is_win.astype(jnp.int32) * NSA_VT_ROWS, NSA_VT_ROWS), NSA_VT_ROWS)
        return vt_ref[0, jnp.clip(kt, 0, qi), rows, :]

    def bias(i, cs):
        is_win, kt, d = entry(i)
        idx = jnp.where(is_win & (d == win_tiles), win_edge, jnp.clip(d, -1, nd) + 1)
        return tab_ref[0, jnp.where(kt < 0, 0, idx), :, cs]

    def state(i):
        st = jnp.asarray(i < nwin).astype(jnp.int32)
        return m_ref.at[st], acc_ref.at[st]

    _flash_list(nwin + qi + 1, qk, vt, bias, state, s_ref, None, p_ref, alpha_ref, False,
                before_loop=select_blocks)

    def branch_out(st):
        return acc_ref[st, 0:NSA_D, :] / acc_ref[st, NSA_D:NSA_D + 1, :]

    comb = comb_ref[...] + gate_row(1) * branch_out(0) + gate_row(2) * branch_out(1)

    for pair in range(NSA_P // 2):
        stacked = jnp.concatenate([comb[:, (2 * pair + h) * t:(2 * pair + h + 1) * t] for h in range(2)], axis=0)
        o_ref[0, :, pair * LANES:(pair + 1) * LANES] = _to_rows(stacked).astype(BF16)


def _nsa_attention(nq, kaug, vt, kc, vct, longc, tab, gates, ovlt):
    b, s, _ = nq.shape
    t = TILE
    nqt = s // t
    nc = kc.shape[2]
    nd = tab.shape[1] - 3
    gw = NSA_P * NSA_D
    return pl.pallas_call(
        functools.partial(_nsa_kernel, nd, nqt),
        grid=(b, NSA_G, nqt),
        in_specs=[
            pl.BlockSpec((1, t, NSA_P * LANES), lambda i, g, j: (i, j, g)),
            pl.BlockSpec((1, 2, 1, s, LANES), lambda i, g, j: (i, 0, g, 0, 0)),
            pl.BlockSpec((1, nqt, 2 * NSA_VT_ROWS, t), lambda i, g, j: (i, 0, g, 0)),
            pl.BlockSpec((1, 1, nc, LANES), lambda i, g, j: (i, g, 0, 0)),
            pl.BlockSpec((1, 1, NSA_D, nc), lambda i, g, j: (i, g, 0, 0)),
            pl.BlockSpec((1,) + longc.shape[1:], lambda i, g, j: (g, 0, 0)),
            pl.BlockSpec((1,) + tab.shape[1:], lambda i, g, j: (g, 0, 0, 0)),
            pl.BlockSpec((1, GATE_ROWS, t), lambda i, g, j: (i * nqt + j, g, 0)),
            _const_spec(ovlt.shape),
        ],
        out_specs=pl.BlockSpec((1, t, gw), lambda i, g, j: (i, j, g)),
        out_shape=jax.ShapeDtypeStruct((b, s, NSA_WIDTH), BF16),
        scratch_shapes=[
            pltpu.VMEM((NSA_P * t, LANES), BF16),
            pltpu.VMEM((NSA_P * t, LANES), BF16),
            pltpu.VMEM((2, t, NSA_P * t), F32),
            pltpu.VMEM((2, t, NSA_P * t), BF16),
            pltpu.VMEM((2, 1, NSA_P * t), F32),
            pltpu.VMEM((2, 1, NSA_P * t), F32),
            pltpu.VMEM((2, NSA_VT_ROWS, NSA_P * t), F32),
            pltpu.VMEM((NSA_D, NSA_P * t), F32),
            pltpu.VMEM((nc, t), F32),
            pltpu.VMEM((nc, NSA_P * t), F32),
            pltpu.VMEM((nc, NSA_P * t), BF16),
        ],
        compiler_params=_cparams(("parallel", "parallel", "arbitrary")),
        name="nsa_attention",
    )(nq, kaug, vt, kc, vct, longc, tab, gates, ovlt)


def _merge_kernel(x_ref, ya_ref, yb_ref, g_ref, wmg_ref, pa_ref, pb_ref, wo_ref, o_ref):
    x = x_ref[...]
    h = _rms(x, g_ref[...]).astype(BF16)
    mg = jax.nn.sigmoid(_dot(h, wmg_ref[...]))
    merged = mg[:, 0:D_MODEL] * _dot(ya_ref[...], pa_ref[...]) + mg[:, D_MODEL:] * _dot(yb_ref[...], pb_ref[...])
    o_ref[...] = x + _dot(merged.astype(BF16), wo_ref[...])


def _merge(x2, ya, yb, g, wmg, pa, pb, wo):
    n, d = x2.shape
    tm = TILE
    row = lambda wd: pl.BlockSpec((tm, wd), lambda i: (i, 0))
    return pl.pallas_call(
        _merge_kernel,
        grid=(n // tm,),
        in_specs=[row(d), row(DA_WIDTH), row(NSA_WIDTH), _const_spec((1, d)),
                  _const_spec(wmg.shape), _const_spec(pa.shape), _const_spec(pb.shape), _const_spec(wo.shape)],
        out_specs=row(d),
        out_shape=jax.ShapeDtypeStruct((n, d), F32),
        compiler_params=_cparams(("parallel",)),
        name="gated_merge_out_proj",
    )(x2, ya, yb, g, wmg, pa, pb, wo)


def _rel_bucket(dist):
    n = jnp.maximum(dist, 0)
    max_exact = N_BUCKETS // 2
    nf = jnp.maximum(n, 1).astype(jnp.float32)
    large = max_exact + (jnp.log(nf / max_exact) / math.log(MAX_DISTANCE / max_exact)
                         * (N_BUCKETS - max_exact)).astype(jnp.int32)
    large = jnp.minimum(large, N_BUCKETS - 1)
    return jnp.where(n < max_exact, n, large)


def _bias_of(table, dist):
    bucket = _rel_bucket(dist)
    col = lambda v: v.reshape((-1,) + (1,) * dist.ndim)
    out = jnp.zeros((table.shape[1],) + dist.shape, F32)
    for bkt in range(N_BUCKETS):
        out = jnp.where(bucket == bkt, col(table[bkt].astype(F32)), out)
    return jnp.where(dist >= 0, out, NEG)


def _far_tiles(t):
    return -(-(MAX_DISTANCE + t - 1) // t)


def _tile_bias(table, t, window_edge):
    nd = _far_tiles(t)
    d0 = 2 * t - 1
    width = (nd + 3) * t - 1
    v = _bias_of(table, jnp.arange(width) - d0)
    w = jnp.concatenate([v, jnp.zeros((v.shape[0], 1), F32)], axis=1)
    m = jnp.tile(w, (1, t))[:, :t * width].reshape(-1, t, width)
    bias = jnp.stack([m[:, :, d * t + d0:d * t + d0 + t] for d in range(-1, nd + 1)], axis=1)
    if window_edge:
        we = WINDOW // t
        j = jnp.arange(t)[:, None]
        i = jnp.arange(t)[None, :]
        edge = jnp.where(we * t + i - j < WINDOW, bias[:, we + 1], NEG)
        bias = jnp.concatenate([bias, edge[:, None]], axis=1)
    return bias


def _cmp_bias(table, t, nqt):
    cpt = t // CMP_STRIDE
    r = jnp.arange((2 * nqt - 1) * cpt)[:, None]
    dist = (nqt - 1 - r // cpt) * t + jnp.arange(t)[None, :] - ((r % cpt) * CMP_STRIDE + CMP_BLOCK - 1)
    return _bias_of(table, dist)


def _lanes_by_head(x):
    x = x.reshape((NSA_G, NSA_P) + x.shape[1:])
    x = jnp.moveaxis(x, 1, -2)
    return x.reshape(x.shape[:-2] + (NSA_P * x.shape[-1],))


def _static_tables(s):
    nc = s // CMP_STRIDE
    n_cmp = nc - CMP_BLOCK // CMP_STRIDE + 1
    nblk = s // SEL_BLOCK
    ovlt = np.zeros((nblk, nc), np.float32)
    for c in range(n_cmp):
        for jb in range(nblk):
            if c * CMP_STRIDE < (jb + 1) * SEL_BLOCK and c * CMP_STRIDE + CMP_BLOCK - 1 >= jb * SEL_BLOCK:
                ovlt[jb, c] = 1.0
    blk = np.zeros((s, LANES), np.float32)
    blk[np.arange(s), NSA_D + np.arange(s) // SEL_BLOCK] = NEG
    r = np.kron(np.eye(512 // 64, dtype=np.float32), np.ones((64, 64), np.float32))
    return jnp.asarray(ovlt, BF16), jnp.asarray(blk, BF16), jnp.asarray(r, BF16)


def _pad_blocks(w, axis=-1):
    w = jnp.moveaxis(w, axis, -1)
    w = w.reshape(w.shape[:-1] + (w.shape[-1] // NSA_D, NSA_D))
    w = jnp.pad(w, [(0, 0)] * (w.ndim - 1) + [(0, LANES - NSA_D)])
    return jnp.moveaxis(w.reshape(w.shape[:-2] + (-1,)), -1, axis)


def kernel(x, w_in, w_branch_a, w_branch_b, w_out, norm_ffn1, norm_mix, norm_ffn2, ffn1_w1, ffn1_w3, ffn1_w2, ffn2_w1, ffn2_w3, ffn2_w2, da_q_gain, da_k_gain, da_lambda_q1, da_lambda_k1, da_lambda_q2, da_lambda_k2, da_subln_gain, nsa_q_gain, nsa_k_gain, cmp_pe_k, cmp_w1_k, cmp_w2_k, cmp_pe_v, cmp_w1_v, cmp_w2_v, rel_bias_table):
    b, s, d = x.shape
    depth = w_in.shape[0]
    t = TILE
    assert d == D_MODEL and s % t == 0 and WINDOW % t == 0 and s >= WINDOW and (b * s) % FFN_ROWS == 0
    assert SEL_TOPK <= s // SEL_BLOCK <= LANES - NSA_D
    n = b * s
    nqt = s // t

    ovlt, blk, r512 = _static_tables(s)
    tab_a = _tile_bias(rel_bias_table[:, :DA_HEADS], t, False) * LOG2E
    tab_b = _lanes_by_head(_tile_bias(rel_bias_table[:, DA_HEADS:], t, True)) * LOG2E
    longc = _lanes_by_head(_cmp_bias(rel_bias_table[:, DA_HEADS:], t, nqt)) * LOG2E

    bf = lambda w: w.astype(BF16)
    f1w1, f1w3, f1w2, f2w1, f2w3, f2w2 = map(bf, (ffn1_w1, ffn1_w3, ffn1_w2, ffn2_w1, ffn2_w3, ffn2_w2))
    wba, wbb, wo = bf(w_branch_a), bf(w_branch_b), bf(w_out)
    w_row = bf(jnp.concatenate([w_in[:, :, 0:1024], w_in[:, :, 1536:2304], w_in[:, :, 2304:2432], w_in[:, :, 2560:2688]],
                               axis=2))
    w_ng = jnp.pad(w_in[:, :, 2816:2840].reshape(depth, d, NSA_G, NSA_P * 3),
                   ((0, 0), (0, 0), (0, 0), (0, GATE_ROWS - NSA_P * 3))).reshape(depth, d, NSA_G * GATE_ROWS)
    w_col = bf(jnp.swapaxes(jnp.concatenate([w_in[:, :, 1024:1536], w_in[:, :, 2432:2560], w_in[:, :, 2688:2816], w_ng],
                                            axis=2), 1, 2))
    w_mg = bf(w_in[:, :, 2840:])
    cw1 = lambda w: bf(_pad_blocks(w.reshape(depth, CMP_BLOCK, NSA_D, CMP_HIDDEN), axis=2))
    cw1k, cw1v = cw1(cmp_w1_k), cw1(cmp_w1_v)
    pek, pev = _pad_blocks(cmp_pe_k.astype(F32)), _pad_blocks(cmp_pe_v.astype(F32))
    cw2k = bf(_pad_blocks(cmp_w2_k))
    cw2vt = bf(jnp.swapaxes(cmp_w2_v, 1, 2))

    x2 = x.reshape(n, d)
    row = lambda v: v.reshape(1, -1).astype(F32)
    tile = lambda v, k: jnp.tile(v.astype(F32), k).reshape(1, -1)
    for l in range(depth):
        x2 = _ffn(x2, row(norm_ffn1[l]), f1w1[l], f1w3[l], f1w2[l])

        dq, dk, nq, kcvc, kaug, vta, vtb, gates = _proj(
            x2, row(norm_mix[l]), w_row[l], w_col[l], r512, blk,
            tile(da_q_gain[l], 8), tile(da_k_gain[l], 8), tile(nsa_q_gain[l], 8),
            tile(nsa_k_gain[l, 1], NSA_G), tile(nsa_k_gain[l, 2], NSA_G), b, s)
        vta = vta.reshape(b, nqt, DA_HEADS * DA_VT_ROWS, t)
        vtb = vtb.reshape(b, nqt, 2 * NSA_G * NSA_VT_ROWS, t)

        lam_init = 0.8 - 0.6 * math.exp(-0.3 * l)
        lamv = jnp.stack([da_lambda_q1[l], da_lambda_k1[l], da_lambda_q2[l], da_lambda_k2[l]]).astype(F32)
        ya = _diff_attention(dq.reshape(b, s, 512), dk.reshape(b, s, 512), vta,
                             tab_a, lamv, row(da_subln_gain[l]), lam_init)

        kc, vct = _compress(kcvc.reshape(b, s, 2 * NSA_G * LANES), pek[l], pev[l], cw1k[l], cw1v[l], cw2k[l], cw2vt[l],
                            _pad_blocks(row(nsa_k_gain[l, 0])))
        yb = _nsa_attention(nq.reshape(b, s, NSA_HEADS * LANES), kaug, vtb, kc, vct, longc, tab_b, gates, ovlt)

        x2 = _merge(x2, ya.reshape(n, DA_WIDTH), yb.reshape(n, NSA_WIDTH), row(norm_mix[l]),
                    w_mg[l], wba[l], wbb[l], wo[l])
        x2 = _ffn(x2, row(norm_ffn2[l]), f2w1[l], f2w3[l], f2w2[l])
    return x2.reshape(b, s, d)
```

```python
import functools
import math

import numpy as np
import jax
import jax.numpy as jnp
from jax import lax
from jax.experimental import pallas as pl
from jax.experimental.pallas import tpu as pltpu

F32 = jnp.float32
BF16 = jnp.bfloat16

D_MODEL = 1024
DA_HEADS = 4
DA_HEAD_DIM = 64
DA_V_DIM = 2 * DA_HEAD_DIM
DA_WIDTH = DA_HEADS * DA_V_DIM
NSA_HEADS = 8
NSA_G = 2
NSA_P = NSA_HEADS // NSA_G
NSA_D = 64
NSA_WIDTH = NSA_HEADS * NSA_D
NSA_KV = NSA_G * NSA_D
CMP_BLOCK = 32
CMP_STRIDE = 16
CMP_HIDDEN = 128
SEL_BLOCK = 64
SEL_TOPK = 8
SEL_FORCED = 3
WINDOW = 512
N_BUCKETS = 32
MAX_DISTANCE = 1024
EPS = 1e-6
NEG = -1e30

LANES = 128
VMEM_LIMIT = 52 * 1024 * 1024

TILE = 256
FFN_ROWS = 512
CW = LANES
DA_UNROLL = 8
NSA_UNROLL = 4
GATE_ROWS = 16
ONES_ROWS = 16
DA_VT_ROWS = DA_V_DIM + ONES_ROWS
NSA_VT_ROWS = NSA_D + ONES_ROWS
LOG2E = math.log2(math.e)

COL_ROWS = DA_WIDTH + 2 * NSA_KV + NSA_G * GATE_ROWS


def _cparams(sem):
    return pltpu.CompilerParams(dimension_semantics=sem, vmem_limit_bytes=VMEM_LIMIT)


def _const_spec(shape):
    nd = len(shape)
    return pl.BlockSpec(shape, lambda *_: (0,) * nd)


def _rms(xf, g):
    ms = jnp.mean(xf * xf, axis=-1, keepdims=True)
    return xf * lax.rsqrt(ms + EPS) * g


def _dot(a, b):
    return jnp.dot(a, b, preferred_element_type=F32)


def _dot_nt(a, b):
    return lax.dot_general(a, b, (((1,), (1,)), ((), ())), preferred_element_type=F32)


def _bf16_pieces(x, parts):
    out = []
    r = x
    for i in range(parts):
        piece = r.astype(BF16)
        out.append(piece)
        if i + 1 < parts:
            r = r - piece.astype(F32)
    return out


def _ffn_kernel(x_ref, g_ref, w1_ref, w3_ref, w2_ref, o_ref):
    x = x_ref[...]
    h = _rms(x, g_ref[...]).astype(BF16)
    a = _dot(h, w1_ref[...])
    b = _dot(h, w3_ref[...])
    t = (jax.nn.silu(a) * b).astype(BF16)
    o_ref[...] = x + 0.5 * _dot(t, w2_ref[...])


def _resident_spec(shape):
    nd = len(shape)
    return pl.BlockSpec(shape, lambda *_: (0,) * nd, pipeline_mode=pl.Buffered(1))


def _ffn(x2, g, w1, w3, w2):
    n, d = x2.shape
    tm = FFN_ROWS
    return pl.pallas_call(
        _ffn_kernel,
        grid=(n // tm,),
        in_specs=[
            pl.BlockSpec((tm, d), lambda i: (i, 0)),
            _const_spec((1, d)),
            _resident_spec(w1.shape),
            _resident_spec(w3.shape),
            _resident_spec(w2.shape),
        ],
        out_specs=pl.BlockSpec((tm, d), lambda i: (i, 0)),
        out_shape=jax.ShapeDtypeStruct((n, d), F32),
        compiler_params=_cparams(("parallel",)),
        name="ffn_half_step",
    )(x2, g, w1, w3, w2)


def _group_rms(z, r_ref, gain):
    w = z.shape[-1]
    ss = _dot((z * z).astype(BF16), r_ref[0:w, 0:w])
    return z * lax.rsqrt(ss * (1.0 / 64.0) + EPS) * gain


def _proj_kernel(x_ref, g_ref, w_ref, wt_ref, r_ref, blk_ref, gq_ref, gk_ref, gnq_ref, gks_ref, gkw_ref,
                 dq_ref, dk_ref, nq_ref, kcvc_ref, kaug_ref, vta_ref, vtb_ref, gate_ref):
    h = _rms(x_ref[...], g_ref[...]).astype(BF16)
    z = _dot(h, w_ref[...])
    zt = _dot_nt(wt_ref[...], h)
    scale = DA_HEAD_DIM ** -0.5 * LOG2E
    dq_ref[...] = (_group_rms(z[:, 0:512], r_ref, gq_ref[...]) * scale).astype(BF16)
    dk_ref[...] = _group_rms(z[:, 512:1024], r_ref, gk_ref[...]).astype(BF16)
    low = lax.broadcasted_iota(jnp.int32, (z.shape[0], LANES), 1) < NSA_D

    def lane_blocks(v):
        out = []
        for c in range(v.shape[1] // LANES):
            pair = v[:, c * LANES:(c + 1) * LANES]
            out.append(jnp.where(low, pair, 0.0))
            out.append(jnp.where(low, pltpu.roll(pair, NSA_D, axis=1), 0.0))
        return out

    nq = _group_rms(z[:, 1024:1536], r_ref, gnq_ref[...]) * scale
    for hd, piece in enumerate(lane_blocks(nq)):
        nq_ref[:, hd * LANES:(hd + 1) * LANES] = piece.astype(BF16)
    for c, piece in enumerate(lane_blocks(z[:, 1536:1792])):
        kcvc_ref[:, c * LANES:(c + 1) * LANES] = piece
    blk = blk_ref[...].astype(F32)
    for g, piece in enumerate(lane_blocks(_group_rms(z[:, 1792:1920], r_ref, gks_ref[...]))):
        kaug_ref[0, 0, g] = (piece + blk).astype(BF16)
    for g, piece in enumerate(lane_blocks(_group_rms(z[:, 1920:2048], r_ref, gkw_ref[...]))):
        kaug_ref[0, 1, g] = piece.astype(BF16)
    ones = jnp.ones((ONES_ROWS, zt.shape[1]), BF16)
    for hd in range(DA_HEADS):
        vta_ref[0, hd * DA_VT_ROWS:hd * DA_VT_ROWS + DA_V_DIM] = zt[hd * DA_V_DIM:(hd + 1) * DA_V_DIM].astype(BF16)
        vta_ref[0, hd * DA_VT_ROWS + DA_V_DIM:(hd + 1) * DA_VT_ROWS] = ones
    for blk4 in range(2 * NSA_G):
        src = DA_WIDTH + (blk4 % 2) * NSA_KV + (blk4 // 2) * NSA_D
        vtb_ref[0, blk4 * NSA_VT_ROWS:blk4 * NSA_VT_ROWS + NSA_D] = zt[src:src + NSA_D].astype(BF16)
        vtb_ref[0, blk4 * NSA_VT_ROWS + NSA_D:(blk4 + 1) * NSA_VT_ROWS] = ones
    gate_ref[0] = jax.nn.sigmoid(zt[DA_WIDTH + 2 * NSA_KV:COL_ROWS])


def _proj(x2, g, w, wt, r, blk, gq, gk, gnq, gks, gkw, b, s):
    n, d = x2.shape
    tm = TILE
    nt = s // tm
    row = lambda wd: pl.BlockSpec((tm, wd), lambda i: (i, 0))
    col = lambda rows: pl.BlockSpec((1, rows, tm), lambda i: (i, 0, 0))
    return pl.pallas_call(
        _proj_kernel,
        grid=(n // tm,),
        in_specs=[row(d), _const_spec((1, d)), _const_spec(w.shape), _const_spec(wt.shape),
                  _const_spec(r.shape), pl.BlockSpec((tm, LANES), lambda i: (i % nt, 0)),
                  _const_spec((1, 512)), _const_spec((1, 512)), _const_spec((1, 512)),
                  _const_spec((1, NSA_KV)), _const_spec((1, NSA_KV))],
        out_specs=[row(512), row(512), row(NSA_HEADS * LANES), row(2 * NSA_G * LANES),
                   pl.BlockSpec((1, 2, NSA_G, tm, LANES), lambda i: (i // nt, 0, 0, i % nt, 0)),
                   col(DA_HEADS * DA_VT_ROWS), col(2 * NSA_G * NSA_VT_ROWS), col(NSA_G * GATE_ROWS)],
        out_shape=[
            jax.ShapeDtypeStruct((n, 512), BF16),
            jax.ShapeDtypeStruct((n, 512), BF16),
            jax.ShapeDtypeStruct((n, NSA_HEADS * LANES), BF16),
            jax.ShapeDtypeStruct((n, 2 * NSA_G * LANES), F32),
            jax.ShapeDtypeStruct((b, 2, NSA_G, s, LANES), BF16),
            jax.ShapeDtypeStruct((n // tm, DA_HEADS * DA_VT_ROWS, tm), BF16),
            jax.ShapeDtypeStruct((n // tm, 2 * NSA_G * NSA_VT_ROWS, tm), BF16),
            jax.ShapeDtypeStruct((n // tm, NSA_G * GATE_ROWS, tm), F32),
        ],
        compiler_params=_cparams(("parallel",)),
        name="norm_in_proj",
    )(x2, g, w, wt, r, blk, gq, gk, gnq, gks, gkw)


def _cmp_kernel(ck_ref, cv_ref, pek_ref, pev_ref, w1k_ref, w1v_ref, w2k_ref, w2vt_ref, gk_ref, kc_ref, vct_ref):
    nc = kc_ref.shape[2]

    def hidden(c_ref, pe_ref, w1_ref):
        lo = hi = None
        for t in range(CMP_STRIDE):
            x = c_ref[0, pl.ds(t, nc, stride=CMP_STRIDE), :]
            a = _dot((x + pe_ref[t:t + 1, :]).astype(BF16), w1_ref[t])
            b = _dot((x + pe_ref[CMP_STRIDE + t:CMP_STRIDE + t + 1, :]).astype(BF16), w1_ref[CMP_STRIDE + t])
            lo = a if lo is None else lo + a
            hi = b if hi is None else hi + b
        return jax.nn.gelu(lo + pltpu.roll(hi, nc - 1, axis=0)).astype(BF16)

    kc = _dot(hidden(ck_ref, pek_ref, w1k_ref), w2k_ref[...])
    ms = jnp.sum(kc * kc, axis=-1, keepdims=True) * (1.0 / NSA_D)
    kc_ref[0, 0] = (kc * lax.rsqrt(ms + EPS) * gk_ref[...]).astype(BF16)
    vct_ref[0, 0] = _dot_nt(w2vt_ref[...], hidden(cv_ref, pev_ref, w1v_ref)).astype(BF16)


def _compress(kcvc, pek, pev, w1k, w1v, w2k, w2vt, gk):
    b, s, _ = kcvc.shape
    g = NSA_G
    nc = s // CMP_STRIDE
    return pl.pallas_call(
        _cmp_kernel,
        grid=(b, g),
        in_specs=[pl.BlockSpec((1, s, LANES), lambda i, j: (i, 0, j)),
                  pl.BlockSpec((1, s, LANES), lambda i, j: (i, 0, NSA_G + j)),
                  _const_spec(pek.shape), _const_spec(pev.shape),
                  _const_spec(w1k.shape), _const_spec(w1v.shape),
                  _const_spec(w2k.shape), _const_spec(w2vt.shape), _const_spec(gk.shape)],
        out_specs=[pl.BlockSpec((1, 1, nc, LANES), lambda i, j: (i, j, 0, 0)),
                   pl.BlockSpec((1, 1, NSA_D, nc), lambda i, j: (i, j, 0, 0))],
        out_shape=[jax.ShapeDtypeStruct((b, g, nc, LANES), BF16),
                   jax.ShapeDtypeStruct((b, g, NSA_D, nc), BF16)],
        compiler_params=_cparams(("parallel", "parallel")),
        name="block_compress",
    )(kcvc, kcvc, pek, pev, w1k, w1v, w2k, w2vt, gk)


def _flash_list(n, qk_fn, vt_fn, bias_fn, state_fn, s_ref, mx_ref, p_ref, alpha_ref, bias_with_scores, unroll,
                before_loop=None):
    width = s_ref.shape[2]
    chunks = [slice(c * CW, (c + 1) * CW) for c in range(width // CW)]

    def scores(i, slot):
        raw = qk_fn(i)
        if not bias_with_scores:
            s_ref[slot] = raw
            return
        for cs in chunks:
            sb = raw[:, cs] + bias_fn(i, cs)
            s_ref[slot, :, cs] = sb
            mx_ref[slot, :, cs] = jnp.max(sb, axis=0, keepdims=True)

    def pv(i, slot):
        acc_ref = state_fn(i)[1]
        acc_ref[...] = alpha_ref[slot] * acc_ref[...] + _dot(vt_fn(i), p_ref[slot])

    def softmax(i, slot):
        m_ref = state_fn(i)[0]
        if bias_with_scores:
            m_old = m_ref[...]
            m_new = jnp.maximum(m_old, mx_ref[slot])
            alpha_ref[slot] = jnp.exp2(m_old - m_new)
            m_ref[...] = m_new
            for cs in chunks:
                p_ref[slot, :, cs] = jnp.exp2(s_ref[slot, :, cs] - m_new[:, cs]).astype(BF16)
            return
        for cs in chunks:
            s = s_ref[slot, :, cs] + bias_fn(i, cs)
            m_old = m_ref[:, cs]
            m_new = jnp.maximum(m_old, jnp.max(s, axis=0, keepdims=True))
            p_ref[slot, :, cs] = jnp.exp2(s - m_new).astype(BF16)
            alpha_ref[slot, :, cs] = jnp.exp2(m_old - m_new)
            m_ref[:, cs] = m_new

    scores(0, 0)
    softmax(0, 0)
    scores(1, 1)
    if before_loop is not None:
        before_loop()

    def body(g, carry):
        first = 1 + unroll * g
        for u in range(unroll):
            slot = (u + 1) % 2
            pv(first + u - 1, 1 - slot)
            softmax(first + u, slot)
            scores(first + u + 1, 1 - slot)
        return carry

    groups = (n - 1 + unroll - 1) // unroll
    lax.fori_loop(0, groups, body, 0)
    pv(unroll * groups, 0)


def _to_rows(xt):
    t = xt.shape[1]
    return jnp.concatenate([xt[:, c * LANES:(c + 1) * LANES].T for c in range(t // LANES)], axis=0)


def _da_kernel(lam_init, nd, nqt, qt_of, kt_of, q_ref, k_ref, vt_ref, tab_ref, lamv_ref, gs_ref, o_ref,
               qq_ref, s_ref, mx_ref, p_ref, alpha_ref, m_ref, acc_ref):
    t = TILE
    lane = lax.broadcasted_iota(jnp.int32, (t, LANES), 1)
    for qt in range(nqt):
        q = q_ref[0, qt * t:(qt + 1) * t, :].astype(F32)
        qq_ref[qt, 0:t, :] = jnp.where(lane < DA_HEAD_DIM, q, 0.0).astype(BF16)
        qq_ref[qt, t:2 * t, :] = jnp.where(lane >= DA_HEAD_DIM, q, 0.0).astype(BF16)
    m_ref[...] = jnp.full(m_ref.shape, NEG, F32)
    acc_ref[...] = jnp.zeros(acc_ref.shape, F32)

    key_tile = lambda i: jnp.minimum(kt_of[i], nqt - 1)

    def qk(i):
        return _dot_nt(k_ref[0, pl.ds(pl.multiple_of(key_tile(i) * t, t), t), :], qq_ref[qt_of[i]])

    def bias(i, cs):
        d = jnp.clip(qt_of[i] - kt_of[i], -1, nd)
        return tab_ref[0, d + 1, :, slice(cs.start % t, cs.start % t + CW)]

    _flash_list(nqt * (nqt + 1) // 2, qk, lambda i: vt_ref[0, key_tile(i)], bias,
                lambda i: (m_ref.at[qt_of[i]], acc_ref.at[qt_of[i]]), s_ref, mx_ref, p_ref, alpha_ref, True, DA_UNROLL)

    lv = lamv_ref[...]
    lam = (jnp.exp(jnp.sum(lv[0:1] * lv[1:2], axis=-1, keepdims=True))
           - jnp.exp(jnp.sum(lv[2:3] * lv[3:4], axis=-1, keepdims=True)) + lam_init)

    for qt in range(nqt):
        ot = acc_ref[qt, 0:DA_V_DIM, :] / acc_ref[qt, DA_V_DIM:DA_V_DIM + 1, :]
        y = _to_rows(ot[:, 0:t] - lam * ot[:, t:2 * t])
        o_ref[0, qt * t:(qt + 1) * t, :] = (_rms(y, gs_ref[...]) * (1.0 - lam_init)).astype(BF16)


def _diff_attention(dq, dk, vt, tab, lamv, gs, lam_init):
    b, s, _ = dq.shape
    t = TILE
    nqt = s // t
    nd = tab.shape[1] - 2
    pairs = [(qt, kt) for qt in range(nqt) for kt in range(qt + 1)] + [(nqt - 1, nqt)] * (DA_UNROLL + 2)
    qt_of = jnp.asarray(np.array([p[0] for p in pairs], np.int32))
    kt_of = jnp.asarray(np.array([p[1] for p in pairs], np.int32))
    seq = pl.BlockSpec((1, s, LANES), lambda i, h, *_: (i, 0, h))
    return pl.pallas_call(
        functools.partial(_da_kernel, lam_init, nd, nqt),
        grid_spec=pltpu.PrefetchScalarGridSpec(
            num_scalar_prefetch=2,
            grid=(b, DA_HEADS),
            in_specs=[
                seq, seq,
                pl.BlockSpec((1, nqt, DA_VT_ROWS, t), lambda i, h, *_: (i, 0, h, 0)),
                pl.BlockSpec((1,) + tab.shape[1:], lambda i, h, *_: (h, 0, 0, 0)),
                pl.BlockSpec(lamv.shape, lambda i, h, *_: (0, 0)),
                pl.BlockSpec(gs.shape, lambda i, h, *_: (0, 0)),
            ],
            out_specs=seq,
            scratch_shapes=[
                pltpu.VMEM((nqt, 2 * t, LANES), BF16),
                pltpu.VMEM((2, t, 2 * t), F32),
                pltpu.VMEM((2, 1, 2 * t), F32),
                pltpu.VMEM((2, t, 2 * t), BF16),
                pltpu.VMEM((2, 1, 2 * t), F32),
                pltpu.VMEM((nqt, 1, 2 * t), F32),
                pltpu.VMEM((nqt, DA_VT_ROWS, 2 * t), F32),
            ]),
        out_shape=jax.ShapeDtypeStruct((b, s, DA_WIDTH), BF16),
        compiler_params=_cparams(("parallel", "parallel")),
        name="diff_attention",
    )(qt_of, kt_of, dq, dk, vt, tab, lamv, gs)


def _nsa_kernel(nd, nqt, q_ref, kaug_ref, vt_ref, kc_ref, vct_ref, longc_ref, tab_ref, gate_ref,
                ovlt_ref, o_ref,
                qz_ref, qs_ref, s_ref, p_ref, alpha_ref, m_ref, acc_ref, comb_ref, psum_ref, sc_ref, pc_ref):
    t = TILE
    halves = t // CW
    nchunk = NSA_P * halves
    qi = pl.program_id(2)
    win_tiles = WINDOW // t
    win_edge = nd + 2

    def gate_row(r):
        return jnp.concatenate([gate_ref[0, 3 * p + r:3 * p + r + 1, :] for p in range(NSA_P)], axis=1)

    for p in range(NSA_P):
        qz_ref[p * t:(p + 1) * t, :] = q_ref[0, :, p * LANES:(p + 1) * LANES]

    def select_blocks():
        nc = sc_ref.shape[0]
        cstart = pl.multiple_of((nqt - 1 - qi) * (t // CMP_STRIDE), t // CMP_STRIDE)
        sc_ref[...] = _dot_nt(kc_ref[0, 0], qz_ref[...])
        for c in range(nchunk):
            p, half = divmod(c, halves)
            cs = slice(c * CW, (c + 1) * CW)
            hs = slice(half * CW, (half + 1) * CW)
            s = sc_ref[:, cs] + longc_ref[0, pl.ds(cstart, nc), cs]
            m = jnp.max(s, axis=0, keepdims=True)
            e = jnp.exp2(s - m)
            pc = jnp.where(m > 0.5 * NEG, e / jnp.sum(e, axis=0, keepdims=True), 0.0)
            pc_ref[:, cs] = pc.astype(BF16)
            if p == 0:
                psum_ref[:, hs] = pc
            else:
                psum_ref[:, hs] += pc
        comb_ref[...] = gate_row(0) * _dot(vct_ref[0, 0], pc_ref[...])
        select_topk()

    def select_topk():
        ovlt = ovlt_ref[...]
        imp = None
        for piece in _bf16_pieces(psum_ref[...], 3):
            d = _dot(ovlt, piece)
            imp = d if imp is None else imp + d
        nblk = imp.shape[0]
        jrow = lax.broadcasted_iota(jnp.int32, (nblk, t), 0)
        cur = (qi * t + lax.broadcasted_iota(jnp.int32, (nblk, t), 1)) // SEL_BLOCK
        forced = (jrow == 0) | (jrow == cur) | (jrow == cur - 1)
        lowest = jnp.float32(-3e38)
        score = jnp.where(forced | (jrow > cur), lowest, imp)
        jrow_f = jrow.astype(F32)
        nsel = jnp.where(forced, 0.0, 1.0)
        for _ in range(SEL_TOPK - SEL_FORCED):
            mx = jnp.max(score, axis=0, keepdims=True)
            first = jnp.min(jnp.where(score == mx, jrow_f, 2.0 * LANES), axis=0, keepdims=True)
            hit = jrow_f == first
            nsel = jnp.where(hit, 0.0, nsel)
            score = jnp.where(hit, lowest, score)
        parts = [jnp.zeros((NSA_D, t), F32), nsel]
        if LANES - NSA_D - nblk:
            parts.append(jnp.zeros((LANES - NSA_D - nblk, t), F32))
        nsel_rows = _to_rows(jnp.concatenate(parts, axis=0))
        for p in range(NSA_P):
            rs = slice(p * t, (p + 1) * t)
            qs_ref[rs, :] = (qz_ref[rs, :].astype(F32) + nsel_rows).astype(BF16)

    m_ref[...] = jnp.full(m_ref.shape, NEG, F32)
    acc_ref[...] = jnp.zeros(acc_ref.shape, F32)
    nwin = win_tiles + 1

    def entry(i):
        is_win = jnp.asarray(i < nwin)
        kt = jnp.where(is_win, qi - win_tiles + i, i - nwin)
        return is_win, kt, qi - kt

    def qk(i):
        is_win, kt, _ = entry(i)
        rows = pl.ds(pl.multiple_of(jnp.clip(kt, 0, qi) * t, t), t)
        q_all = qz_ref if isinstance(i, int) and i < nwin else qs_ref
        return _dot_nt(kaug_ref[0, is_win.astype(jnp.int32), 0, rows, :], q_all[...])

    def vt(i):
        is_win, kt, _ = entry(i)
        rows = pl.ds(pl.multiple_of(is_win.astype(jnp.int32) * NSA_VT_ROWS, NSA_VT_ROWS), NSA_VT_ROWS)
        return vt_ref[0, jnp.clip(kt, 0, qi), rows, :]

    def bias(i, cs):
        is_win, kt, d = entry(i)
        idx = jnp.where(is_win & (d == win_tiles), win_edge, jnp.clip(d, -1, nd) + 1)
        return tab_ref[0, jnp.where(kt < 0, 0, idx), :, cs]

    def state(i):
        st = jnp.asarray(i < nwin).astype(jnp.int32)
        return m_ref.at[st], acc_ref.at[st]

    _flash_list(nwin + qi + 1, qk, vt, bias, state, s_ref, None, p_ref, alpha_ref, False, NSA_UNROLL,
                before_loop=select_blocks)

    def branch_out(st):
        return acc_ref[st, 0:NSA_D, :] / acc_ref[st, NSA_D:NSA_D + 1, :]

    comb = comb_ref[...] + gate_row(1) * branch_out(0) + gate_row(2) * branch_out(1)

    for pair in range(NSA_P // 2):
        stacked = jnp.concatenate([comb[:, (2 * pair + h) * t:(2 * pair + h + 1) * t] for h in range(2)], axis=0)
        o_ref[0, :, pair * LANES:(pair + 1) * LANES] = _to_rows(stacked).astype(BF16)


def _nsa_attention(nq, kaug, vt, kc, vct, longc, tab, gates, ovlt):
    b, s, _ = nq.shape
    t = TILE
    nqt = s // t
    nc = kc.shape[2]
    nd = tab.shape[1] - 3
    gw = NSA_P * NSA_D
    return pl.pallas_call(
        functools.partial(_nsa_kernel, nd, nqt),
        grid=(b, NSA_G, nqt),
        in_specs=[
            pl.BlockSpec((1, t, NSA_P * LANES), lambda i, g, j: (i, j, g)),
            pl.BlockSpec((1, 2, 1, s, LANES), lambda i, g, j: (i, 0, g, 0, 0)),
            pl.BlockSpec((1, nqt, 2 * NSA_VT_ROWS, t), lambda i, g, j: (i, 0, g, 0)),
            pl.BlockSpec((1, 1, nc, LANES), lambda i, g, j: (i, g, 0, 0)),
            pl.BlockSpec((1, 1, NSA_D, nc), lambda i, g, j: (i, g, 0, 0)),
            pl.BlockSpec((1,) + longc.shape[1:], lambda i, g, j: (g, 0, 0)),
            pl.BlockSpec((1,) + tab.shape[1:], lambda i, g, j: (g, 0, 0, 0)),
            pl.BlockSpec((1, GATE_ROWS, t), lambda i, g, j: (i * nqt + j, g, 0)),
            _const_spec(ovlt.shape),
        ],
        out_specs=pl.BlockSpec((1, t, gw), lambda i, g, j: (i, j, g)),
        out_shape=jax.ShapeDtypeStruct((b, s, NSA_WIDTH), BF16),
        scratch_shapes=[
            pltpu.VMEM((NSA_P * t, LANES), BF16),
            pltpu.VMEM((NSA_P * t, LANES), BF16),
            pltpu.VMEM((2, t, NSA_P * t), F32),
            pltpu.VMEM((2, t, NSA_P * t), BF16),
            pltpu.VMEM((2, 1, NSA_P * t), F32),
            pltpu.VMEM((2, 1, NSA_P * t), F32),
            pltpu.VMEM((2, NSA_VT_ROWS, NSA_P * t), F32),
            pltpu.VMEM((NSA_D, NSA_P * t), F32),
            pltpu.VMEM((nc, t), F32),
            pltpu.VMEM((nc, NSA_P * t), F32),
            pltpu.VMEM((nc, NSA_P * t), BF16),
        ],
        compiler_params=_cparams(("parallel", "parallel", "arbitrary")),
        name="nsa_attention",
    )(nq, kaug, vt, kc, vct, longc, tab, gates, ovlt)


def _merge_kernel(x_ref, ya_ref, yb_ref, g_ref, wmg_ref, pa_ref, pb_ref, wo_ref, o_ref):
    x = x_ref[...]
    h = _rms(x, g_ref[...]).astype(BF16)
    mg = jax.nn.sigmoid(_dot(h, wmg_ref[...]))
    merged = mg[:, 0:D_MODEL] * _dot(ya_ref[...], pa_ref[...]) + mg[:, D_MODEL:] * _dot(yb_ref[...], pb_ref[...])
    o_ref[...] = x + _dot(merged.astype(BF16), wo_ref[...])


def _merge(x2, ya, yb, g, wmg, pa, pb, wo):
    n, d = x2.shape
    tm = TILE
    row = lambda wd: pl.BlockSpec((tm, wd), lambda i: (i, 0))
    return pl.pallas_call(
        _merge_kernel,
        grid=(n // tm,),
        in_specs=[row(d), row(DA_WIDTH), row(NSA_WIDTH), _const_spec((1, d)),
                  _const_spec(wmg.shape), _const_spec(pa.shape), _const_spec(pb.shape), _const_spec(wo.shape)],
        out_specs=row(d),
        out_shape=jax.ShapeDtypeStruct((n, d), F32),
        compiler_params=_cparams(("parallel",)),
        name="gated_merge_out_proj",
    )(x2, ya, yb, g, wmg, pa, pb, wo)


def _rel_bucket(dist):
    n = jnp.maximum(dist, 0)
    max_exact = N_BUCKETS // 2
    nf = jnp.maximum(n, 1).astype(jnp.float32)
    large = max_exact + (jnp.log(nf / max_exact) / math.log(MAX_DISTANCE / max_exact)
                         * (N_BUCKETS - max_exact)).astype(jnp.int32)
    large = jnp.minimum(large, N_BUCKETS - 1)
    return jnp.where(n < max_exact, n, large)


def _bias_of(table, dist):
    bucket = _rel_bucket(dist)
    col = lambda v: v.reshape((-1,) + (1,) * dist.ndim)
    out = jnp.zeros((table.shape[1],) + dist.shape, F32)
    for bkt in range(N_BUCKETS):
        out = jnp.where(bucket == bkt, col(table[bkt].astype(F32)), out)
    return jnp.where(dist >= 0, out, NEG)


def _far_tiles(t):
    return -(-(MAX_DISTANCE + t - 1) // t)


def _tile_bias(table, t, window_edge):
    nd = _far_tiles(t)
    d0 = 2 * t - 1
    width = (nd + 3) * t - 1
    v = _bias_of(table, jnp.arange(width) - d0)
    w = jnp.concatenate([v, jnp.zeros((v.shape[0], 1), F32)], axis=1)
    m = jnp.tile(w, (1, t))[:, :t * width].reshape(-1, t, width)
    bias = jnp.stack([m[:, :, d * t + d0:d * t + d0 + t] for d in range(-1, nd + 1)], axis=1)
    if window_edge:
        we = WINDOW // t
        j = jnp.arange(t)[:, None]
        i = jnp.arange(t)[None, :]
        edge = jnp.where(we * t + i - j < WINDOW, bias[:, we + 1], NEG)
        bias = jnp.concatenate([bias, edge[:, None]], axis=1)
    return bias


def _cmp_bias(table, t, nqt):
    cpt = t // CMP_STRIDE
    r = jnp.arange((2 * nqt - 1) * cpt)[:, None]
    dist = (nqt - 1 - r // cpt) * t + jnp.arange(t)[None, :] - ((r % cpt) * CMP_STRIDE + CMP_BLOCK - 1)
    return _bias_of(table, dist)


def _lanes_by_head(x):
    x = x.reshape((NSA_G, NSA_P) + x.shape[1:])
    x = jnp.moveaxis(x, 1, -2)
    return x.reshape(x.shape[:-2] + (NSA_P * x.shape[-1],))


def _static_tables(s):
    nc = s // CMP_STRIDE
    n_cmp = nc - CMP_BLOCK // CMP_STRIDE + 1
    nblk = s // SEL_BLOCK
    ovlt = np.zeros((nblk, nc), np.float32)
    for c in range(n_cmp):
        for jb in range(nblk):
            if c * CMP_STRIDE < (jb + 1) * SEL_BLOCK and c * CMP_STRIDE + CMP_BLOCK - 1 >= jb * SEL_BLOCK:
                ovlt[jb, c] = 1.0
    blk = np.zeros((s, LANES), np.float32)
    blk[np.arange(s), NSA_D + np.arange(s) // SEL_BLOCK] = NEG
    r = np.kron(np.eye(512 // 64, dtype=np.float32), np.ones((64, 64), np.float32))
    return jnp.asarray(ovlt, BF16), jnp.asarray(blk, BF16), jnp.asarray(r, BF16)


def _pad_blocks(w, axis=-1):
    w = jnp.moveaxis(w, axis, -1)
    w = w.reshape(w.shape[:-1] + (w.shape[-1] // NSA_D, NSA_D))
    w = jnp.pad(w, [(0, 0)] * (w.ndim - 1) + [(0, LANES - NSA_D)])
    return jnp.moveaxis(w.reshape(w.shape[:-2] + (-1,)), -1, axis)


def kernel(x, w_in, w_branch_a, w_branch_b, w_out, norm_ffn1, norm_mix, norm_ffn2, ffn1_w1, ffn1_w3, ffn1_w2, ffn2_w1, ffn2_w3, ffn2_w2, da_q_gain, da_k_gain, da_lambda_q1, da_lambda_k1, da_lambda_q2, da_lambda_k2, da_subln_gain, nsa_q_gain, nsa_k_gain, cmp_pe_k, cmp_w1_k, cmp_w2_k, cmp_pe_v, cmp_w1_v, cmp_w2_v, rel_bias_table):
    b, s, d = x.shape
    depth = w_in.shape[0]
    t = TILE
    assert d == D_MODEL and s % t == 0 and WINDOW % t == 0 and s >= WINDOW and (b * s) % FFN_ROWS == 0
    assert SEL_TOPK <= s // SEL_BLOCK <= LANES - NSA_D
    n = b * s
    nqt = s // t

    ovlt, blk, r512 = _static_tables(s)
    tab_a = _tile_bias(rel_bias_table[:, :DA_HEADS], t, False) * LOG2E
    tab_b = _lanes_by_head(_tile_bias(rel_bias_table[:, DA_HEADS:], t, True)) * LOG2E
    longc = _lanes_by_head(_cmp_bias(rel_bias_table[:, DA_HEADS:], t, nqt)) * LOG2E

    bf = lambda w: w.astype(BF16)
    f1w1, f1w3, f1w2, f2w1, f2w3, f2w2 = map(bf, (ffn1_w1, ffn1_w3, ffn1_w2, ffn2_w1, ffn2_w3, ffn2_w2))
    wba, wbb, wo = bf(w_branch_a), bf(w_branch_b), bf(w_out)
    w_row = bf(jnp.concatenate([w_in[:, :, 0:1024], w_in[:, :, 1536:2304], w_in[:, :, 2304:2432], w_in[:, :, 2560:2688]],
                               axis=2))
    w_ng = jnp.pad(w_in[:, :, 2816:2840].reshape(depth, d, NSA_G, NSA_P * 3),
                   ((0, 0), (0, 0), (0, 0), (0, GATE_ROWS - NSA_P * 3))).reshape(depth, d, NSA_G * GATE_ROWS)
    w_col = bf(jnp.swapaxes(jnp.concatenate([w_in[:, :, 1024:1536], w_in[:, :, 2432:2560], w_in[:, :, 2688:2816], w_ng],
                                            axis=2), 1, 2))
    w_mg = bf(w_in[:, :, 2840:])
    cw1 = lambda w: bf(_pad_blocks(w.reshape(depth, CMP_BLOCK, NSA_D, CMP_HIDDEN), axis=2))
    cw1k, cw1v = cw1(cmp_w1_k), cw1(cmp_w1_v)
    pek, pev = _pad_blocks(cmp_pe_k.astype(F32)), _pad_blocks(cmp_pe_v.astype(F32))
    cw2k = bf(_pad_blocks(cmp_w2_k))
    cw2vt = bf(jnp.swapaxes(cmp_w2_v, 1, 2))

    x2 = x.reshape(n, d)
    row = lambda v: v.reshape(1, -1).astype(F32)
    tile = lambda v, k: jnp.tile(v.astype(F32), k).reshape(1, -1)
    for l in range(depth):
        x2 = _ffn(x2, row(norm_ffn1[l]), f1w1[l], f1w3[l], f1w2[l])

        dq, dk, nq, kcvc, kaug, vta, vtb, gates = _proj(
            x2, row(norm_mix[l]), w_row[l], w_col[l], r512, blk,
            tile(da_q_gain[l], 8), tile(da_k_gain[l], 8), tile(nsa_q_gain[l], 8),
            tile(nsa_k_gain[l, 1], NSA_G), tile(nsa_k_gain[l, 2], NSA_G), b, s)
        vta = vta.reshape(b, nqt, DA_HEADS * DA_VT_ROWS, t)
        vtb = vtb.reshape(b, nqt, 2 * NSA_G * NSA_VT_ROWS, t)

        lam_init = 0.8 - 0.6 * math.exp(-0.3 * l)
        lamv = jnp.stack([da_lambda_q1[l], da_lambda_k1[l], da_lambda_q2[l], da_lambda_k2[l]]).astype(F32)
        ya = _diff_attention(dq.reshape(b, s, 512), dk.reshape(b, s, 512), vta,
                             tab_a, lamv, row(da_subln_gain[l]), lam_init)

        kc, vct = _compress(kcvc.reshape(b, s, 2 * NSA_G * LANES), pek[l], pev[l], cw1k[l], cw1v[l], cw2k[l], cw2vt[l],
                            _pad_blocks(row(nsa_k_gain[l, 0])))
        yb = _nsa_attention(nq.reshape(b, s, NSA_HEADS * LANES), kaug, vtb, kc, vct, longc, tab_b, gates, ovlt)

        x2 = _merge(x2, ya.reshape(n, DA_WIDTH), yb.reshape(n, NSA_WIDTH), row(norm_mix[l]),
                    w_mg[l], wba[l], wbb[l], wo[l])
        x2 = _ffn(x2, row(norm_ffn2[l]), f2w1[l], f2w3[l], f2w2[l])
    return x2.reshape(b, s, d)
```

```python
import functools
import math

import numpy as np
import jax
import jax.numpy as jnp
from jax import lax
from jax.experimental import pallas as pl
from jax.experimental.pallas import tpu as pltpu

F32 = jnp.float32
BF16 = jnp.bfloat16

D_MODEL = 1024
DA_HEADS = 4
DA_HEAD_DIM = 64
DA_V_DIM = 2 * DA_HEAD_DIM
DA_WIDTH = DA_HEADS * DA_V_DIM
NSA_HEADS = 8
NSA_G = 2
NSA_P = NSA_HEADS // NSA_G
NSA_D = 64
NSA_WIDTH = NSA_HEADS * NSA_D
NSA_KV = NSA_G * NSA_D
CMP_BLOCK = 32
CMP_STRIDE = 16
CMP_HIDDEN = 128
SEL_BLOCK = 64
SEL_TOPK = 8
SEL_FORCED = 3
WINDOW = 512
N_BUCKETS = 32
MAX_DISTANCE = 1024
EPS = 1e-6
NEG = -1e30

LANES = 128
VMEM_LIMIT = 52 * 1024 * 1024

TILE = 256
FFN_ROWS = 512
CW = LANES
DA_UNROLL = 8
NSA_UNROLL = 4
GATE_ROWS = 16
ONES_ROWS = 16
DA_VT_ROWS = DA_V_DIM + ONES_ROWS
NSA_VT_ROWS = NSA_D + ONES_ROWS
LOG2E = math.log2(math.e)

COL_ROWS = DA_WIDTH + 2 * NSA_KV + NSA_G * GATE_ROWS


def _cparams(sem):
    return pltpu.CompilerParams(dimension_semantics=sem, vmem_limit_bytes=VMEM_LIMIT)


def _const_spec(shape):
    nd = len(shape)
    return pl.BlockSpec(shape, lambda *_: (0,) * nd)


def _rms(xf, g):
    ms = jnp.mean(xf * xf, axis=-1, keepdims=True)
    return xf * lax.rsqrt(ms + EPS) * g


def _dot(a, b):
    return jnp.dot(a, b, preferred_element_type=F32)


def _dot_nt(a, b):
    return lax.dot_general(a, b, (((1,), (1,)), ((), ())), preferred_element_type=F32)


def _bf16_pieces(x, parts):
    out = []
    r = x
    for i in range(parts):
        piece = r.astype(BF16)
        out.append(piece)
        if i + 1 < parts:
            r = r - piece.astype(F32)
    return out


def _ffn_kernel(x_ref, g_ref, w1_ref, w3_ref, w2_ref, o_ref):
    x = x_ref[...]
    h = _rms(x, g_ref[...]).astype(BF16)
    a = _dot(h, w1_ref[...])
    b = _dot(h, w3_ref[...])
    t = (jax.nn.silu(a) * b).astype(BF16)
    o_ref[...] = x + 0.5 * _dot(t, w2_ref[...])


def _resident_spec(shape):
    nd = len(shape)
    return pl.BlockSpec(shape, lambda *_: (0,) * nd, pipeline_mode=pl.Buffered(1))


def _ffn(x2, g, w1, w3, w2):
    n, d = x2.shape
    tm = FFN_ROWS
    return pl.pallas_call(
        _ffn_kernel,
        grid=(n // tm,),
        in_specs=[
            pl.BlockSpec((tm, d), lambda i: (i, 0)),
            _const_spec((1, d)),
            _resident_spec(w1.shape),
            _resident_spec(w3.shape),
            _resident_spec(w2.shape),
        ],
        out_specs=pl.BlockSpec((tm, d), lambda i: (i, 0)),
        out_shape=jax.ShapeDtypeStruct((n, d), F32),
        compiler_params=_cparams(("parallel",)),
        name="ffn_half_step",
    )(x2, g, w1, w3, w2)


def _group_rms(z, r_ref, gain):
    w = z.shape[-1]
    ss = _dot((z * z).astype(BF16), r_ref[0:w, 0:w])
    return z * lax.rsqrt(ss * (1.0 / 64.0) + EPS) * gain


def _proj_kernel(x_ref, g_ref, w_ref, wt_ref, r_ref, blk_ref, gq_ref, gk_ref, gnq_ref, gks_ref, gkw_ref,
                 dq_ref, dk_ref, nq_ref, kcvc_ref, kaug_ref, vta_ref, vtb_ref, gate_ref):
    h = _rms(x_ref[...], g_ref[...]).astype(BF16)
    z = _dot(h, w_ref[...])
    zt = _dot_nt(wt_ref[...], h)
    scale = DA_HEAD_DIM ** -0.5 * LOG2E
    dq_ref[...] = (_group_rms(z[:, 0:512], r_ref, gq_ref[...]) * scale).astype(BF16)
    dk_ref[...] = _group_rms(z[:, 512:1024], r_ref, gk_ref[...]).astype(BF16)
    low = lax.broadcasted_iota(jnp.int32, (z.shape[0], LANES), 1) < NSA_D

    def lane_blocks(v):
        out = []
        for c in range(v.shape[1] // LANES):
            pair = v[:, c * LANES:(c + 1) * LANES]
            out.append(jnp.where(low, pair, 0.0))
            out.append(jnp.where(low, pltpu.roll(pair, NSA_D, axis=1), 0.0))
        return out

    nq = _group_rms(z[:, 1024:1536], r_ref, gnq_ref[...]) * scale
    for hd, piece in enumerate(lane_blocks(nq)):
        nq_ref[:, hd * LANES:(hd + 1) * LANES] = piece.astype(BF16)
    for c, piece in enumerate(lane_blocks(z[:, 1536:1792])):
        kcvc_ref[:, c * LANES:(c + 1) * LANES] = piece
    blk = blk_ref[...].astype(F32)
    for g, piece in enumerate(lane_blocks(_group_rms(z[:, 1792:1920], r_ref, gks_ref[...]))):
        kaug_ref[0, 0, g] = (piece + blk).astype(BF16)
    for g, piece in enumerate(lane_blocks(_group_rms(z[:, 1920:2048], r_ref, gkw_ref[...]))):
        kaug_ref[0, 1, g] = piece.astype(BF16)
    ones = jnp.ones((ONES_ROWS, zt.shape[1]), BF16)
    for hd in range(DA_HEADS):
        vta_ref[0, hd * DA_VT_ROWS:hd * DA_VT_ROWS + DA_V_DIM] = zt[hd * DA_V_DIM:(hd + 1) * DA_V_DIM].astype(BF16)
        vta_ref[0, hd * DA_VT_ROWS + DA_V_DIM:(hd + 1) * DA_VT_ROWS] = ones
    for blk4 in range(2 * NSA_G):
        src = DA_WIDTH + (blk4 % 2) * NSA_KV + (blk4 // 2) * NSA_D
        vtb_ref[0, blk4 * NSA_VT_ROWS:blk4 * NSA_VT_ROWS + NSA_D] = zt[src:src + NSA_D].astype(BF16)
        vtb_ref[0, blk4 * NSA_VT_ROWS + NSA_D:(blk4 + 1) * NSA_VT_ROWS] = ones
    gate_ref[0] = jax.nn.sigmoid(zt[DA_WIDTH + 2 * NSA_KV:COL_ROWS])


def _proj(x2, g, w, wt, r, blk, gq, gk, gnq, gks, gkw, b, s):
    n, d = x2.shape
    tm = TILE
    nt = s // tm
    row = lambda wd: pl.BlockSpec((tm, wd), lambda i: (i, 0))
    col = lambda rows: pl.BlockSpec((1, rows, tm), lambda i: (i, 0, 0))
    return pl.pallas_call(
        _proj_kernel,
        grid=(n // tm,),
        in_specs=[row(d), _const_spec((1, d)), _const_spec(w.shape), _const_spec(wt.shape),
                  _const_spec(r.shape), pl.BlockSpec((tm, LANES), lambda i: (i % nt, 0)),
                  _const_spec((1, 512)), _const_spec((1, 512)), _const_spec((1, 512)),
                  _const_spec((1, NSA_KV)), _const_spec((1, NSA_KV))],
        out_specs=[row(512), row(512), row(NSA_HEADS * LANES), row(2 * NSA_G * LANES),
                   pl.BlockSpec((1, 2, NSA_G, tm, LANES), lambda i: (i // nt, 0, 0, i % nt, 0)),
                   col(DA_HEADS * DA_VT_ROWS), col(2 * NSA_G * NSA_VT_ROWS), col(NSA_G * GATE_ROWS)],
        out_shape=[
            jax.ShapeDtypeStruct((n, 512), BF16),
            jax.ShapeDtypeStruct((n, 512), BF16),
            jax.ShapeDtypeStruct((n, NSA_HEADS * LANES), BF16),
            jax.ShapeDtypeStruct((n, 2 * NSA_G * LANES), F32),
            jax.ShapeDtypeStruct((b, 2, NSA_G, s, LANES), BF16),
            jax.ShapeDtypeStruct((n // tm, DA_HEADS * DA_VT_ROWS, tm), BF16),
            jax.ShapeDtypeStruct((n // tm, 2 * NSA_G * NSA_VT_ROWS, tm), BF16),
            jax.ShapeDtypeStruct((n // tm, NSA_G * GATE_ROWS, tm), F32),
        ],
        compiler_params=_cparams(("parallel",)),
        name="norm_in_proj",
    )(x2, g, w, wt, r, blk, gq, gk, gnq, gks, gkw)


def _cmp_kernel(ck_ref, cv_ref, pek_ref, pev_ref, w1k_ref, w1v_ref, w2k_ref, w2vt_ref, gk_ref, kc_ref, vct_ref):
    nc = kc_ref.shape[2]

    def hidden(c_ref, pe_ref, w1_ref):
        lo = hi = None
        for t in range(CMP_STRIDE):
            x = c_ref[0, pl.ds(t, nc, stride=CMP_STRIDE), :]
            a = _dot((x + pe_ref[t:t + 1, :]).astype(BF16), w1_ref[t])
            b = _dot((x + pe_ref[CMP_STRIDE + t:CMP_STRIDE + t + 1, :]).astype(BF16), w1_ref[CMP_STRIDE + t])
            lo = a if lo is None else lo + a
            hi = b if hi is None else hi + b
        return jax.nn.gelu(lo + pltpu.roll(hi, nc - 1, axis=0)).astype(BF16)

    kc = _dot(hidden(ck_ref, pek_ref, w1k_ref), w2k_ref[...])
    ms = jnp.sum(kc * kc, axis=-1, keepdims=True) * (1.0 / NSA_D)
    kc_ref[0, 0] = (kc * lax.rsqrt(ms + EPS) * gk_ref[...]).astype(BF16)
    vct_ref[0, 0] = _dot_nt(w2vt_ref[...], hidden(cv_ref, pev_ref, w1v_ref)).astype(BF16)


def _compress(kcvc, pek, pev, w1k, w1v, w2k, w2vt, gk):
    b, s, _ = kcvc.shape
    g = NSA_G
    nc = s // CMP_STRIDE
    return pl.pallas_call(
        _cmp_kernel,
        grid=(b, g),
        in_specs=[pl.BlockSpec((1, s, LANES), lambda i, j: (i, 0, j)),
                  pl.BlockSpec((1, s, LANES), lambda i, j: (i, 0, NSA_G + j)),
                  _const_spec(pek.shape), _const_spec(pev.shape),
                  _const_spec(w1k.shape), _const_spec(w1v.shape),
                  _const_spec(w2k.shape), _const_spec(w2vt.shape), _const_spec(gk.shape)],
        out_specs=[pl.BlockSpec((1, 1, nc, LANES), lambda i, j: (i, j, 0, 0)),
                   pl.BlockSpec((1, 1, NSA_D, nc), lambda i, j: (i, j, 0, 0))],
        out_shape=[jax.ShapeDtypeStruct((b, g, nc, LANES), BF16),
                   jax.ShapeDtypeStruct((b, g, NSA_D, nc), BF16)],
        compiler_params=_cparams(("parallel", "parallel")),
        name="block_compress",
    )(kcvc, kcvc, pek, pev, w1k, w1v, w2k, w2vt, gk)


def _flash_list(n, qk_fn, vt_fn, bias_fn, state_fn, s_ref, mx_ref, p_ref, alpha_ref, bias_with_scores, unroll,
                before_loop=None):
    width = s_ref.shape[2]
    chunks = [slice(c * CW, (c + 1) * CW) for c in range(width // CW)]

    def scores(i, slot):
        raw = qk_fn(i)
        if not bias_with_scores:
            s_ref[slot] = raw
            return
        for cs in chunks:
            sb = raw[:, cs] + bias_fn(i, cs)
            s_ref[slot, :, cs] = sb
            mx_ref[slot, :, cs] = jnp.max(sb, axis=0, keepdims=True)

    def pv(i, slot):
        acc_ref = state_fn(i)[1]
        acc_ref[...] = alpha_ref[slot] * acc_ref[...] + _dot(vt_fn(i), p_ref[slot])

    def softmax(i, slot):
        m_ref = state_fn(i)[0]
        if bias_with_scores:
            m_old = m_ref[...]
            m_new = jnp.maximum(m_old, mx_ref[slot])
            alpha_ref[slot] = jnp.exp2(m_old - m_new)
            m_ref[...] = m_new
            for cs in chunks:
                p_ref[slot, :, cs] = jnp.exp2(s_ref[slot, :, cs] - m_new[:, cs]).astype(BF16)
            return
        for cs in chunks:
            s = s_ref[slot, :, cs] + bias_fn(i, cs)
            m_old = m_ref[:, cs]
            m_new = jnp.maximum(m_old, jnp.max(s, axis=0, keepdims=True))
            p_ref[slot, :, cs] = jnp.exp2(s - m_new).astype(BF16)
            alpha_ref[slot, :, cs] = jnp.exp2(m_old - m_new)
            m_ref[:, cs] = m_new

    scores(0, 0)
    softmax(0, 0)
    scores(1, 1)
    if before_loop is not None:
        before_loop()

    def body(g, carry):
        first = 1 + unroll * g
        for u in range(unroll):
            slot = (u + 1) % 2
            pv(first + u - 1, 1 - slot)
            softmax(first + u, slot)
            scores(first + u + 1, 1 - slot)
        return carry

    groups = (n - 1 + unroll - 1) // unroll
    lax.fori_loop(0, groups, body, 0)
    pv(unroll * groups, 0)


def _to_rows(xt):
    t = xt.shape[1]
    return jnp.concatenate([xt[:, c * LANES:(c + 1) * LANES].T for c in range(t // LANES)], axis=0)


def _da_kernel(lam_init, nd, nqt, qt_of, kt_of, q_ref, k_ref, vt_ref, tab_ref, lamv_ref, gs_ref, o_ref,
               qq_ref, s_ref, mx_ref, p_ref, alpha_ref, m_ref, acc_ref):
    t = TILE
    lane = lax.broadcasted_iota(jnp.int32, (t, LANES), 1)
    for qt in range(nqt):
        q = q_ref[0, qt * t:(qt + 1) * t, :].astype(F32)
        qq_ref[qt, 0:t, :] = jnp.where(lane < DA_HEAD_DIM, q, 0.0).astype(BF16)
        qq_ref[qt, t:2 * t, :] = jnp.where(lane >= DA_HEAD_DIM, q, 0.0).astype(BF16)
    m_ref[...] = jnp.full(m_ref.shape, NEG, F32)
    acc_ref[...] = jnp.zeros(acc_ref.shape, F32)

    key_tile = lambda i: jnp.minimum(kt_of[i], nqt - 1)

    def qk(i):
        return _dot_nt(k_ref[0, pl.ds(pl.multiple_of(key_tile(i) * t, t), t), :], qq_ref[qt_of[i]])

    def bias(i, cs):
        d = jnp.clip(qt_of[i] - kt_of[i], -1, nd)
        return tab_ref[0, d + 1, :, slice(cs.start % t, cs.start % t + CW)]

    _flash_list(nqt * (nqt + 1) // 2, qk, lambda i: vt_ref[0, key_tile(i)], bias,
                lambda i: (m_ref.at[qt_of[i]], acc_ref.at[qt_of[i]]), s_ref, mx_ref, p_ref, alpha_ref, True, DA_UNROLL)

    lv = lamv_ref[...]
    lam = (jnp.exp(jnp.sum(lv[0:1] * lv[1:2], axis=-1, keepdims=True))
           - jnp.exp(jnp.sum(lv[2:3] * lv[3:4], axis=-1, keepdims=True)) + lam_init)

    for qt in range(nqt):
        ot = acc_ref[qt, 0:DA_V_DIM, :] / acc_ref[qt, DA_V_DIM:DA_V_DIM + 1, :]
        y = _to_rows(ot[:, 0:t] - lam * ot[:, t:2 * t])
        o_ref[0, qt * t:(qt + 1) * t, :] = (_rms(y, gs_ref[...]) * (1.0 - lam_init)).astype(BF16)


def _diff_attention(dq, dk, vt, tab, lamv, gs, lam_init):
    b, s, _ = dq.shape
    t = TILE
    nqt = s // t
    nd = tab.shape[1] - 2
    pairs = [(qt, kt) for qt in range(nqt) for kt in range(qt + 1)] + [(nqt - 1, nqt)] * (DA_UNROLL + 2)
    qt_of = jnp.asarray(np.array([p[0] for p in pairs], np.int32))
    kt_of = jnp.asarray(np.array([p[1] for p in pairs], np.int32))
    seq = pl.BlockSpec((1, s, LANES), lambda i, h, *_: (i, 0, h))
    return pl.pallas_call(
        functools.partial(_da_kernel, lam_init, nd, nqt),
        grid_spec=pltpu.PrefetchScalarGridSpec(
            num_scalar_prefetch=2,
            grid=(b, DA_HEADS),
            in_specs=[
                seq, seq,
                pl.BlockSpec((1, nqt, DA_VT_ROWS, t), lambda i, h, *_: (i, 0, h, 0)),
                pl.BlockSpec((1,) + tab.shape[1:], lambda i, h, *_: (h, 0, 0, 0)),
                pl.BlockSpec(lamv.shape, lambda i, h, *_: (0, 0)),
                pl.BlockSpec(gs.shape, lambda i, h, *_: (0, 0)),
            ],
            out_specs=seq,
            scratch_shapes=[
                pltpu.VMEM((nqt, 2 * t, LANES), BF16),
                pltpu.VMEM((2, t, 2 * t), F32),
                pltpu.VMEM((2, 1, 2 * t), F32),
                pltpu.VMEM((2, t, 2 * t), BF16),
                pltpu.VMEM((2, 1, 2 * t), F32),
                pltpu.VMEM((nqt, 1, 2 * t), F32),
                pltpu.VMEM((nqt, DA_VT_ROWS, 2 * t), F32),
            ]),
        out_shape=jax.ShapeDtypeStruct((b, s, DA_WIDTH), BF16),
        compiler_params=_cparams(("parallel", "parallel")),
        name="diff_attention",
    )(qt_of, kt_of, dq, dk, vt, tab, lamv, gs)


def _nsa_kernel(nd, nqt, q_ref, kaug_ref, vt_ref, kc_ref, vct_ref, longc_ref, tab_ref, gate_ref,
                ovlt_ref, o_ref,
                qz_ref, qs_ref, s_ref, p_ref, alpha_ref, m_ref, acc_ref, comb_ref, psum_ref, sc_ref, pc_ref):
    t = TILE
    halves = t // CW
    nchunk = NSA_P * halves
    qi = pl.program_id(2)
    win_tiles = WINDOW // t
    win_edge = nd + 2

    def gate_row(r):
        return jnp.concatenate([gate_ref[0, 3 * p + r:3 * p + r + 1, :] for p in range(NSA_P)], axis=1)

    for p in range(NSA_P):
        qz_ref[p * t:(p + 1) * t, :] = q_ref[0, :, p * LANES:(p + 1) * LANES]

    def select_blocks():
        nc = sc_ref.shape[0]
        cstart = pl.multiple_of((nqt - 1 - qi) * (t // CMP_STRIDE), t // CMP_STRIDE)
        sc_ref[...] = _dot_nt(kc_ref[0, 0], qz_ref[...])
        for c in range(nchunk):
            p, half = divmod(c, halves)
            cs = slice(c * CW, (c + 1) * CW)
            hs = slice(half * CW, (half + 1) * CW)
            s = sc_ref[:, cs] + longc_ref[0, pl.ds(cstart, nc), cs]
            m = jnp.max(s, axis=0, keepdims=True)
            e = jnp.exp2(s - m)
            pc = jnp.where(m > 0.5 * NEG, e / jnp.sum(e, axis=0, keepdims=True), 0.0)
            pc_ref[:, cs] = pc.astype(BF16)
            if p == 0:
                psum_ref[:, hs] = pc
            else:
                psum_ref[:, hs] += pc
        comb_ref[...] = gate_row(0) * _dot(vct_ref[0, 0], pc_ref[...])
        select_topk()

    def select_topk():
        ovlt = ovlt_ref[...]
        imp = None
        for piece in _bf16_pieces(psum_ref[...], 3):
            d = _dot(ovlt, piece)
            imp = d if imp is None else imp + d
        nblk = imp.shape[0]
        jrow = lax.broadcasted_iota(jnp.int32, (nblk, t), 0)
        cur = (qi * t + lax.broadcasted_iota(jnp.int32, (nblk, t), 1)) // SEL_BLOCK
        forced = (jrow == 0) | (jrow == cur) | (jrow == cur - 1)
        lowest = jnp.float32(-3e38)
        score = jnp.where(forced | (jrow > cur), lowest, imp)
        jrow_f = jrow.astype(F32)
        nsel = jnp.where(forced, 0.0, 1.0)
        for _ in range(SEL_TOPK - SEL_FORCED):
            mx = jnp.max(score, axis=0, keepdims=True)
            first = jnp.min(jnp.where(score == mx, jrow_f, 2.0 * LANES), axis=0, keepdims=True)
            hit = jrow_f == first
            nsel = jnp.where(hit, 0.0, nsel)
            score = jnp.where(hit, lowest, score)
        parts = [jnp.zeros((NSA_D, t), F32), nsel]
        if LANES - NSA_D - nblk:
            parts.append(jnp.zeros((LANES - NSA_D - nblk, t), F32))
        nsel_rows = _to_rows(jnp.concatenate(parts, axis=0))
        for p in range(NSA_P):
            rs = slice(p * t, (p + 1) * t)
            qs_ref[rs, :] = (qz_ref[rs, :].astype(F32) + nsel_rows).astype(BF16)

    m_ref[...] = jnp.full(m_ref.shape, NEG, F32)
    acc_ref[...] = jnp.zeros(acc_ref.shape, F32)
    nwin = win_tiles + 1

    def entry(i):
        is_win = jnp.asarray(i < nwin)
        kt = jnp.where(is_win, qi - win_tiles + i, i - nwin)
        return is_win, kt, qi - kt

    def qk(i):
        is_win, kt, _ = entry(i)
        rows = pl.ds(pl.multiple_of(jnp.clip(kt, 0, qi) * t, t), t)
        q_all = qz_ref if isinstance(i, int) and i < nwin else qs_ref
        return _dot_nt(kaug_ref[0, is_win.astype(jnp.int32), 0, rows, :], q_all[...])

    def vt(i):
        is_win, kt, _ = entry(i)
        rows = pl.ds(pl.multiple_of(is_win.astype(jnp.int32) * NSA_VT_ROWS, NSA_VT_ROWS), NSA_VT_ROWS)
        return vt_ref[0, jnp.clip(kt, 0, qi), rows, :]

    def bias(i, cs):
        is_win, kt, d = entry(i)
        idx = jnp.where(is_win & (d == win_tiles), win_edge, jnp.clip(d, -1, nd) + 1)
        return tab_ref[0, jnp.where(kt < 0, 0, idx), :, cs]

    def state(i):
        st = jnp.asarray(i < nwin).astype(jnp.int32)
        return m_ref.at[st], acc_ref.at[st]

    _flash_list(nwin + qi + 1, qk, vt, bias, state, s_ref, None, p_ref, alpha_ref, False, NSA_UNROLL,
                before_loop=select_blocks)

    def branch_out(st):
        return acc_ref[st, 0:NSA_D, :] / acc_ref[st, NSA_D:NSA_D + 1, :]

    comb = comb_ref[...] + gate_row(1) * branch_out(0) + gate_row(2) * branch_out(1)

    for pair in range(NSA_P // 2):
        stacked = jnp.concatenate([comb[:, (2 * pair + h) * t:(2 * pair + h + 1) * t] for h in range(2)], axis=0)
        o_ref[0, :, pair * LANES:(pair + 1) * LANES] = _to_rows(stacked).astype(BF16)


def _nsa_attention(nq, kaug, vt, kc, vct, longc, tab, gates, ovlt):
    b, s, _ = nq.shape
    t = TILE
    nqt = s // t
    nc = kc.shape[2]
    nd = tab.shape[1] - 3
    gw = NSA_P * NSA_D
    return pl.pallas_call(
        functools.partial(_nsa_kernel, nd, nqt),
        grid=(b, NSA_G, nqt),
        in_specs=[
            pl.BlockSpec((1, t, NSA_P * LANES), lambda i, g, j: (i, j, g)),
            pl.BlockSpec((1, 2, 1, s, LANES), lambda i, g, j: (i, 0, g, 0, 0)),
            pl.BlockSpec((1, nqt, 2 * NSA_VT_ROWS, t), lambda i, g, j: (i, 0, g, 0)),
            pl.BlockSpec((1, 1, nc, LANES), lambda i, g, j: (i, g, 0, 0)),
            pl.BlockSpec((1, 1, NSA_D, nc), lambda i, g, j: (i, g, 0, 0)),
            pl.BlockSpec((1,) + longc.shape[1:], lambda i, g, j: (g, 0, 0)),
            pl.BlockSpec((1,) + tab.shape[1:], lambda i, g, j: (g, 0, 0, 0)),
            pl.BlockSpec((1, GATE_ROWS, t), lambda i, g, j: (i * nqt + j, g, 0)),
            _const_spec(ovlt.shape),
        ],
        out_specs=pl.BlockSpec((1, t, gw), lambda i, g, j: (i, j, g)),
        out_shape=jax.ShapeDtypeStruct((b, s, NSA_WIDTH), BF16),
        scratch_shapes=[
            pltpu.VMEM((NSA_P * t, LANES), BF16),
            pltpu.VMEM((NSA_P * t, LANES), BF16),
            pltpu.VMEM((2, t, NSA_P * t), F32),
            pltpu.VMEM((2, t, NSA_P * t), BF16),
            pltpu.VMEM((2, 1, NSA_P * t), F32),
            pltpu.VMEM((2, 1, NSA_P * t), F32),
            pltpu.VMEM((2, NSA_VT_ROWS, NSA_P * t), F32),
            pltpu.VMEM((NSA_D, NSA_P * t), F32),
            pltpu.VMEM((nc, t), F32),
            pltpu.VMEM((nc, NSA_P * t), F32),
            pltpu.VMEM((nc, NSA_P * t), BF16),
        ],
        compiler_params=_cparams(("parallel", "parallel", "arbitrary")),
        name="nsa_attention",
    )(nq, kaug, vt, kc, vct, longc, tab, gates, ovlt)


def _merge_kernel(x_ref, ya_ref, yb_ref, g_ref, wmg_ref, pa_ref, pb_ref, wo_ref, o_ref):
    x = x_ref[...]
    h = _rms(x, g_ref[...]).astype(BF16)
    ya = ya_ref[...]
    yb = yb_ref[...]
    half = D_MODEL // 2
    out = x
    for c0 in (0, half):
        ga = jax.nn.sigmoid(_dot(h, wmg_ref[:, c0:c0 + half]))
        gb = jax.nn.sigmoid(_dot(h, wmg_ref[:, D_MODEL + c0:D_MODEL + c0 + half]))
        merged = ga * _dot(ya, pa_ref[:, c0:c0 + half]) + gb * _dot(yb, pb_ref[:, c0:c0 + half])
        out = out + _dot(merged.astype(BF16), wo_ref[c0:c0 + half, :])
    o_ref[...] = out


def _merge(x2, ya, yb, g, wmg, pa, pb, wo):
    n, d = x2.shape
    tm = FFN_ROWS
    row = lambda wd: pl.BlockSpec((tm, wd), lambda i: (i, 0))
    return pl.pallas_call(
        _merge_kernel,
        grid=(n // tm,),
        in_specs=[row(d), row(DA_WIDTH), row(NSA_WIDTH), _const_spec((1, d)),
                  _resident_spec(wmg.shape), _resident_spec(pa.shape), _resident_spec(pb.shape),
                  _resident_spec(wo.shape)],
        out_specs=row(d),
        out_shape=jax.ShapeDtypeStruct((n, d), F32),
        compiler_params=_cparams(("parallel",)),
        name="gated_merge_out_proj",
    )(x2, ya, yb, g, wmg, pa, pb, wo)


def _rel_bucket(dist):
    n = jnp.maximum(dist, 0)
    max_exact = N_BUCKETS // 2
    nf = jnp.maximum(n, 1).astype(jnp.float32)
    large = max_exact + (jnp.log(nf / max_exact) / math.log(MAX_DISTANCE / max_exact)
                         * (N_BUCKETS - max_exact)).astype(jnp.int32)
    large = jnp.minimum(large, N_BUCKETS - 1)
    return jnp.where(n < max_exact, n, large)


def _bias_of(table, dist):
    bucket = _rel_bucket(dist)
    col = lambda v: v.reshape((-1,) + (1,) * dist.ndim)
    out = jnp.zeros((table.shape[1],) + dist.shape, F32)
    for bkt in range(N_BUCKETS):
        out = jnp.where(bucket == bkt, col(table[bkt].astype(F32)), out)
    return jnp.where(dist >= 0, out, NEG)


def _far_tiles(t):
    return -(-(MAX_DISTANCE + t - 1) // t)


def _tile_bias(table, t, window_edge):
    nd = _far_tiles(t)
    d0 = 2 * t - 1
    width = (nd + 3) * t - 1
    v = _bias_of(table, jnp.arange(width) - d0)
    w = jnp.concatenate([v, jnp.zeros((v.shape[0], 1), F32)], axis=1)
    m = jnp.tile(w, (1, t))[:, :t * width].reshape(-1, t, width)
    bias = jnp.stack([m[:, :, d * t + d0:d * t + d0 + t] for d in range(-1, nd + 1)], axis=1)
    if window_edge:
        we = WINDOW // t
        j = jnp.arange(t)[:, None]
        i = jnp.arange(t)[None, :]
        edge = jnp.where(we * t + i - j < WINDOW, bias[:, we + 1], NEG)
        bias = jnp.concatenate([bias, edge[:, None]], axis=1)
    return bias


def _cmp_bias(table, t, nqt):
    cpt = t // CMP_STRIDE
    r = jnp.arange((2 * nqt - 1) * cpt)[:, None]
    dist = (nqt - 1 - r // cpt) * t + jnp.arange(t)[None, :] - ((r % cpt) * CMP_STRIDE + CMP_BLOCK - 1)
    return _bias_of(table, dist)


def _lanes_by_head(x):
    x = x.reshape((NSA_G, NSA_P) + x.shape[1:])
    x = jnp.moveaxis(x, 1, -2)
    return x.reshape(x.shape[:-2] + (NSA_P * x.shape[-1],))


def _static_tables(s):
    nc = s // CMP_STRIDE
    n_cmp = nc - CMP_BLOCK // CMP_STRIDE + 1
    nblk = s // SEL_BLOCK
    ovlt = np.zeros((nblk, nc), np.float32)
    for c in range(n_cmp):
        for jb in range(nblk):
            if c * CMP_STRIDE < (jb + 1) * SEL_BLOCK and c * CMP_STRIDE + CMP_BLOCK - 1 >= jb * SEL_BLOCK:
                ovlt[jb, c] = 1.0
    blk = np.zeros((s, LANES), np.float32)
    blk[np.arange(s), NSA_D + np.arange(s) // SEL_BLOCK] = NEG
    r = np.kron(np.eye(512 // 64, dtype=np.float32), np.ones((64, 64), np.float32))
    return jnp.asarray(ovlt, BF16), jnp.asarray(blk, BF16), jnp.asarray(r, BF16)


def _pad_blocks(w, axis=-1):
    w = jnp.moveaxis(w, axis, -1)
    w = w.reshape(w.shape[:-1] + (w.shape[-1] // NSA_D, NSA_D))
    w = jnp.pad(w, [(0, 0)] * (w.ndim - 1) + [(0, LANES - NSA_D)])
    return jnp.moveaxis(w.reshape(w.shape[:-2] + (-1,)), -1, axis)


def kernel(x, w_in, w_branch_a, w_branch_b, w_out, norm_ffn1, norm_mix, norm_ffn2, ffn1_w1, ffn1_w3, ffn1_w2, ffn2_w1, ffn2_w3, ffn2_w2, da_q_gain, da_k_gain, da_lambda_q1, da_lambda_k1, da_lambda_q2, da_lambda_k2, da_subln_gain, nsa_q_gain, nsa_k_gain, cmp_pe_k, cmp_w1_k, cmp_w2_k, cmp_pe_v, cmp_w1_v, cmp_w2_v, rel_bias_table):
    b, s, d = x.shape
    depth = w_in.shape[0]
    t = TILE
    assert d == D_MODEL and s % t == 0 and WINDOW % t == 0 and s >= WINDOW and (b * s) % FFN_ROWS == 0
    assert SEL_TOPK <= s // SEL_BLOCK <= LANES - NSA_D
    n = b * s
    nqt = s // t

    ovlt, blk, r512 = _static_tables(s)
    tab_a = _tile_bias(rel_bias_table[:, :DA_HEADS], t, False) * LOG2E
    tab_b = _lanes_by_head(_tile_bias(rel_bias_table[:, DA_HEADS:], t, True)) * LOG2E
    longc = _lanes_by_head(_cmp_bias(rel_bias_table[:, DA_HEADS:], t, nqt)) * LOG2E

    bf = lambda w: w.astype(BF16)
    f1w1, f1w3, f1w2, f2w1, f2w3, f2w2 = map(bf, (ffn1_w1, ffn1_w3, ffn1_w2, ffn2_w1, ffn2_w3, ffn2_w2))
    wba, wbb, wo = bf(w_branch_a), bf(w_branch_b), bf(w_out)
    w_row = bf(jnp.concatenate([w_in[:, :, 0:1024], w_in[:, :, 1536:2304], w_in[:, :, 2304:2432], w_in[:, :, 2560:2688]],
                               axis=2))
    w_ng = jnp.pad(w_in[:, :, 2816:2840].reshape(depth, d, NSA_G, NSA_P * 3),
                   ((0, 0), (0, 0), (0, 0), (0, GATE_ROWS - NSA_P * 3))).reshape(depth, d, NSA_G * GATE_ROWS)
    w_col = bf(jnp.swapaxes(jnp.concatenate([w_in[:, :, 1024:1536], w_in[:, :, 2432:2560], w_in[:, :, 2688:2816], w_ng],
                                            axis=2), 1, 2))
    w_mg = bf(w_in[:, :, 2840:])
    cw1 = lambda w: bf(_pad_blocks(w.reshape(depth, CMP_BLOCK, NSA_D, CMP_HIDDEN), axis=2))
    cw1k, cw1v = cw1(cmp_w1_k), cw1(cmp_w1_v)
    pek, pev = _pad_blocks(cmp_pe_k.astype(F32)), _pad_blocks(cmp_pe_v.astype(F32))
    cw2k = bf(_pad_blocks(cmp_w2_k))
    cw2vt = bf(jnp.swapaxes(cmp_w2_v, 1, 2))

    x2 = x.reshape(n, d)
    row = lambda v: v.reshape(1, -1).astype(F32)
    tile = lambda v, k: jnp.tile(v.astype(F32), k).reshape(1, -1)
    for l in range(depth):
        x2 = _ffn(x2, row(norm_ffn1[l]), f1w1[l], f1w3[l], f1w2[l])

        dq, dk, nq, kcvc, kaug, vta, vtb, gates = _proj(
            x2, row(norm_mix[l]), w_row[l], w_col[l], r512, blk,
            tile(da_q_gain[l], 8), tile(da_k_gain[l], 8), tile(nsa_q_gain[l], 8),
            tile(nsa_k_gain[l, 1], NSA_G), tile(nsa_k_gain[l, 2], NSA_G), b, s)
        vta = vta.reshape(b, nqt, DA_HEADS * DA_VT_ROWS, t)
        vtb = vtb.reshape(b, nqt, 2 * NSA_G * NSA_VT_ROWS, t)

        lam_init = 0.8 - 0.6 * math.exp(-0.3 * l)
        lamv = jnp.stack([da_lambda_q1[l], da_lambda_k1[l], da_lambda_q2[l], da_lambda_k2[l]]).astype(F32)
        ya = _diff_attention(dq.reshape(b, s, 512), dk.reshape(b, s, 512), vta,
                             tab_a, lamv, row(da_subln_gain[l]), lam_init)

        kc, vct = _compress(kcvc.reshape(b, s, 2 * NSA_G * LANES), pek[l], pev[l], cw1k[l], cw1v[l], cw2k[l], cw2vt[l],
                            _pad_blocks(row(nsa_k_gain[l, 0])))
        yb = _nsa_attention(nq.reshape(b, s, NSA_HEADS * LANES), kaug, vtb, kc, vct, longc, tab_b, gates, ovlt)

        x2 = _merge(x2, ya.reshape(n, DA_WIDTH), yb.reshape(n, NSA_WIDTH), row(norm_mix[l]),
                    w_mg[l], wba[l], wbb[l], wo[l])
        x2 = _ffn(x2, row(norm_ffn2[l]), f2w1[l], f2w3[l], f2w2[l])
    return x2.reshape(b, s, d)
```
